```python
import jax
import jax.numpy as jnp
from jax import lax
import numpy as np

D_MODEL = 2048
BATCH = 4
SEQ = 4096
DEPTH = 2
DEC_BATCH = 128
DEC_SEQ = 8
PAST_LEN = 16384
PAGE_SIZE = 128

N_EVEN = (DEPTH + 1) // 2
N_ODD = DEPTH // 2
QBLOCK = 128
NEG = -1e30
EPS = 1e-6
MLA_HEADS = 8
Q_RANK = 512
KV_RANK = 256
NOPE_DIM = 128
ROPE_DIM = 64
MLA_VDIM = 128
ROPE_THETA = 10000.0
ML_HEADS = 4
ML_QK = 128
ML_V = 256
ML_CHUNK = 64
SB_HEADS = 8
SB_KV_HEADS = 4
SB_HD = 64
NSA_HEADS = 16
NSA_HD = 64
CMP_STRIDE = 16
CMP_LEN = 2 * CMP_STRIDE
CMP_HID = 128
SEL_BLOCK = 64
N_SEL = 16
WINDOW = 512
FORCE_BONUS = 1000.0
N_MEM = 256
X_HEADS = 4
X_HD = 128
N_GROUPS = 4
N_EXP = 8
TOP_K = 2
D_EXP = 512

EVEN_COLS = (Q_RANK, KV_RANK, ROPE_DIM, ML_HEADS * ML_QK, ML_HEADS * ML_QK, ML_HEADS * ML_V, ML_HEADS, ML_HEADS, ML_HEADS * ML_V)
D_IN_EVEN = Q_RANK + KV_RANK + ROPE_DIM + 2 * ML_HEADS * ML_QK + 2 * ML_HEADS * ML_V + 2 * ML_HEADS
D_OUT_EVEN = MLA_HEADS * MLA_VDIM + ML_HEADS * ML_V
ODD_COLS = (SB_HEADS * SB_HD, SB_KV_HEADS * SB_HD, SB_KV_HEADS * SB_HD, NSA_HEADS * NSA_HD, 4 * NSA_HD, 2 * NSA_HD, 3 * NSA_HEADS)
D_IN_ODD = SB_HEADS * SB_HD + 2 * SB_KV_HEADS * SB_HD + NSA_HEADS * NSA_HD + 6 * NSA_HD + 3 * NSA_HEADS
D_OUT_ODD = SB_HEADS * SB_HD + NSA_HEADS * NSA_HD

kernel_name = 'hybrid_mla_mlstm_sb_nsa_hmoe_step'


def rmsnorm(x, g):
    xf = x.astype(jnp.float32)
    y = xf * lax.rsqrt(jnp.mean(xf * xf, axis=-1, keepdims=True) + EPS)
    return (y * g.astype(jnp.float32)).astype(x.dtype)


def split_cols(a, widths):
    return jnp.split(a, np.cumsum(widths)[:-1].tolist(), axis=-1)


def qblock(t):
    return min(QBLOCK, t)


def map_query_blocks(fn, arrays, block):
    t = arrays[0].shape[1]
    nb = -(-t // block)
    pad = nb * block - t
    def to_blocks(a):
        a = jnp.pad(a, [(0, 0), (0, pad)] + [(0, 0)] * (a.ndim - 2))
        a = a.reshape((a.shape[0], nb, block) + a.shape[2:])
        return jnp.moveaxis(a, 1, 0)
    xs = (jnp.arange(nb),) + tuple(to_blocks(a) for a in arrays)
    out = lax.map(lambda args: fn(*args), xs)
    out = jnp.moveaxis(out, 0, 1)
    out = out.reshape((out.shape[0], nb * block) + out.shape[3:])
    return out[:, :t]


def rope_angles(pos, dim):
    inv = ROPE_THETA ** (-jnp.arange(0, dim, 2, dtype=jnp.float32) / dim)
    ang = pos.astype(jnp.float32)[:, None] * inv[None, :]
    return jnp.cos(ang), jnp.sin(ang)


def apply_rope(x, cos, sin):
    half = x.shape[-1] // 2
    x1 = x[..., :half].astype(jnp.float32)
    x2 = x[..., half:].astype(jnp.float32)
    return jnp.concatenate([x1 * cos - x2 * sin, x1 * sin + x2 * cos], axis=-1).astype(x.dtype)


def alibi_slopes(n):
    return 2.0 ** (-8.0 * jnp.arange(1, n + 1, dtype=jnp.float32) / n)


def gather_pages(pool, page_table):
    g = pool[page_table]
    return g.reshape((page_table.shape[0], page_table.shape[1] * pool.shape[1]) + pool.shape[2:])


def mla_mixer(c_q, c_kv_raw, k_r_raw, pos, past, g_cq, w_uq, g_ckv, w_uk, w_uv):
    b, t, _ = c_q.shape
    cos, sin = rope_angles(pos, ROPE_DIM)
    q = jnp.einsum('btr,rhd->bthd', rmsnorm(c_q, g_cq), w_uq)
    q_nope = q[..., :NOPE_DIM]
    q_rope = apply_rope(q[..., NOPE_DIM:], cos[None, :, None, :], sin[None, :, None, :])
    q_lat = jnp.einsum('bthd,rhd->bthr', q_nope, w_uk)
    c_kv = rmsnorm(c_kv_raw, g_ckv)
    k_r = apply_rope(k_r_raw, cos[None], sin[None])
    new_rows = jnp.concatenate([c_kv, k_r], axis=-1)
    keys = jnp.concatenate([past, new_rows], axis=1)
    k_pos = jnp.arange(keys.shape[1])
    lat, kr = keys[..., :KV_RANK], keys[..., KV_RANK:]
    scale = (NOPE_DIM + ROPE_DIM) ** -0.5
    def blk(bi, ql, qr, qp):
        s = (jnp.einsum('bqhr,bsr->bhqs', ql, lat) + jnp.einsum('bqhd,bsd->bhqs', qr, kr)).astype(jnp.float32) * scale
        mask = k_pos[None, :] <= qp[0][:, None]
        p = jax.nn.softmax(jnp.where(mask, s, NEG), axis=-1).astype(lat.dtype)
        return jnp.einsum('bhqs,bsr->bqhr', p, lat)
    o_lat = map_query_blocks(blk, (q_lat, q_rope, pos[None]), qblock(t))
    out = jnp.einsum('bthr,rhd->bthd', o_lat, w_uv).reshape(b, t, MLA_HEADS * MLA_VDIM)
    return out, new_rows


def mlstm_mixer(q, k, v, ig, fg, og, c0, n0, m0, g_mh):
    b, t = q.shape[:2]
    dt = q.dtype
    q = q.reshape(b, t, ML_HEADS, ML_QK)
    k = k.reshape(b, t, ML_HEADS, ML_QK) * (ML_QK ** -0.5)
    v = v.reshape(b, t, ML_HEADS, ML_V)
    li = ig.astype(jnp.float32)
    lf = jax.nn.log_sigmoid(fg.astype(jnp.float32))
    L = min(ML_CHUNK, t)
    nc = -(-t // L)
    pad = nc * L - t
    def chunks(a, val=0.0):
        a = jnp.pad(a, [(0, 0), (0, pad)] + [(0, 0)] * (a.ndim - 2), constant_values=val)
        return jnp.moveaxis(a.reshape((b, nc, L) + a.shape[2:]), 1, 0)
    tri = jnp.tril(jnp.ones((L, L), dtype=bool))
    def step(carry, xs):
        c, n, m = carry
        qc, kc, vc, lic, lfc = xs
        cb = jnp.cumsum(lfc, axis=1)
        a = cb + m[:, None, :]
        d = cb[:, :, None, :] - cb[:, None, :, :] + lic[:, None, :, :]
        d = jnp.where(tri[None, :, :, None], d, NEG)
        mt = jnp.maximum(a, d.max(axis=2))
        w = jnp.exp(d - mt[:, :, None, :])
        inter = jnp.exp(a - mt)
        sc = w * jnp.einsum('bthd,bshd->btsh', qc, kc).astype(jnp.float32)
        num = jnp.einsum('btsh,bshv->bthv', sc, vc) + inter[..., None] * jnp.einsum('bhvd,bthd->bthv', c, qc)
        den = sc.sum(axis=2) + inter * jnp.einsum('bhd,bthd->bth', n, qc)
        h = num / jnp.maximum(jnp.abs(den), jnp.exp(-mt))[..., None]
        wl, il = w[:, -1], inter[:, -1]
        c = il[..., None, None] * c + jnp.einsum('bsh,bshv,bshd->bhvd', wl, vc, kc)
        n = il[..., None] * n + jnp.einsum('bsh,bshd->bhd', wl, kc)
        return (c, n, mt[:, -1]), h
    carry0 = (c0.astype(jnp.float32), n0.astype(jnp.float32), m0.astype(jnp.float32))
    (c, n, m), h = lax.scan(step, carry0, (chunks(q), chunks(k), chunks(v), chunks(li, NEG), chunks(lf)))
    h = jnp.moveaxis(h, 0, 1).reshape(b, nc * L, ML_HEADS, ML_V)[:, :t]
    h = h - h.mean(axis=-1, keepdims=True)
    h = h * lax.rsqrt(jnp.mean(h * h, axis=-1, keepdims=True) + EPS) * g_mh.astype(jnp.float32)
    out = jax.nn.sigmoid(og.astype(jnp.float32)).reshape(b, t, ML_HEADS, ML_V) * h
    return out.reshape(b, t, ML_HEADS * ML_V).astype(dt), c, n, m


def sb_mixer(q, k_new, v_new, pos, past_kv):
    b, t = q.shape[:2]
    q = q.reshape(b, t, SB_KV_HEADS, SB_HEADS // SB_KV_HEADS, SB_HD)
    new_rows = jnp.stack([k_new.reshape(b, t, SB_KV_HEADS, SB_HD), v_new.reshape(b, t, SB_KV_HEADS, SB_HD)], axis=2)
    kv = jnp.concatenate([past_kv, new_rows], axis=1)
    k, v = kv[:, :, 0], kv[:, :, 1]
    k_pos = jnp.arange(kv.shape[1])
    def blk(bi, qh, qp):
        z = jnp.einsum('bqgrd,bsgd->bgrqs', qh, k).astype(jnp.float32) * (SB_HD ** -0.5)
        mask = k_pos[None, :] < qp[0][:, None]
        l1mb = jnp.where(mask, jax.nn.log_sigmoid(-z), 0.0)
        after = lax.cumsum(l1mb, axis=z.ndim - 1, reverse=True) - l1mb
        att = jnp.where(mask, jnp.exp(jax.nn.log_sigmoid(z) + after), 0.0).astype(v.dtype)
        return jnp.einsum('bgrqs,bsgd->bqgrd', att, v)
    o = map_query_blocks(blk, (q, pos[None]), qblock(t))
    return o.reshape(b, t, SB_HEADS * SB_HD), new_rows


def nsa_mixer(q, kv_new, win_new, gate, pos, past_kv, win_prior, pe, wc1, wc2):
    b, t = q.shape[:2]
    past = past_kv.shape[1]
    s_len = past + t
    scale = NSA_HD ** -0.5
    slopes = alibi_slopes(NSA_HEADS)
    kv_all = jnp.concatenate([past_kv, kv_new], axis=1)
    n_chunk = max(-(-s_len // CMP_STRIDE), 2)
    ck = jnp.pad(kv_all[:, :, :2], ((0, 0), (0, n_chunk * CMP_STRIDE - s_len), (0, 0), (0, 0)))
    ck = ck.reshape(b, n_chunk, CMP_STRIDE, 2, NSA_HD)
    w1 = wc1.reshape(2, 2, CMP_STRIDE, NSA_HD, CMP_HID)
    proj = jnp.einsum('bcrkd,kzrdh->bczkh', ck, w1)
    pe_term = jnp.einsum('kpd,kpdh->kh', pe, wc1.reshape(2, CMP_LEN, NSA_HD, CMP_HID))
    hid = jax.nn.gelu(proj[:, :-1, 0] + proj[:, 1:, 1] + pe_term)
    cmp = jnp.einsum('bnkh,khd->bnkd', hid, wc2)
    k_cmp, v_cmp = cmp[:, :, 0], cmp[:, :, 1]
    cmp_start = CMP_STRIDE * jnp.arange(n_chunk - 1)
    cmp_end = cmp_start + CMP_LEN - 1
    n_sel = -(-s_len // SEL_BLOCK)
    sel = jnp.pad(kv_all[:, :, 2:], ((0, 0), (0, n_sel * SEL_BLOCK - s_len), (0, 0), (0, 0)))
    sel = sel.reshape(b, n_sel, SEL_BLOCK, 2, NSA_HD)
    sel_start = SEL_BLOCK * jnp.arange(n_sel)
    overlap = ((cmp_start[:, None] < sel_start[None, :] + SEL_BLOCK) & (cmp_start[:, None] + CMP_LEN > sel_start[None, :])).astype(jnp.float32)
    k_top = min(N_SEL, n_sel)
    bidx = jnp.arange(b)[:, None, None]
    pw = win_prior.shape[1]
    qb = qblock(t)
    nb = -(-t // qb)
    band = jnp.concatenate([jnp.zeros((b, WINDOW - pw, 2, NSA_HD), win_new.dtype), win_prior, win_new, jnp.zeros((b, nb * qb - t, 2, NSA_HD), win_new.dtype)], axis=1)
    band_pos = past - WINDOW + jnp.arange(WINDOW + nb * qb)
    def blk(bi, qh, gh, qp):
        tq = qp[0]
        valid_c = cmp_end[None, :] <= tq[:, None]
        dist_c = (tq[:, None] - cmp_end[None, :]).astype(jnp.float32)
        s_c = jnp.einsum('bqhd,bnd->bhqn', qh, k_cmp).astype(jnp.float32) * scale - slopes[:, None, None] * dist_c
        p_c = jnp.where(valid_c, jax.nn.softmax(jnp.where(valid_c, s_c, NEG), axis=-1), 0.0)
        o_c = jnp.einsum('bhqn,bnd->bqhd', p_c.astype(v_cmp.dtype), v_cmp)
        imp = jnp.einsum('bhqn,nj->bqj', p_c, overlap)
        cur = tq // SEL_BLOCK
        j = jnp.arange(n_sel)
        forced = (j[None, :] == 0) | (j[None, :] == cur[:, None]) | (j[None, :] == cur[:, None] - 1)
        cand = sel_start[None, :] <= tq[:, None]
        score = jnp.where(cand[None], imp + jnp.where(forced, FORCE_BONUS, 0.0)[None], NEG)
        top_s, idx = lax.top_k(score, k_top)
        kv_sel = sel[bidx, idx]
        kpos = idx[..., None] * SEL_BLOCK + jnp.arange(SEL_BLOCK)
        ok = (top_s > 0.5 * NEG)[..., None] & (kpos <= tq[None, :, None, None])
        dist_s = (tq[None, :, None, None] - kpos).astype(jnp.float32)
        s_s = jnp.einsum('bqhd,bqkld->bqhkl', qh, kv_sel[..., 0, :]).astype(jnp.float32) * scale - slopes[None, None, :, None, None] * dist_s[:, :, None]
        s_s = jnp.where(ok[:, :, None], s_s, NEG)
        p_s = jax.nn.softmax(s_s.reshape(s_s.shape[:3] + (-1,)), axis=-1).reshape(s_s.shape)
        o_s = jnp.einsum('bqhkl,bqkld->bqhd', p_s.astype(kv_sel.dtype), kv_sel[..., 1, :])
        bw = lax.dynamic_slice_in_dim(band, bi * qb, qb + WINDOW, axis=1)
        bp = lax.dynamic_slice_in_dim(band_pos, bi * qb, qb + WINDOW, axis=0)
        dist_w = tq[:, None] - bp[None, :]
        ok_w = (bp[None, :] >= 0) & (dist_w >= 0) & (dist_w < WINDOW)
        s_w = jnp.einsum('bqhd,bsd->bhqs', qh, bw[:, :, 0]).astype(jnp.float32) * scale - slopes[:, None, None] * dist_w.astype(jnp.float32)
        p_w = jax.nn.softmax(jnp.where(ok_w, s_w, NEG), axis=-1)
        o_w = jnp.einsum('bhqs,bsd->bqhd', p_w.astype(bw.dtype), bw[:, :, 1])
        g = jax.nn.sigmoid(gh.astype(jnp.float32))
        return (g[..., 0:1] * o_c + g[..., 1:2] * o_s + g[..., 2:3] * o_w).astype(qh.dtype)
    o = map_query_blocks(blk, (q, gate, pos[None]), qb)
    new_win = jnp.concatenate([win_prior, win_new], axis=1)[:, -min(WINDOW, pw + t):]
    return o.reshape(b, t, NSA_HEADS * NSA_HD), kv_new, new_win


def mem_kv_rows(mem, g, wk, wv):
    b = mem.shape[0]
    h = rmsnorm(mem, g)
    k = (h @ wk).reshape(b, -1, X_HEADS, X_HD)
    v = (h @ wv).reshape(b, -1, X_HEADS, X_HD)
    return jnp.stack([k, v], axis=2)


def mem_attend(h, mkv, wq, wo):
    b, t, _ = h.shape
    q = (h @ wq).reshape(b, t, X_HEADS, X_HD)
    s = jnp.einsum('bthd,bmhd->bhtm', q, mkv[:, :, 0]).astype(jnp.float32) * (X_HD ** -0.5)
    p = jax.nn.softmax(s, axis=-1).astype(mkv.dtype)
    o = jnp.einsum('bhtm,bmhd->bthd', p, mkv[:, :, 1]).reshape(b, t, X_HEADS * X_HD)
    return o @ wo


def moe(h, w_rg, b_rg, w_re, b_re, w_e1, w_e3, w_e2):
    b, t, d = h.shape
    hf = h.reshape(b * t, d)
    lg = (hf @ w_rg).astype(jnp.float32) + b_rg.astype(jnp.float32)
    pg = jax.nn.softmax(lg, axis=-1)
    g1h = jax.nn.one_hot(jnp.argmax(lg, axis=-1), N_GROUPS, dtype=jnp.float32)
    gw = jnp.sum(pg * g1h, axis=-1)
    le = ((hf @ w_re).astype(jnp.float32) + b_re.astype(jnp.float32)).reshape(-1, N_GROUPS, N_EXP)
    le_g = jnp.einsum('nge,ng->ne', le, g1h)
    top_v, top_i = lax.top_k(le_g, TOP_K)
    we = jax.nn.softmax(top_v, axis=-1)
    ew = jnp.einsum('nk,nke->ne', we, jax.nn.one_hot(top_i, N_EXP, dtype=jnp.float32)) * gw[:, None]
    gate = (g1h[:, :, None] * ew[:, None, :]).astype(hf.dtype)
    out = jnp.zeros_like(hf)
    for g in range(N_GROUPS):
        a = jnp.einsum('nd,edf->nef', hf, w_e1[g])
        c = jnp.einsum('nd,edf->nef', hf, w_e3[g])
        hid = jax.nn.silu(a) * c * gate[:, g, :, None]
        out = out + jnp.einsum('nef,efd->nd', hid, w_e2[g])
    return out.reshape(b, t, d)


def trunk(x, start, mla_past, ml_state, sb_past, nsa_past, win_prior, mem_kv, p):
    b, t, _ = x.shape
    pos = start + jnp.arange(t, dtype=jnp.int32)
    mla_rows, ml_c, ml_n, ml_m, sb_rows, nsa_rows, wins = [], [], [], [], [], [], []
    for l in range(DEPTH):
        e = l // 2
        h = rmsnorm(x, p['g_mix'][l])
        if l % 2 == 0:
            c_q, c_kv, k_r, mq, mk, mv, mi, mf, mo = split_cols(h @ p['w_in_even'][e], EVEN_COLS)
            o_a, rows = mla_mixer(c_q, c_kv, k_r, pos, mla_past[e], p['g_cq'][e], p['w_uq'][e], p['g_ckv'][e], p['w_uk'][e], p['w_uv'][e])
            c0, n0, m0 = ml_state[e]
            o_b, c, n, m = mlstm_mixer(mq, mk, mv, mi + p['b_ml_i'][e], mf + p['b_ml_f'][e], mo, c0, n0, m0, p['g_mh'][e])
            mla_rows.append(rows)
            ml_c.append(c)
            ml_n.append(n)
            ml_m.append(m)
            x = x + jnp.concatenate([o_a, o_b], axis=-1) @ p['w_out_even'][e]
        else:
            sq, sk, sv, nq, nkv, nwin, ngate = split_cols(h @ p['w_in_odd'][e], ODD_COLS)
            o_c, srows = sb_mixer(sq, sk, sv, pos, sb_past[e])
            o_d, nrows, win = nsa_mixer(nq.reshape(b, t, NSA_HEADS, NSA_HD), nkv.reshape(b, t, 4, NSA_HD), nwin.reshape(b, t, 2, NSA_HD), ngate.reshape(b, t, NSA_HEADS, 3), pos, nsa_past[e], win_prior[e], p['nsa_pe'][e], p['nsa_wc1'][e], p['nsa_wc2'][e])
            sb_rows.append(srows)
            nsa_rows.append(nrows)
            wins.append(win)
            x = x + jnp.concatenate([o_c, o_d], axis=-1) @ p['w_out_odd'][e]
        x = x + mem_attend(rmsnorm(x, p['g_xattn'][l]), mem_kv[l], p['w_xq'][l], p['w_xo'][l])
        x = x + moe(rmsnorm(x, p['g_ffn'][l]), p['w_rg'][l], p['b_rg'][l], p['w_re'][l], p['b_re'][l], p['w_e1'][l], p['w_e3'][l], p['w_e2'][l])
    y = rmsnorm(x, p['g_final'])
    return y, mla_rows, ml_c, ml_n, ml_m, sb_rows, nsa_rows, wins


def setup_inputs(seed: int = 0) -> dict:
    key = jax.random.key(seed)
    keys = iter(jax.random.split(key, 64))
    def nrm(shape, scale=1.0):
        return scale * jax.random.normal(next(keys), shape, jnp.float32)
    def gain(shape):
        return 1.0 + 0.02 * jax.random.normal(next(keys), shape, jnp.float32)
    n_pages = PAST_LEN // PAGE_SIZE
    n_pool = (5 * DEC_BATCH * n_pages) // 4
    page_table = jax.random.permutation(next(keys), n_pool)[: DEC_BATCH * n_pages].reshape(DEC_BATCH, n_pages).astype(jnp.int32)
    wb = min(WINDOW, PAST_LEN)
    d = D_MODEL
    return {
        'x_prompt': nrm((BATCH, SEQ, d)),
        'x_sample': nrm((DEC_BATCH, DEC_SEQ, d)),
        'mem_prompt': nrm((BATCH, N_MEM, d)),
        'cache_mla': nrm((N_EVEN, n_pool, PAGE_SIZE, KV_RANK + ROPE_DIM)),
        'state_mlstm_c': nrm((N_EVEN, DEC_BATCH, ML_HEADS, ML_V, ML_QK), 0.1),
        'state_mlstm_n': nrm((N_EVEN, DEC_BATCH, ML_HEADS, ML_QK), 0.1),
        'state_mlstm_m': nrm((N_EVEN, DEC_BATCH, ML_HEADS), 0.5),
        'cache_sb_kv': nrm((N_ODD, n_pool, PAGE_SIZE, 2, SB_KV_HEADS, SB_HD)),
        'cache_nsa_kv': nrm((N_ODD, n_pool, PAGE_SIZE, 4, NSA_HD)),
        'state_nsa_win': nrm((N_ODD, DEC_BATCH, wb, 2, NSA_HD)),
        'cache_mem_kv': nrm((DEPTH, DEC_BATCH, N_MEM, 2, X_HEADS, X_HD)),
        'page_table': page_table,
        'g_mix': gain((DEPTH, d)),
        'w_in_even': nrm((N_EVEN, d, D_IN_EVEN), d ** -0.5),
        'b_ml_i': nrm((N_EVEN, ML_HEADS), 0.1) - 1.0,
        'b_ml_f': nrm((N_EVEN, ML_HEADS), 0.5) + 3.0,
        'g_cq': gain((N_EVEN, Q_RANK)),
        'w_uq': nrm((N_EVEN, Q_RANK, MLA_HEADS, NOPE_DIM + ROPE_DIM), Q_RANK ** -0.5),
        'g_ckv': gain((N_EVEN, KV_RANK)),
        'w_uk': nrm((N_EVEN, KV_RANK, MLA_HEADS, NOPE_DIM), KV_RANK ** -0.5),
        'w_uv': nrm((N_EVEN, KV_RANK, MLA_HEADS, MLA_VDIM), KV_RANK ** -0.5),
        'g_mh': gain((N_EVEN, ML_HEADS, ML_V)),
        'w_out_even': nrm((N_EVEN, D_OUT_EVEN, d), D_OUT_EVEN ** -0.5),
        'w_in_odd': nrm((N_ODD, d, D_IN_ODD), d ** -0.5),
        'nsa_pe': nrm((N_ODD, 2, CMP_LEN, NSA_HD), 0.1),
        'nsa_wc1': nrm((N_ODD, 2, CMP_LEN * NSA_HD, CMP_HID), (CMP_LEN * NSA_HD) ** -0.5),
        'nsa_wc2': nrm((N_ODD, 2, CMP_HID, NSA_HD), CMP_HID ** -0.5),
        'w_out_odd': nrm((N_ODD, D_OUT_ODD, d), D_OUT_ODD ** -0.5),
        'g_xattn': gain((DEPTH, d)),
        'g_memnorm': gain((DEPTH, d)),
        'w_xq': nrm((DEPTH, d, X_HEADS * X_HD), d ** -0.5),
        'w_xk': nrm((DEPTH, d, X_HEADS * X_HD), d ** -0.5),
        'w_xv': nrm((DEPTH, d, X_HEADS * X_HD), d ** -0.5),
        'w_xo': nrm((DEPTH, X_HEADS * X_HD, d), (X_HEADS * X_HD) ** -0.5),
        'g_ffn': gain((DEPTH, d)),
        'w_rg': nrm((DEPTH, d, N_GROUPS), d ** -0.5),
        'b_rg': nrm((DEPTH, N_GROUPS), 0.01),
        'w_re': nrm((DEPTH, d, N_GROUPS * N_EXP), d ** -0.5),
        'b_re': nrm((DEPTH, N_GROUPS * N_EXP), 0.01),
        'w_e1': nrm((DEPTH, N_GROUPS, N_EXP, d, D_EXP), d ** -0.5),
        'w_e3': nrm((DEPTH, N_GROUPS, N_EXP, d, D_EXP), d ** -0.5),
        'w_e2': nrm((DEPTH, N_GROUPS, N_EXP, D_EXP, d), D_EXP ** -0.5),
        'g_final': gain((d,)),
    }


def reference(x_prompt, x_sample, mem_prompt, cache_mla, state_mlstm_c, state_mlstm_n, state_mlstm_m, cache_sb_kv, cache_nsa_kv, state_nsa_win, cache_mem_kv, page_table, g_mix, w_in_even, b_ml_i, b_ml_f, g_cq, w_uq, g_ckv, w_uk, w_uv, g_mh, w_out_even, w_in_odd, nsa_pe, nsa_wc1, nsa_wc2, w_out_odd, g_xattn, g_memnorm, w_xq, w_xk, w_xv, w_xo, g_ffn, w_rg, b_rg, w_re, b_re, w_e1, w_e3, w_e2, g_final):
    p = dict(g_mix=g_mix, w_in_even=w_in_even, b_ml_i=b_ml_i, b_ml_f=b_ml_f, g_cq=g_cq, w_uq=w_uq, g_ckv=g_ckv, w_uk=w_uk, w_uv=w_uv, g_mh=g_mh, w_out_even=w_out_even, w_in_odd=w_in_odd, nsa_pe=nsa_pe, nsa_wc1=nsa_wc1, nsa_wc2=nsa_wc2, w_out_odd=w_out_odd, g_xattn=g_xattn, w_xq=w_xq, w_xo=w_xo, g_ffn=g_ffn, w_rg=w_rg, b_rg=b_rg, w_re=w_re, b_re=b_re, w_e1=w_e1, w_e3=w_e3, w_e2=w_e2, g_final=g_final)
    dt = x_prompt.dtype
    bp = x_prompt.shape[0]
    mem_kv_list_p = [mem_kv_rows(mem_prompt, g_memnorm[l], w_xk[l], w_xv[l]) for l in range(DEPTH)]
    y_prompt, mla_p, c_p, n_p, m_p, sb_p, nsa_p, win_pl = trunk(
        x_prompt, 0,
        [jnp.zeros((bp, 0, KV_RANK + ROPE_DIM), dt) for _ in range(N_EVEN)],
        [(jnp.zeros((bp, ML_HEADS, ML_V, ML_QK), jnp.float32), jnp.zeros((bp, ML_HEADS, ML_QK), jnp.float32), jnp.zeros((bp, ML_HEADS), jnp.float32)) for _ in range(N_EVEN)],
        [jnp.zeros((bp, 0, 2, SB_KV_HEADS, SB_HD), dt) for _ in range(N_ODD)],
        [jnp.zeros((bp, 0, 4, NSA_HD), dt) for _ in range(N_ODD)],
        [jnp.zeros((bp, 0, 2, NSA_HD), dt) for _ in range(N_ODD)],
        mem_kv_list_p, p)
    past_len = page_table.shape[1] * cache_mla.shape[2]
    y_sample, mla_s, c_s, n_s, m_s, sb_s, nsa_s, win_sl = trunk(
        x_sample, past_len,
        [gather_pages(cache_mla[e], page_table) for e in range(N_EVEN)],
        [(state_mlstm_c[e], state_mlstm_n[e], state_mlstm_m[e]) for e in range(N_EVEN)],
        [gather_pages(cache_sb_kv[e], page_table) for e in range(N_ODD)],
        [gather_pages(cache_nsa_kv[e], page_table) for e in range(N_ODD)],
        [state_nsa_win[e] for e in range(N_ODD)],
        [cache_mem_kv[l] for l in range(DEPTH)], p)
    mla_rows_p = jnp.stack(mla_p)
    mla_rows_s = jnp.stack(mla_s)
    mlstm_c_p = jnp.stack(c_p)
    mlstm_c_s = jnp.stack(c_s)
    mlstm_n_p = jnp.stack(n_p)
    mlstm_n_s = jnp.stack(n_s)
    mlstm_m_p = jnp.stack(m_p)
    mlstm_m_s = jnp.stack(m_s)
    sb_rows_p = jnp.stack(sb_p)
    sb_rows_s = jnp.stack(sb_s)
    nsa_rows_p = jnp.stack(nsa_p)
    nsa_rows_s = jnp.stack(nsa_s)
    win_p = jnp.stack(win_pl)
    win_s = jnp.stack(win_sl)
    mem_kv_p = jnp.stack(mem_kv_list_p)
    return (y_prompt, y_sample, mla_rows_p, mla_rows_s, mlstm_c_p, mlstm_c_s, mlstm_n_p, mlstm_n_s, mlstm_m_p, mlstm_m_s, sb_rows_p, sb_rows_s, nsa_rows_p, nsa_rows_s, win_p, win_s, mem_kv_p)
```

```python
import functools

import jax
import jax.numpy as jnp
import numpy as np
from jax import lax
from jax.experimental import pallas as pl
from jax.experimental.pallas import tpu as pltpu

F32 = jnp.float32
BF16 = jnp.bfloat16

D_MODEL = 2048
QBLOCK = 128
NEG = -1e30
EPS = 1e-6
MLA_HEADS = 8
Q_RANK = 512
KV_RANK = 256
NOPE_DIM = 128
ROPE_DIM = 64
MLA_VDIM = 128
ROPE_THETA = 10000.0
ML_HEADS = 4
ML_QK = 128
ML_V = 256
ML_CHUNK = 64
SB_HEADS = 8
SB_KV_HEADS = 4
SB_HD = 64
NSA_HEADS = 16
NSA_HD = 64
CMP_STRIDE = 16
CMP_LEN = 2 * CMP_STRIDE
CMP_HID = 128
SEL_BLOCK = 64
N_SEL = 16
WINDOW = 512
FORCE_BONUS = 1000.0
X_HEADS = 4
X_HD = 128
N_GROUPS = 4
N_EXP = 8
TOP_K = 2
D_EXP = 512

EVEN_COLS = (Q_RANK, KV_RANK, ROPE_DIM, ML_HEADS * ML_QK, ML_HEADS * ML_QK, ML_HEADS * ML_V, ML_HEADS, ML_HEADS, ML_HEADS * ML_V)
ODD_COLS = (SB_HEADS * SB_HD, SB_KV_HEADS * SB_HD, SB_KV_HEADS * SB_HD, NSA_HEADS * NSA_HD, 4 * NSA_HD, 2 * NSA_HD, 3 * NSA_HEADS)

LANE = 128
VMEM_LIMIT = 56 * 1024 * 1024
ROW_TILE = 512
EXPERT_TILE = 256


def _pick_tile(n, candidates):
    for c in candidates:
        if n % c == 0:
            return c
    raise ValueError(f"no tile in {candidates} divides {n}")


def _rms_matmul_kernel(*refs, do_norm, has_res):
    x_ref, g_ref, w_ref = refs[:3]
    if has_res:
        r_ref, o_ref, xn_ref = refs[3:]
    else:
        o_ref, xn_ref = refs[3:]

    @pl.when(pl.program_id(1) == 0)
    def _():
        x = x_ref[...].astype(F32)
        if do_norm:
            ms = jnp.mean(x * x, axis=-1, keepdims=True)
            x = (x * lax.rsqrt(ms + EPS)) * g_ref[...]
        xn_ref[...] = x.astype(BF16)

    acc = jnp.dot(xn_ref[...], w_ref[...], preferred_element_type=F32)
    if has_res:
        acc = acc + r_ref[...]
    o_ref[...] = acc


def rms_matmul(x, w, g=None, res=None):
    n, k = x.shape
    m = w.shape[1]
    mp = -(-m // LANE) * LANE
    wb = w.astype(BF16)
    if mp != m:
        wb = jnp.pad(wb, ((0, 0), (0, mp - m)))
        if res is not None:
            res = jnp.pad(res, ((0, 0), (0, mp - m)))
    tm = _pick_tile(n, (ROW_TILE, 256, 128, 64, 32, 16, 8))
    tn = _pick_tile(mp, (512, 384, 256, 128))
    do_norm = g is not None
    gg = (g if do_norm else jnp.ones((k,), F32)).astype(F32).reshape(1, k)
    in_specs = [
        pl.BlockSpec((tm, k), lambda i, j: (i, 0)),
        pl.BlockSpec((1, k), lambda i, j: (0, 0)),
        pl.BlockSpec((k, tn), lambda i, j: (0, j)),
    ]
    args = [x, gg, wb]
    if res is not None:
        in_specs.append(pl.BlockSpec((tm, tn), lambda i, j: (i, j)))
        args.append(res)
    out = pl.pallas_call(
        functools.partial(_rms_matmul_kernel, do_norm=do_norm, has_res=res is not None),
        grid=(n // tm, mp // tn),
        in_specs=in_specs,
        out_specs=pl.BlockSpec((tm, tn), lambda i, j: (i, j)),
        out_shape=jax.ShapeDtypeStruct((n, mp), F32),
        scratch_shapes=[pltpu.VMEM((tm, k), BF16)],
        compiler_params=pltpu.CompilerParams(
            dimension_semantics=("parallel", "arbitrary"), vmem_limit_bytes=VMEM_LIMIT),
        name="rms_matmul",
    )(*args)
    return out[:, :m] if mp != m else out


def _router_kernel(x_ref, g_ref, w_ref, b_ref, lg_ref, h_ref):
    x = x_ref[...]
    ms = jnp.mean(x * x, axis=-1, keepdims=True)
    h = (x * lax.rsqrt(ms + EPS)) * g_ref[...]
    h_ref[...] = h.astype(BF16)
    lg_ref[...] = jnp.dot(h, w_ref[...], preferred_element_type=F32,
                          precision=lax.Precision.HIGHEST) + b_ref[...]


def _experts_kernel(te_ref, tv_ref, xs_ref, sw_ref, w1_ref, w3_ref, w2_ref, o_ref):
    i = pl.program_id(0)

    @pl.when(tv_ref[i] != 0)
    def _():
        x = xs_ref[...]
        a = jnp.dot(x, w1_ref[0], preferred_element_type=F32)
        c = jnp.dot(x, w3_ref[0], preferred_element_type=F32)
        hid = (jax.nn.silu(a) * c) * sw_ref[...]
        o_ref[...] = jnp.dot(hid.astype(BF16), w2_ref[0], preferred_element_type=F32)

    @pl.when(tv_ref[i] == 0)
    def _():
        o_ref[...] = jnp.zeros_like(o_ref)


def moe(x, g, w_rg, b_rg, w_re, b_re, w_e1, w_e3, w_e2):
    n, d = x.shape
    ne = N_GROUPS * N_EXP
    tm = _pick_tile(n, (ROW_TILE, 256, 128))
    wr = jnp.zeros((d, LANE), F32).at[:, :N_GROUPS].set(w_rg).at[:, N_GROUPS:N_GROUPS + ne].set(w_re)
    br = jnp.zeros((1, LANE), F32).at[0, :N_GROUPS].set(b_rg).at[0, N_GROUPS:N_GROUPS + ne].set(b_re)
    logits, h = pl.pallas_call(
        _router_kernel,
        grid=(n // tm,),
        in_specs=[
            pl.BlockSpec((tm, d), lambda i: (i, 0)),
            pl.BlockSpec((1, d), lambda i: (0, 0)),
            pl.BlockSpec((d, LANE), lambda i: (0, 0)),
            pl.BlockSpec((1, LANE), lambda i: (0, 0)),
        ],
        out_specs=[pl.BlockSpec((tm, LANE), lambda i: (i, 0)), pl.BlockSpec((tm, d), lambda i: (i, 0))],
        out_shape=[jax.ShapeDtypeStruct((n, LANE), F32), jax.ShapeDtypeStruct((n, d), BF16)],
        compiler_params=pltpu.CompilerParams(dimension_semantics=("parallel",), vmem_limit_bytes=VMEM_LIMIT),
        name="moe_router",
    )(x, g.astype(F32).reshape(1, d), wr, br)

    lg = logits[:, :N_GROUPS]
    le = logits[:, N_GROUPS:N_GROUPS + ne].reshape(n, N_GROUPS, N_EXP)
    pg = jax.nn.softmax(lg, axis=-1)
    gi = jnp.argmax(lg, axis=-1)
    gw = jnp.take_along_axis(pg, gi[:, None], axis=1)[:, 0]
    le_g = jnp.take_along_axis(le, gi[:, None, None], axis=1)[:, 0]
    top_v, top_i = lax.top_k(le_g, TOP_K)
    we = jax.nn.softmax(top_v, axis=-1)
    eid = (gi[:, None] * N_EXP + top_i).astype(jnp.int32)
    wt = we * gw[:, None]

    ts = EXPERT_TILE
    n_tiles = -(-(TOP_K * n) // ts) + ne
    member = (eid[:, :, None] == jnp.arange(ne)[None, None, :]).any(axis=1).astype(jnp.int32)
    cnt = member.sum(axis=0)
    rank = jnp.cumsum(member, axis=0) - member
    tiles_e = (cnt + ts - 1) // ts
    tile_end = jnp.cumsum(tiles_e)
    pad_off = (tile_end - tiles_e) * ts
    pos = pad_off[eid] + jnp.take_along_axis(rank, eid, axis=1)
    tok = jnp.broadcast_to(jnp.arange(n, dtype=jnp.int32)[:, None], (n, TOP_K))
    slot_tok = jnp.zeros((n_tiles * ts,), jnp.int32).at[pos.reshape(-1)].set(tok.reshape(-1))
    slot_w = jnp.zeros((n_tiles * ts,), F32).at[pos.reshape(-1)].set(wt.reshape(-1))
    tile_id = jnp.arange(n_tiles, dtype=jnp.int32)
    tile_e = jnp.minimum(jnp.searchsorted(tile_end, tile_id, side="right"), ne - 1).astype(jnp.int32)
    tile_v = (tile_id < tile_end[-1]).astype(jnp.int32)

    xs = jnp.take(h, slot_tok, axis=0)
    w1 = w_e1.reshape(ne, d, D_EXP).astype(BF16)
    w3 = w_e3.reshape(ne, d, D_EXP).astype(BF16)
    w2 = w_e2.reshape(ne, D_EXP, d).astype(BF16)
    ys = pl.pallas_call(
        _experts_kernel,
        grid_spec=pltpu.PrefetchScalarGridSpec(
            num_scalar_prefetch=2,
            grid=(n_tiles,),
            in_specs=[
                pl.BlockSpec((ts, d), lambda i, te, tv: (i, 0)),
                pl.BlockSpec((ts, 1), lambda i, te, tv: (i, 0)),
                pl.BlockSpec((1, d, D_EXP), lambda i, te, tv: (te[i], 0, 0)),
                pl.BlockSpec((1, d, D_EXP), lambda i, te, tv: (te[i], 0, 0)),
                pl.BlockSpec((1, D_EXP, d), lambda i, te, tv: (te[i], 0, 0)),
            ],
            out_specs=pl.BlockSpec((ts, d), lambda i, te, tv: (i, 0)),
        ),
        out_shape=jax.ShapeDtypeStruct((n_tiles * ts, d), F32),
        compiler_params=pltpu.CompilerParams(dimension_semantics=("arbitrary",), vmem_limit_bytes=VMEM_LIMIT),
        name="moe_experts",
    )(tile_e, tile_v, xs, slot_w.reshape(-1, 1), w1, w3, w2)
    return jnp.take(ys, pos[:, 0], axis=0) + jnp.take(ys, pos[:, 1], axis=0)


def rmsnorm(x, g):
    xf = x.astype(F32)
    y = xf * lax.rsqrt(jnp.mean(xf * xf, axis=-1, keepdims=True) + EPS)
    return (y * g.astype(F32)).astype(x.dtype)


def split_cols(a, widths):
    return jnp.split(a, np.cumsum(widths)[:-1].tolist(), axis=-1)


def qblock(t):
    return min(QBLOCK, t)


def map_query_blocks(fn, arrays, block):
    t = arrays[0].shape[1]
    nb = -(-t // block)
    pad = nb * block - t

    def to_blocks(a):
        a = jnp.pad(a, [(0, 0), (0, pad)] + [(0, 0)] * (a.ndim - 2))
        a = a.reshape((a.shape[0], nb, block) + a.shape[2:])
        return jnp.moveaxis(a, 1, 0)

    xs = (jnp.arange(nb),) + tuple(to_blocks(a) for a in arrays)
    out = lax.map(lambda args: fn(*args), xs)
    out = jnp.moveaxis(out, 0, 1)
    out = out.reshape((out.shape[0], nb * block) + out.shape[3:])
    return out[:, :t]


def rope_angles(pos, dim):
    inv = ROPE_THETA ** (-jnp.arange(0, dim, 2, dtype=F32) / dim)
    ang = pos.astype(F32)[:, None] * inv[None, :]
    return jnp.cos(ang), jnp.sin(ang)


def apply_rope(x, cos, sin):
    half = x.shape[-1] // 2
    x1 = x[..., :half].astype(F32)
    x2 = x[..., half:].astype(F32)
    return jnp.concatenate([x1 * cos - x2 * sin, x1 * sin + x2 * cos], axis=-1).astype(x.dtype)


def alibi_slopes(n):
    return 2.0 ** (-8.0 * jnp.arange(1, n + 1, dtype=F32) / n)


def gather_pages(pool, page_table):
    g = pool[page_table]
    return g.reshape((page_table.shape[0], page_table.shape[1] * pool.shape[1]) + pool.shape[2:])


def mla_mixer(c_q, c_kv_raw, k_r_raw, pos, past, g_cq, w_uq, g_ckv, w_uk, w_uv):
    b, t, _ = c_q.shape
    cos, sin = rope_angles(pos, ROPE_DIM)
    q = jnp.einsum('btr,rhd->bthd', rmsnorm(c_q, g_cq), w_uq)
    q_nope = q[..., :NOPE_DIM]
    q_rope = apply_rope(q[..., NOPE_DIM:], cos[None, :, None, :], sin[None, :, None, :])
    q_lat = jnp.einsum('bthd,rhd->bthr', q_nope, w_uk)
    c_kv = rmsnorm(c_kv_raw, g_ckv)
    k_r = apply_rope(k_r_raw, cos[None], sin[None])
    new_rows = jnp.concatenate([c_kv, k_r], axis=-1)
    keys = jnp.concatenate([past, new_rows], axis=1)
    k_pos = jnp.arange(keys.shape[1])
    lat, kr = keys[..., :KV_RANK], keys[..., KV_RANK:]
    scale = (NOPE_DIM + ROPE_DIM) ** -0.5

    def blk(bi, ql, qr, qp):
        s = (jnp.einsum('bqhr,bsr->bhqs', ql, lat) + jnp.einsum('bqhd,bsd->bhqs', qr, kr)).astype(F32) * scale
        mask = k_pos[None, :] <= qp[0][:, None]
        p = jax.nn.softmax(jnp.where(mask, s, NEG), axis=-1).astype(lat.dtype)
        return jnp.einsum('bhqs,bsr->bqhr', p, lat)

    o_lat = map_query_blocks(blk, (q_lat, q_rope, pos[None]), qblock(t))
    out = jnp.einsum('bthr,rhd->bthd', o_lat, w_uv).reshape(b, t, MLA_HEADS * MLA_VDIM)
    return out, new_rows


def mlstm_mixer(q, k, v, ig, fg, og, c0, n0, m0, g_mh):
    b, t = q.shape[:2]
    dt = q.dtype
    q = q.reshape(b, t, ML_HEADS, ML_QK)
    k = k.reshape(b, t, ML_HEADS, ML_QK) * (ML_QK ** -0.5)
    v = v.reshape(b, t, ML_HEADS, ML_V)
    li = ig.astype(F32)
    lf = jax.nn.log_sigmoid(fg.astype(F32))
    L = min(ML_CHUNK, t)
    nc = -(-t // L)
    pad = nc * L - t

    def chunks(a, val=0.0):
        a = jnp.pad(a, [(0, 0), (0, pad)] + [(0, 0)] * (a.ndim - 2), constant_values=val)
        return jnp.moveaxis(a.reshape((b, nc, L) + a.shape[2:]), 1, 0)

    tri = jnp.tril(jnp.ones((L, L), dtype=bool))

    def step(carry, xs):
        c, n, m = carry
        qc, kc, vc, lic, lfc = xs
        cb = jnp.cumsum(lfc, axis=1)
        a = cb + m[:, None, :]
        d = cb[:, :, None, :] - cb[:, None, :, :] + lic[:, None, :, :]
        d = jnp.where(tri[None, :, :, None], d, NEG)
        mt = jnp.maximum(a, d.max(axis=2))
        w = jnp.exp(d - mt[:, :, None, :])
        inter = jnp.exp(a - mt)
        sc = w * jnp.einsum('bthd,bshd->btsh', qc, kc).astype(F32)
        num = jnp.einsum('btsh,bshv->bthv', sc, vc) + inter[..., None] * jnp.einsum('bhvd,bthd->bthv', c, qc)
        den = sc.sum(axis=2) + inter * jnp.einsum('bhd,bthd->bth', n, qc)
        h = num / jnp.maximum(jnp.abs(den), jnp.exp(-mt))[..., None]
        wl, il = w[:, -1], inter[:, -1]
        c = il[..., None, None] * c + jnp.einsum('bsh,bshv,bshd->bhvd', wl, vc, kc)
        n = il[..., None] * n + jnp.einsum('bsh,bshd->bhd', wl, kc)
        return (c, n, mt[:, -1]), h

    carry0 = (c0.astype(F32), n0.astype(F32), m0.astype(F32))
    (c, n, m), h = lax.scan(step, carry0, (chunks(q), chunks(k), chunks(v), chunks(li, NEG), chunks(lf)))
    h = jnp.moveaxis(h, 0, 1).reshape(b, nc * L, ML_HEADS, ML_V)[:, :t]
    h = h - h.mean(axis=-1, keepdims=True)
    h = h * lax.rsqrt(jnp.mean(h * h, axis=-1, keepdims=True) + EPS) * g_mh.astype(F32)
    out = jax.nn.sigmoid(og.astype(F32)).reshape(b, t, ML_HEADS, ML_V) * h
    return out.reshape(b, t, ML_HEADS * ML_V).astype(dt), c, n, m


def sb_mixer(q, k_new, v_new, pos, past_kv):
    b, t = q.shape[:2]
    q = q.reshape(b, t, SB_KV_HEADS, SB_HEADS // SB_KV_HEADS, SB_HD)
    new_rows = jnp.stack([k_new.reshape(b, t, SB_KV_HEADS, SB_HD), v_new.reshape(b, t, SB_KV_HEADS, SB_HD)], axis=2)
    kv = jnp.concatenate([past_kv, new_rows], axis=1)
    k, v = kv[:, :, 0], kv[:, :, 1]
    k_pos = jnp.arange(kv.shape[1])

    def blk(bi, qh, qp):
        z = jnp.einsum('bqgrd,bsgd->bgrqs', qh, k).astype(F32) * (SB_HD ** -0.5)
        mask = k_pos[None, :] < qp[0][:, None]
        l1mb = jnp.where(mask, jax.nn.log_sigmoid(-z), 0.0)
        after = lax.cumsum(l1mb, axis=z.ndim - 1, reverse=True) - l1mb
        att = jnp.where(mask, jnp.exp(jax.nn.log_sigmoid(z) + after), 0.0).astype(v.dtype)
        return jnp.einsum('bgrqs,bsgd->bqgrd', att, v)

    o = map_query_blocks(blk, (q, pos[None]), qblock(t))
    return o.reshape(b, t, SB_HEADS * SB_HD), new_rows


def nsa_mixer(q, kv_new, win_new, gate, pos, past_kv, win_prior, pe, wc1, wc2):
    b, t = q.shape[:2]
    past = past_kv.shape[1]
    s_len = past + t
    scale = NSA_HD ** -0.5
    slopes = alibi_slopes(NSA_HEADS)
    kv_all = jnp.concatenate([past_kv, kv_new], axis=1)
    n_chunk = max(-(-s_len // CMP_STRIDE), 2)
    ck = jnp.pad(kv_all[:, :, :2], ((0, 0), (0, n_chunk * CMP_STRIDE - s_len), (0, 0), (0, 0)))
    ck = ck.reshape(b, n_chunk, CMP_STRIDE, 2, NSA_HD)
    w1 = wc1.reshape(2, 2, CMP_STRIDE, NSA_HD, CMP_HID)
    proj = jnp.einsum('bcrkd,kzrdh->bczkh', ck, w1)
    pe_term = jnp.einsum('kpd,kpdh->kh', pe, wc1.reshape(2, CMP_LEN, NSA_HD, CMP_HID))
    hid = jax.nn.gelu(proj[:, :-1, 0] + proj[:, 1:, 1] + pe_term)
    cmp = jnp.einsum('bnkh,khd->bnkd', hid, wc2)
    k_cmp, v_cmp = cmp[:, :, 0], cmp[:, :, 1]
    cmp_start = CMP_STRIDE * jnp.arange(n_chunk - 1)
    cmp_end = cmp_start + CMP_LEN - 1
    n_sel = -(-s_len // SEL_BLOCK)
    sel = jnp.pad(kv_all[:, :, 2:], ((0, 0), (0, n_sel * SEL_BLOCK - s_len), (0, 0), (0, 0)))
    sel = sel.reshape(b, n_sel, SEL_BLOCK, 2, NSA_HD)
    sel_start = SEL_BLOCK * jnp.arange(n_sel)
    overlap = ((cmp_start[:, None] < sel_start[None, :] + SEL_BLOCK) & (cmp_start[:, None] + CMP_LEN > sel_start[None, :])).astype(F32)
    k_top = min(N_SEL, n_sel)
    bidx = jnp.arange(b)[:, None, None]
    pw = win_prior.shape[1]
    qb = qblock(t)
    nb = -(-t // qb)
    band = jnp.concatenate([jnp.zeros((b, WINDOW - pw, 2, NSA_HD), win_new.dtype), win_prior, win_new, jnp.zeros((b, nb * qb - t, 2, NSA_HD), win_new.dtype)], axis=1)
    band_pos = past - WINDOW + jnp.arange(WINDOW + nb * qb)

    def blk(bi, qh, gh, qp):
        tq = qp[0]
        valid_c = cmp_end[None, :] <= tq[:, None]
        dist_c = (tq[:, None] - cmp_end[None, :]).astype(F32)
        s_c = jnp.einsum('bqhd,bnd->bhqn', qh, k_cmp).astype(F32) * scale - slopes[:, None, None] * dist_c
        p_c = jnp.where(valid_c, jax.nn.softmax(jnp.where(valid_c, s_c, NEG), axis=-1), 0.0)
        o_c = jnp.einsum('bhqn,bnd->bqhd', p_c.astype(v_cmp.dtype), v_cmp)
        imp = jnp.einsum('bhqn,nj->bqj', p_c, overlap)
        cur = tq // SEL_BLOCK
        j = jnp.arange(n_sel)
        forced = (j[None, :] == 0) | (j[None, :] == cur[:, None]) | (j[None, :] == cur[:, None] - 1)
        cand = sel_start[None, :] <= tq[:, None]
        score = jnp.where(cand[None], imp + jnp.where(forced, FORCE_BONUS, 0.0)[None], NEG)
        top_s, idx = lax.top_k(score, k_top)
        kv_sel = sel[bidx, idx]
        kpos = idx[..., None] * SEL_BLOCK + jnp.arange(SEL_BLOCK)
        ok = (top_s > 0.5 * NEG)[..., None] & (kpos <= tq[None, :, None, None])
        dist_s = (tq[None, :, None, None] - kpos).astype(F32)
        s_s = jnp.einsum('bqhd,bqkld->bqhkl', qh, kv_sel[..., 0, :]).astype(F32) * scale - slopes[None, None, :, None, None] * dist_s[:, :, None]
        s_s = jnp.where(ok[:, :, None], s_s, NEG)
        p_s = jax.nn.softmax(s_s.reshape(s_s.shape[:3] + (-1,)), axis=-1).reshape(s_s.shape)
        o_s = jnp.einsum('bqhkl,bqkld->bqhd', p_s.astype(kv_sel.dtype), kv_sel[..., 1, :])
        bw = lax.dynamic_slice_in_dim(band, bi * qb, qb + WINDOW, axis=1)
        bp = lax.dynamic_slice_in_dim(band_pos, bi * qb, qb + WINDOW, axis=0)
        dist_w = tq[:, None] - bp[None, :]
        ok_w = (bp[None, :] >= 0) & (dist_w >= 0) & (dist_w < WINDOW)
        s_w = jnp.einsum('bqhd,bsd->bhqs', qh, bw[:, :, 0]).astype(F32) * scale - slopes[:, None, None] * dist_w.astype(F32)
        p_w = jax.nn.softmax(jnp.where(ok_w, s_w, NEG), axis=-1)
        o_w = jnp.einsum('bhqs,bsd->bqhd', p_w.astype(bw.dtype), bw[:, :, 1])
        g = jax.nn.sigmoid(gh.astype(F32))
        return (g[..., 0:1] * o_c + g[..., 1:2] * o_s + g[..., 2:3] * o_w).astype(qh.dtype)

    o = map_query_blocks(blk, (q, gate, pos[None]), qb)
    new_win = jnp.concatenate([win_prior, win_new], axis=1)[:, -min(WINDOW, pw + t):]
    return o.reshape(b, t, NSA_HEADS * NSA_HD), kv_new, new_win


def mem_kv_rows(mem, g, wk, wv):
    b, nm, d = mem.shape
    kv = rms_matmul(mem.reshape(b * nm, d), jnp.concatenate([wk, wv], axis=1), g=g)
    hd = X_HEADS * X_HD
    k = kv[:, :hd].reshape(b, nm, X_HEADS, X_HD)
    v = kv[:, hd:].reshape(b, nm, X_HEADS, X_HD)
    return jnp.stack([k, v], axis=2)


def mem_attend(x, g, mkv, wq, wo):
    b, t, d = x.shape
    q = rms_matmul(x.reshape(b * t, d), wq, g=g).reshape(b, t, X_HEADS, X_HD)
    s = jnp.einsum('bthd,bmhd->bhtm', q, mkv[:, :, 0]).astype(F32) * (X_HD ** -0.5)
    p = jax.nn.softmax(s, axis=-1).astype(mkv.dtype)
    o = jnp.einsum('bhtm,bmhd->bthd', p, mkv[:, :, 1]).reshape(b * t, X_HEADS * X_HD)
    return rms_matmul(o, wo, res=x.reshape(b * t, d)).reshape(b, t, d)


def trunk(x, start, mla_past, ml_state, sb_past, nsa_past, win_prior, mem_kv, p):
    b, t, d = x.shape
    pos = start + jnp.arange(t, dtype=jnp.int32)
    depth = p['g_mix'].shape[0]
    mla_rows, ml_c, ml_n, ml_m, sb_rows, nsa_rows, wins = [], [], [], [], [], [], []
    for l in range(depth):
        e = l // 2
        x2 = x.reshape(b * t, d)
        if l % 2 == 0:
            proj = rms_matmul(x2, p['w_in_even'][e], g=p['g_mix'][l]).reshape(b, t, -1)
            c_q, c_kv, k_r, mq, mk, mv, mi, mf, mo = split_cols(proj, EVEN_COLS)
            o_a, rows = mla_mixer(c_q, c_kv, k_r, pos, mla_past[e], p['g_cq'][e], p['w_uq'][e], p['g_ckv'][e], p['w_uk'][e], p['w_uv'][e])
            c0, n0, m0 = ml_state[e]
            o_b, c, n, m = mlstm_mixer(mq, mk, mv, mi + p['b_ml_i'][e], mf + p['b_ml_f'][e], mo, c0, n0, m0, p['g_mh'][e])
            mla_rows.append(rows)
            ml_c.append(c)
            ml_n.append(n)
            ml_m.append(m)
            mix = jnp.concatenate([o_a, o_b], axis=-1).reshape(b * t, -1)
            x = rms_matmul(mix, p['w_out_even'][e], res=x2).reshape(b, t, d)
        else:
            proj = rms_matmul(x2, p['w_in_odd'][e], g=p['g_mix'][l]).reshape(b, t, -1)
            sq, sk, sv, nq, nkv, nwin, ngate = split_cols(proj, ODD_COLS)
            o_c, srows = sb_mixer(sq, sk, sv, pos, sb_past[e])
            o_d, nrows, win = nsa_mixer(nq.reshape(b, t, NSA_HEADS, NSA_HD), nkv.reshape(b, t, 4, NSA_HD), nwin.reshape(b, t, 2, NSA_HD), ngate.reshape(b, t, NSA_HEADS, 3), pos, nsa_past[e], win_prior[e], p['nsa_pe'][e], p['nsa_wc1'][e], p['nsa_wc2'][e])
            sb_rows.append(srows)
            nsa_rows.append(nrows)
            wins.append(win)
            mix = jnp.concatenate([o_c, o_d], axis=-1).reshape(b * t, -1)
            x = rms_matmul(mix, p['w_out_odd'][e], res=x2).reshape(b, t, d)
        x = mem_attend(x, p['g_xattn'][l], mem_kv[l], p['w_xq'][l], p['w_xo'][l])
        x2 = x.reshape(b * t, d)
        x = (x2 + moe(x2, p['g_ffn'][l], p['w_rg'][l], p['b_rg'][l], p['w_re'][l], p['b_re'][l], p['w_e1'][l], p['w_e3'][l], p['w_e2'][l])).reshape(b, t, d)
    y = rmsnorm(x, p['g_final'])
    return y, mla_rows, ml_c, ml_n, ml_m, sb_rows, nsa_rows, wins


def kernel(x_prompt, x_sample, mem_prompt, cache_mla, state_mlstm_c, state_mlstm_n, state_mlstm_m, cache_sb_kv, cache_nsa_kv, state_nsa_win, cache_mem_kv, page_table, g_mix, w_in_even, b_ml_i, b_ml_f, g_cq, w_uq, g_ckv, w_uk, w_uv, g_mh, w_out_even, w_in_odd, nsa_pe, nsa_wc1, nsa_wc2, w_out_odd, g_xattn, g_memnorm, w_xq, w_xk, w_xv, w_xo, g_ffn, w_rg, b_rg, w_re, b_re, w_e1, w_e3, w_e2, g_final):
    p = dict(g_mix=g_mix, w_in_even=w_in_even, b_ml_i=b_ml_i, b_ml_f=b_ml_f, g_cq=g_cq, w_uq=w_uq, g_ckv=g_ckv, w_uk=w_uk, w_uv=w_uv, g_mh=g_mh, w_out_even=w_out_even, w_in_odd=w_in_odd, nsa_pe=nsa_pe, nsa_wc1=nsa_wc1, nsa_wc2=nsa_wc2, w_out_odd=w_out_odd, g_xattn=g_xattn, w_xq=w_xq, w_xo=w_xo, g_ffn=g_ffn, w_rg=w_rg, b_rg=b_rg, w_re=w_re, b_re=b_re, w_e1=w_e1, w_e3=w_e3, w_e2=w_e2, g_final=g_final)
    dt = x_prompt.dtype
    bp = x_prompt.shape[0]
    depth = g_mix.shape[0]
    n_even = (depth + 1) // 2
    n_odd = depth // 2
    mem_kv_list_p = [mem_kv_rows(mem_prompt, g_memnorm[l], w_xk[l], w_xv[l]) for l in range(depth)]
    y_prompt, mla_p, c_p, n_p, m_p, sb_p, nsa_p, win_pl = trunk(
        x_prompt, 0,
        [jnp.zeros((bp, 0, KV_RANK + ROPE_DIM), dt) for _ in range(n_even)],
        [(jnp.zeros((bp, ML_HEADS, ML_V, ML_QK), F32), jnp.zeros((bp, ML_HEADS, ML_QK), F32), jnp.zeros((bp, ML_HEADS), F32)) for _ in range(n_even)],
        [jnp.zeros((bp, 0, 2, SB_KV_HEADS, SB_HD), dt) for _ in range(n_odd)],
        [jnp.zeros((bp, 0, 4, NSA_HD), dt) for _ in range(n_odd)],
        [jnp.zeros((bp, 0, 2, NSA_HD), dt) for _ in range(n_odd)],
        mem_kv_list_p, p)
    past_len = page_table.shape[1] * cache_mla.shape[2]
    y_sample, mla_s, c_s, n_s, m_s, sb_s, nsa_s, win_sl = trunk(
        x_sample, past_len,
        [gather_pages(cache_mla[e], page_table) for e in range(n_even)],
        [(state_mlstm_c[e], state_mlstm_n[e], state_mlstm_m[e]) for e in range(n_even)],
        [gather_pages(cache_sb_kv[e], page_table) for e in range(n_odd)],
        [gather_pages(cache_nsa_kv[e], page_table) for e in range(n_odd)],
        [state_nsa_win[e] for e in range(n_odd)],
        [cache_mem_kv[l] for l in range(depth)], p)
    return (y_prompt, y_sample, jnp.stack(mla_p), jnp.stack(mla_s), jnp.stack(c_p), jnp.stack(c_s),
            jnp.stack(n_p), jnp.stack(n_s), jnp.stack(m_p), jnp.stack(m_s), jnp.stack(sb_p), jnp.stack(sb_s),
            jnp.stack(nsa_p), jnp.stack(nsa_s), jnp.stack(win_pl), jnp.stack(win_sl), jnp.stack(mem_kv_list_p))
```

```python
import functools

import jax
import jax.numpy as jnp
import numpy as np
from jax import lax
from jax.experimental import pallas as pl
from jax.experimental.pallas import tpu as pltpu

F32 = jnp.float32
BF16 = jnp.bfloat16

D_MODEL = 2048
QBLOCK = 128
NEG = -1e30
EPS = 1e-6
MLA_HEADS = 8
Q_RANK = 512
KV_RANK = 256
NOPE_DIM = 128
ROPE_DIM = 64
MLA_VDIM = 128
ROPE_THETA = 10000.0
ML_HEADS = 4
ML_QK = 128
ML_V = 256
ML_CHUNK = 64
SB_HEADS = 8
SB_KV_HEADS = 4
SB_HD = 64
NSA_HEADS = 16
NSA_HD = 64
CMP_STRIDE = 16
CMP_LEN = 2 * CMP_STRIDE
CMP_HID = 128
SEL_BLOCK = 64
N_SEL = 16
WINDOW = 512
FORCE_BONUS = 1000.0
X_HEADS = 4
X_HD = 128
N_GROUPS = 4
N_EXP = 8
TOP_K = 2
D_EXP = 512

EVEN_COLS = (Q_RANK, KV_RANK, ROPE_DIM, ML_HEADS * ML_QK, ML_HEADS * ML_QK, ML_HEADS * ML_V, ML_HEADS, ML_HEADS, ML_HEADS * ML_V)
ODD_COLS = (SB_HEADS * SB_HD, SB_KV_HEADS * SB_HD, SB_KV_HEADS * SB_HD, NSA_HEADS * NSA_HD, 4 * NSA_HD, 2 * NSA_HD, 3 * NSA_HEADS)

LANE = 128
VMEM_LIMIT = 56 * 1024 * 1024
ROW_TILE = 512
EXPERT_TILE = 256


def _pick_tile(n, candidates):
    for c in candidates:
        if n % c == 0:
            return c
    raise ValueError(f"no tile in {candidates} divides {n}")


def _rms_matmul_kernel(*refs, do_norm, has_bias, has_res, act):
    x_ref, g_ref, w_ref = refs[:3]
    rest = list(refs[3:])
    b_ref = rest.pop(0) if has_bias else None
    r_ref = rest.pop(0) if has_res else None
    o_ref, xn_ref = rest

    @pl.when(pl.program_id(1) == 0)
    def _():
        x = x_ref[...].astype(F32)
        if do_norm:
            ms = jnp.mean(x * x, axis=-1, keepdims=True)
            x = (x * lax.rsqrt(ms + EPS)) * g_ref[...]
        xn_ref[...] = x.astype(BF16)

    acc = jnp.dot(xn_ref[...], w_ref[...], preferred_element_type=F32)
    if has_bias:
        acc = acc + b_ref[...]
    if act == "gelu":
        acc = jax.nn.gelu(acc)
    if has_res:
        acc = acc + r_ref[...]
    o_ref[...] = acc


def rms_matmul(x, w, g=None, res=None, bias=None, act=None, keep_pad=False):
    n, k = x.shape
    m = w.shape[1]
    mp = -(-m // LANE) * LANE
    wb = w.astype(BF16)
    if mp != m:
        wb = jnp.pad(wb, ((0, 0), (0, mp - m)))
        if res is not None:
            res = jnp.pad(res, ((0, 0), (0, mp - m)))
        if bias is not None:
            bias = jnp.pad(bias, ((0, mp - m),))
    tm = _pick_tile(n, (ROW_TILE, 256, 128, 64, 32, 16, 8))
    tn = _pick_tile(mp, (512, 384, 256, 128))
    do_norm = g is not None
    gg = (g if do_norm else jnp.ones((k,), F32)).astype(F32).reshape(1, k)
    in_specs = [
        pl.BlockSpec((tm, k), lambda i, j: (i, 0)),
        pl.BlockSpec((1, k), lambda i, j: (0, 0)),
        pl.BlockSpec((k, tn), lambda i, j: (0, j)),
    ]
    args = [x, gg, wb]
    if bias is not None:
        in_specs.append(pl.BlockSpec((1, tn), lambda i, j: (0, j)))
        args.append(bias.astype(F32).reshape(1, mp))
    if res is not None:
        in_specs.append(pl.BlockSpec((tm, tn), lambda i, j: (i, j)))
        args.append(res)
    out = pl.pallas_call(
        functools.partial(_rms_matmul_kernel, do_norm=do_norm, has_bias=bias is not None,
                          has_res=res is not None, act=act),
        grid=(n // tm, mp // tn),
        in_specs=in_specs,
        out_specs=pl.BlockSpec((tm, tn), lambda i, j: (i, j)),
        out_shape=jax.ShapeDtypeStruct((n, mp), F32),
        scratch_shapes=[pltpu.VMEM((tm, k), BF16)],
        compiler_params=pltpu.CompilerParams(
            dimension_semantics=("parallel", "arbitrary"), vmem_limit_bytes=VMEM_LIMIT),
        name="rms_matmul",
    )(*args)
    return out if (keep_pad or mp == m) else out[:, :m]


ATT_TILE = 256


def _sb_prompt_kernel(q_ref, k_ref, v_ref, o_ref, qs_ref, acc_ref, run_ref, *, tq, groups, rep, hd, scale):
    qi = pl.program_id(1)
    kk = pl.program_id(2)
    nk = pl.num_programs(2)

    @pl.when(kk == 0)
    def _():
        for gi in range(groups):
            for ri in range(rep):
                c0 = (gi * rep + ri) * hd
                qs_ref[gi, ri * tq:(ri + 1) * tq, :] = (q_ref[:, c0:c0 + hd] * scale).astype(BF16)
        acc_ref[...] = jnp.zeros_like(acc_ref)
        run_ref[...] = jnp.zeros_like(run_ref)

    @pl.when(kk <= qi)
    def _():
        rows = rep * tq
        row_t = lax.broadcasted_iota(jnp.int32, (rows, tq), 0) % tq
        col = lax.broadcasted_iota(jnp.int32, (rows, tq), 1)
        mask = (col < row_t) | (kk > 0)
        later = (lax.broadcasted_iota(jnp.int32, (tq, tq), 0) >
                 lax.broadcasted_iota(jnp.int32, (tq, tq), 1)).astype(BF16)
        for gi in range(groups):
            kt = k_ref[:, gi * hd:(gi + 1) * hd].astype(BF16)
            vt = v_ref[:, gi * hd:(gi + 1) * hd].astype(BF16)
            z = lax.dot_general(qs_ref[gi], kt, (((1,), (1,)), ((), ())), preferred_element_type=F32)
            ls = jnp.minimum(z, 0.0) - jnp.log(1.0 + jnp.exp(-jnp.abs(z)))
            l1mb = jnp.where(mask, ls - z, 0.0)
            hi = l1mb.astype(BF16)
            lo = (l1mb - hi.astype(F32)).astype(BF16)
            aft = jnp.dot(hi, later, preferred_element_type=F32) + jnp.dot(lo, later, preferred_element_type=F32)
            run = run_ref[gi][:, :1]
            att = jnp.where(mask, jnp.exp(ls + aft + run), 0.0)
            acc_ref[gi] += jnp.dot(att.astype(BF16), vt, preferred_element_type=F32)
            run_ref[gi] = jnp.broadcast_to(run + aft[:, :1] + l1mb[:, :1], (rows, LANE))

    @pl.when(kk == nk - 1)
    def _():
        for gi in range(groups):
            for ri in range(rep):
                c0 = (gi * rep + ri) * hd
                o_ref[:, c0:c0 + hd] = acc_ref[gi, ri * tq:(ri + 1) * tq, :]


def sb_attn_prompt(proj, b, t, q_col, k_col, v_col):
    tq = ATT_TILE
    nq = t // tq
    qw = SB_HEADS * SB_HD
    kw = SB_KV_HEADS * SB_HD
    rep = SB_HEADS // SB_KV_HEADS
    assert t % tq == 0 and q_col % qw == 0 and k_col % kw == 0 and v_col % kw == 0
    kern = functools.partial(_sb_prompt_kernel, tq=tq, groups=SB_KV_HEADS, rep=rep, hd=SB_HD, scale=SB_HD ** -0.5)
    return pl.pallas_call(
        kern,
        grid=(b, nq, nq),
        in_specs=[
            pl.BlockSpec((tq, qw), lambda bi, qi, kk: (bi * nq + qi, q_col // qw)),
            pl.BlockSpec((tq, kw), lambda bi, qi, kk: (bi * nq + jnp.maximum(qi - kk, 0), k_col // kw)),
            pl.BlockSpec((tq, kw), lambda bi, qi, kk: (bi * nq + jnp.maximum(qi - kk, 0), v_col // kw)),
        ],
        out_specs=pl.BlockSpec((tq, qw), lambda bi, qi, kk: (bi * nq + qi, 0)),
        out_shape=jax.ShapeDtypeStruct((b * t, qw), F32),
        scratch_shapes=[
            pltpu.VMEM((SB_KV_HEADS, rep * tq, SB_HD), BF16),
            pltpu.VMEM((SB_KV_HEADS, rep * tq, SB_HD), F32),
            pltpu.VMEM((SB_KV_HEADS, rep * tq, LANE), F32),
        ],
        compiler_params=pltpu.CompilerParams(
            dimension_semantics=("parallel", "parallel", "arbitrary"), vmem_limit_bytes=VMEM_LIMIT),
        name="sb_attn_prompt",
    )(proj, proj, proj)


def _alibi_slopes_np(n):
    return [float(v) for v in (np.float32(2.0) ** (np.float32(-8.0) * np.arange(1, n + 1, dtype=np.float32) / np.float32(n)))]


def _mqa_flash_kernel(*refs, mode, heads, dk, dv, k_off, v_off, tq, scale, slopes, window, sel_block, nks):
    if mode == "select":
        q_ref, kv_ref, sel_ref, o_ref, qs_ref, m_ref, l_ref, acc_ref = refs
    else:
        q_ref, kv_ref, o_ref, qs_ref, m_ref, l_ref, acc_ref = refs
    qi = pl.program_id(1)
    kk = pl.program_id(2)
    kj = qi - (nks - 1) + kk if mode == "window" else kk

    @pl.when(kk == 0)
    def _():
        for h in range(heads):
            qs_ref[h] = q_ref[:, h * dk:(h + 1) * dk].astype(BF16)
        m_ref[...] = jnp.full_like(m_ref, NEG)
        l_ref[...] = jnp.zeros_like(l_ref)
        acc_ref[...] = jnp.zeros_like(acc_ref)

    active = (kj >= 0) if mode == "window" else (kj <= qi)

    @pl.when(active)
    def _():
        q_pos = qi * tq + lax.broadcasted_iota(jnp.int32, (tq, tq), 0)
        k_pos = kj * tq + lax.broadcasted_iota(jnp.int32, (tq, tq), 1)
        dist_i = q_pos - k_pos
        mask = dist_i >= 0
        if mode == "window":
            mask = mask & (dist_i < window)
        if mode == "select":
            nblk = sel_ref.shape[1]
            blk_of_key = (kj * tq + lax.broadcasted_iota(jnp.int32, (nblk, tq), 1)) // sel_block
            expand = (lax.broadcasted_iota(jnp.int32, (nblk, tq), 0) == blk_of_key).astype(BF16)
            chosen = jnp.dot(sel_ref[...].astype(BF16), expand, preferred_element_type=F32)
            mask = mask & (chosen > 0.5)
        dist = dist_i.astype(F32)
        kt = kv_ref[:, k_off:k_off + dk].astype(BF16)
        vt = kv_ref[:, v_off:v_off + dv].astype(BF16)
        for h in range(heads):
            s = lax.dot_general(qs_ref[h], kt, (((1,), (1,)), ((), ())), preferred_element_type=F32) * scale
            if slopes is not None:
                s = s - slopes[h] * dist
            s = jnp.where(mask, s, NEG)
            m_old = m_ref[h][:, :1]
            m_new = jnp.maximum(m_old, jnp.max(s, axis=-1, keepdims=True))
            alpha = jnp.exp(m_old - m_new)
            p = jnp.exp(s - m_new)
            l_ref[h] = jnp.broadcast_to(alpha * l_ref[h][:, :1] + jnp.sum(p, axis=-1, keepdims=True), (tq, LANE))
            acc_ref[h] = alpha * acc_ref[h] + jnp.dot(p.astype(BF16), vt, preferred_element_type=F32)
            m_ref[h] = jnp.broadcast_to(m_new, (tq, LANE))

    @pl.when(kk == nks - 1)
    def _():
        for h in range(heads):
            o_ref[:, h * dv:(h + 1) * dv] = acc_ref[h] / l_ref[h][:, :1]


def mqa_flash(q, q_col, kv, kv_col, kv_w, b, t, *, mode, heads, dk, dv, k_off, v_off, scale, slopes=None, sel=None):
    tq = ATT_TILE
    nq = t // tq
    qw = heads * dk
    assert t % tq == 0 and q_col % qw == 0 and kv_col % kv_w == 0
    nks = (WINDOW // tq + 1) if mode == "window" else nq
    if mode == "window":
        kv_idx = lambda bi, qi, kk: (bi * nq + jnp.maximum(qi - (nks - 1) + kk, 0), kv_col // kv_w)
    else:
        kv_idx = lambda bi, qi, kk: (bi * nq + jnp.minimum(kk, qi), kv_col // kv_w)
    in_specs = [
        pl.BlockSpec((tq, qw), lambda bi, qi, kk: (bi * nq + qi, q_col // qw)),
        pl.BlockSpec((tq, kv_w), kv_idx),
    ]
    args = [q, kv]
    if mode == "select":
        in_specs.append(pl.BlockSpec((tq, sel.shape[1]), lambda bi, qi, kk: (bi * nq + qi, 0)))
        args.append(sel)
    kern = functools.partial(_mqa_flash_kernel, mode=mode, heads=heads, dk=dk, dv=dv, k_off=k_off, v_off=v_off,
                             tq=tq, scale=scale, slopes=slopes, window=WINDOW, sel_block=SEL_BLOCK, nks=nks)
    return pl.pallas_call(
        kern,
        grid=(b, nq, nks),
        in_specs=in_specs,
        out_specs=pl.BlockSpec((tq, heads * dv), lambda bi, qi, kk: (bi * nq + qi, 0)),
        out_shape=jax.ShapeDtypeStruct((b * t, heads * dv), F32),
        scratch_shapes=[
            pltpu.VMEM((heads, tq, dk), BF16),
            pltpu.VMEM((heads, tq, LANE), F32),
            pltpu.VMEM((heads, tq, LANE), F32),
            pltpu.VMEM((heads, tq, dv), F32),
        ],
        compiler_params=pltpu.CompilerParams(
            dimension_semantics=("parallel", "parallel", "arbitrary"), vmem_limit_bytes=VMEM_LIMIT),
        name="mqa_flash_" + mode,
    )(*args)


def _nsa_cmp_kernel(q_ref, kc_ref, vc_ref, oc_ref, sel_ref, *, tq, heads, hd, scale, slopes, pos0, n_sel, k_top):
    qi = pl.program_id(1)
    nc = kc_ref.shape[1]
    lsel = sel_ref.shape[1]
    pos = pos0 + qi * tq + lax.broadcasted_iota(jnp.int32, (tq, nc), 0)
    cmp_end = CMP_STRIDE * lax.broadcasted_iota(jnp.int32, (tq, nc), 1) + (CMP_LEN - 1)
    valid = cmp_end <= pos
    dist = (pos - cmp_end).astype(F32)
    kc = kc_ref[0].astype(BF16)
    vc = vc_ref[0].astype(BF16)
    psum = jnp.zeros((tq, nc), F32)
    for h in range(heads):
        qh = (q_ref[:, h * hd:(h + 1) * hd] * scale).astype(BF16)
        s = lax.dot_general(qh, kc, (((1,), (1,)), ((), ())), preferred_element_type=F32) - slopes[h] * dist
        s = jnp.where(valid, s, NEG)
        e = jnp.exp(s - jnp.max(s, axis=-1, keepdims=True))
        p = jnp.where(valid, e / jnp.sum(e, axis=-1, keepdims=True), 0.0)
        oc_ref[:, h * hd:(h + 1) * hd] = jnp.dot(p.astype(BF16), vc, preferred_element_type=F32)
        psum = psum + p
    c_start = CMP_STRIDE * lax.broadcasted_iota(jnp.int32, (nc, lsel), 0)
    s_start = SEL_BLOCK * lax.broadcasted_iota(jnp.int32, (nc, lsel), 1)
    overlap = ((c_start < s_start + SEL_BLOCK) & (c_start + CMP_LEN > s_start)).astype(F32)
    imp = jnp.dot(psum, overlap, preferred_element_type=F32, precision=lax.Precision.HIGHEST)
    tpos = pos0 + qi * tq + lax.broadcasted_iota(jnp.int32, (tq, lsel), 0)
    j = lax.broadcasted_iota(jnp.int32, (tq, lsel), 1)
    cur = tpos // SEL_BLOCK
    forced = (j == 0) | (j == cur) | (j == cur - 1)
    cand = (SEL_BLOCK * j <= tpos) & (j < n_sel)
    score = jnp.where(cand, imp + jnp.where(forced, FORCE_BONUS, 0.0), NEG)
    work = score
    chosen = jnp.zeros((tq, lsel), F32)
    for _ in range(k_top):
        mx = jnp.max(work, axis=-1, keepdims=True)
        first = jnp.min(jnp.where(work == mx, j, lsel), axis=-1, keepdims=True)
        hit = j == first
        chosen = jnp.where(hit, 1.0, chosen)
        work = jnp.where(hit, -3.0e38, work)
    sel_ref[...] = jnp.where(score > 0.5 * NEG, chosen, 0.0)


def nsa_cmp_select(q, q_col, k_cmp, v_cmp, b, t, pos0, n_sel):
    tq = min(ATT_TILE, t)
    nq = t // tq
    qw = NSA_HEADS * NSA_HD
    nc = k_cmp.shape[1]
    lsel = -(-n_sel // LANE) * LANE
    assert t % tq == 0 and q_col % qw == 0
    kern = functools.partial(_nsa_cmp_kernel, tq=tq, heads=NSA_HEADS, hd=NSA_HD, scale=NSA_HD ** -0.5,
                             slopes=_alibi_slopes_np(NSA_HEADS), pos0=pos0, n_sel=n_sel, k_top=min(N_SEL, n_sel))
    return pl.pallas_call(
        kern,
        grid=(b, nq),
        in_specs=[
            pl.BlockSpec((tq, qw), lambda bi, qi: (bi * nq + qi, q_col // qw)),
            pl.BlockSpec((1, nc, NSA_HD), lambda bi, qi: (bi, 0, 0)),
            pl.BlockSpec((1, nc, NSA_HD), lambda bi, qi: (bi, 0, 0)),
        ],
        out_specs=[
            pl.BlockSpec((tq, qw), lambda bi, qi: (bi * nq + qi, 0)),
            pl.BlockSpec((tq, lsel), lambda bi, qi: (bi * nq + qi, 0)),
        ],
        out_shape=[jax.ShapeDtypeStruct((b * t, qw), F32), jax.ShapeDtypeStruct((b * t, lsel), F32)],
        compiler_params=pltpu.CompilerParams(
            dimension_semantics=("parallel", "parallel"), vmem_limit_bytes=VMEM_LIMIT),
        name="nsa_cmp_select",
    )(q, k_cmp, v_cmp)


def nsa_compress(ck, pe, wc1, wc2):
    b, n_chunk = ck.shape[:2]
    feat = CMP_STRIDE * NSA_HD
    x = jnp.moveaxis(ck, 3, 0).reshape(2, b, n_chunk, feat)
    nxt = jnp.concatenate([x[:, :, 1:], jnp.zeros((2, b, 1, feat), x.dtype)], axis=2)
    xcat = jnp.concatenate([x, nxt], axis=-1).reshape(2, b * n_chunk, 2 * feat)
    pe_flat = jnp.pad(pe.reshape(2, 1, 2 * feat), ((0, 0), (0, 7), (0, 0)))
    out = []
    for kind in range(2):
        pe_term = rms_matmul(pe_flat[kind], wc1[kind])[0]
        hid = rms_matmul(xcat[kind], wc1[kind], bias=pe_term, act="gelu")
        out.append(rms_matmul(hid, wc2[kind]).reshape(b, n_chunk, NSA_HD))
    return out[0], out[1]


def _router_kernel(x_ref, g_ref, w_ref, b_ref, lg_ref, h_ref):
    x = x_ref[...]
    ms = jnp.mean(x * x, axis=-1, keepdims=True)
    h = (x * lax.rsqrt(ms + EPS)) * g_ref[...]
    h_ref[...] = h.astype(BF16)
    lg_ref[...] = jnp.dot(h, w_ref[...], preferred_element_type=F32,
                          precision=lax.Precision.HIGHEST) + b_ref[...]


def _experts_kernel(te_ref, tv_ref, xs_ref, sw_ref, w1_ref, w3_ref, w2_ref, o_ref):
    i = pl.program_id(0)

    @pl.when(tv_ref[i] != 0)
    def _():
        x = xs_ref[...]
        a = jnp.dot(x, w1_ref[0], preferred_element_type=F32)
        c = jnp.dot(x, w3_ref[0], preferred_element_type=F32)
        hid = (jax.nn.silu(a) * c) * sw_ref[...]
        o_ref[...] = jnp.dot(hid.astype(BF16), w2_ref[0], preferred_element_type=F32)

    @pl.when(tv_ref[i] == 0)
    def _():
        o_ref[...] = jnp.zeros_like(o_ref)


def moe(x, g, w_rg, b_rg, w_re, b_re, w_e1, w_e3, w_e2):
    n, d = x.shape
    ne = N_GROUPS * N_EXP
    tm = _pick_tile(n, (ROW_TILE, 256, 128))
    wr = jnp.zeros((d, LANE), F32).at[:, :N_GROUPS].set(w_rg).at[:, N_GROUPS:N_GROUPS + ne].set(w_re)
    br = jnp.zeros((1, LANE), F32).at[0, :N_GROUPS].set(b_rg).at[0, N_GROUPS:N_GROUPS + ne].set(b_re)
    logits, h = pl.pallas_call(
        _router_kernel,
        grid=(n // tm,),
        in_specs=[
            pl.BlockSpec((tm, d), lambda i: (i, 0)),
            pl.BlockSpec((1, d), lambda i: (0, 0)),
            pl.BlockSpec((d, LANE), lambda i: (0, 0)),
            pl.BlockSpec((1, LANE), lambda i: (0, 0)),
        ],
        out_specs=[pl.BlockSpec((tm, LANE), lambda i: (i, 0)), pl.BlockSpec((tm, d), lambda i: (i, 0))],
        out_shape=[jax.ShapeDtypeStruct((n, LANE), F32), jax.ShapeDtypeStruct((n, d), BF16)],
        compiler_params=pltpu.CompilerParams(dimension_semantics=("parallel",), vmem_limit_bytes=VMEM_LIMIT),
        name="moe_router",
    )(x, g.astype(F32).reshape(1, d), wr, br)

    lg = logits[:, :N_GROUPS]
    le = logits[:, N_GROUPS:N_GROUPS + ne].reshape(n, N_GROUPS, N_EXP)
    pg = jax.nn.softmax(lg, axis=-1)
    gi = jnp.argmax(lg, axis=-1)
    gw = jnp.take_along_axis(pg, gi[:, None], axis=1)[:, 0]
    le_g = jnp.take_along_axis(le, gi[:, None, None], axis=1)[:, 0]
    top_v, top_i = lax.top_k(le_g, TOP_K)
    we = jax.nn.softmax(top_v, axis=-1)
    eid = (gi[:, None] * N_EXP + top_i).astype(jnp.int32)
    wt = we * gw[:, None]

    ts = EXPERT_TILE
    n_tiles = -(-(TOP_K * n) // ts) + ne
    member = (eid[:, :, None] == jnp.arange(ne)[None, None, :]).any(axis=1).astype(jnp.int32)
    cnt = member.sum(axis=0)
    rank = jnp.cumsum(member, axis=0) - member
    tiles_e = (cnt + ts - 1) // ts
    tile_end = jnp.cumsum(tiles_e)
    pad_off = (tile_end - tiles_e) * ts
    pos = pad_off[eid] + jnp.take_along_axis(rank, eid, axis=1)
    tok = jnp.broadcast_to(jnp.arange(n, dtype=jnp.int32)[:, None], (n, TOP_K))
    slot_tok = jnp.zeros((n_tiles * ts,), jnp.int32).at[pos.reshape(-1)].set(tok.reshape(-1))
    slot_w = jnp.zeros((n_tiles * ts,), F32).at[pos.reshape(-1)].set(wt.reshape(-1))
    tile_id = jnp.arange(n_tiles, dtype=jnp.int32)
    tile_e = jnp.minimum(jnp.searchsorted(tile_end, tile_id, side="right"), ne - 1).astype(jnp.int32)
    tile_v = (tile_id < tile_end[-1]).astype(jnp.int32)

    xs = jnp.take(h, slot_tok, axis=0)
    w1 = w_e1.reshape(ne, d, D_EXP).astype(BF16)
    w3 = w_e3.reshape(ne, d, D_EXP).astype(BF16)
    w2 = w_e2.reshape(ne, D_EXP, d).astype(BF16)
    ys = pl.pallas_call(
        _experts_kernel,
        grid_spec=pltpu.PrefetchScalarGridSpec(
            num_scalar_prefetch=2,
            grid=(n_tiles,),
            in_specs=[
                pl.BlockSpec((ts, d), lambda i, te, tv: (i, 0)),
                pl.BlockSpec((ts, 1), lambda i, te, tv: (i, 0)),
                pl.BlockSpec((1, d, D_EXP), lambda i, te, tv: (te[i], 0, 0)),
                pl.BlockSpec((1, d, D_EXP), lambda i, te, tv: (te[i], 0, 0)),
                pl.BlockSpec((1, D_EXP, d), lambda i, te, tv: (te[i], 0, 0)),
            ],
            out_specs=pl.BlockSpec((ts, d), lambda i, te, tv: (i, 0)),
        ),
        out_shape=jax.ShapeDtypeStruct((n_tiles * ts, d), F32),
        compiler_params=pltpu.CompilerParams(dimension_semantics=("arbitrary",), vmem_limit_bytes=VMEM_LIMIT),
        name="moe_experts",
    )(tile_e, tile_v, xs, slot_w.reshape(-1, 1), w1, w3, w2)
    return jnp.take(ys, pos[:, 0], axis=0) + jnp.take(ys, pos[:, 1], axis=0)


def rmsnorm(x, g):
    xf = x.astype(F32)
    y = xf * lax.rsqrt(jnp.mean(xf * xf, axis=-1, keepdims=True) + EPS)
    return (y * g.astype(F32)).astype(x.dtype)


def split_cols(a, widths):
    return jnp.split(a, np.cumsum(widths)[:-1].tolist(), axis=-1)


def qblock(t):
    return min(QBLOCK, t)


def map_query_blocks(fn, arrays, block):
    t = arrays[0].shape[1]
    nb = -(-t // block)
    pad = nb * block - t

    def to_blocks(a):
        a = jnp.pad(a, [(0, 0), (0, pad)] + [(0, 0)] * (a.ndim - 2))
        a = a.reshape((a.shape[0], nb, block) + a.shape[2:])
        return jnp.moveaxis(a, 1, 0)

    xs = (jnp.arange(nb),) + tuple(to_blocks(a) for a in arrays)
    out = lax.map(lambda args: fn(*args), xs)
    out = jnp.moveaxis(out, 0, 1)
    out = out.reshape((out.shape[0], nb * block) + out.shape[3:])
    return out[:, :t]


def rope_angles(pos, dim):
    inv = ROPE_THETA ** (-jnp.arange(0, dim, 2, dtype=F32) / dim)
    ang = pos.astype(F32)[:, None] * inv[None, :]
    return jnp.cos(ang), jnp.sin(ang)


def apply_rope(x, cos, sin):
    half = x.shape[-1] // 2
    x1 = x[..., :half].astype(F32)
    x2 = x[..., half:].astype(F32)
    return jnp.concatenate([x1 * cos - x2 * sin, x1 * sin + x2 * cos], axis=-1).astype(x.dtype)


def alibi_slopes(n):
    return 2.0 ** (-8.0 * jnp.arange(1, n + 1, dtype=F32) / n)


def gather_pages(pool, page_table):
    g = pool[page_table]
    return g.reshape((page_table.shape[0], page_table.shape[1] * pool.shape[1]) + pool.shape[2:])


def mla_mixer(c_q, c_kv_raw, k_r_raw, pos, past, g_cq, w_uq, g_ckv, w_uk, w_uv):
    b, t, _ = c_q.shape
    cos, sin = rope_angles(pos, ROPE_DIM)
    q = jnp.einsum('btr,rhd->bthd', rmsnorm(c_q, g_cq), w_uq)
    q_nope = q[..., :NOPE_DIM]
    q_rope = apply_rope(q[..., NOPE_DIM:], cos[None, :, None, :], sin[None, :, None, :])
    q_lat = jnp.einsum('bthd,rhd->bthr', q_nope, w_uk)
    c_kv = rmsnorm(c_kv_raw, g_ckv)
    k_r = apply_rope(k_r_raw, cos[None], sin[None])
    new_rows = jnp.concatenate([c_kv, k_r], axis=-1)
    keys = jnp.concatenate([past, new_rows], axis=1)
    k_pos = jnp.arange(keys.shape[1])
    lat, kr = keys[..., :KV_RANK], keys[..., KV_RANK:]
    scale = (NOPE_DIM + ROPE_DIM) ** -0.5

    def blk(bi, ql, qr, qp):
        s = (jnp.einsum('bqhr,bsr->bhqs', ql, lat) + jnp.einsum('bqhd,bsd->bhqs', qr, kr)).astype(F32) * scale
        mask = k_pos[None, :] <= qp[0][:, None]
        p = jax.nn.softmax(jnp.where(mask, s, NEG), axis=-1).astype(lat.dtype)
        return jnp.einsum('bhqs,bsr->bqhr', p, lat)

    o_lat = map_query_blocks(blk, (q_lat, q_rope, pos[None]), qblock(t))
    out = jnp.einsum('bthr,rhd->bthd', o_lat, w_uv).reshape(b, t, MLA_HEADS * MLA_VDIM)
    return out, new_rows


def mlstm_mixer(q, k, v, ig, fg, og, c0, n0, m0, g_mh):
    b, t = q.shape[:2]
    dt = q.dtype
    q = q.reshape(b, t, ML_HEADS, ML_QK)
    k = k.reshape(b, t, ML_HEADS, ML_QK) * (ML_QK ** -0.5)
    v = v.reshape(b, t, ML_HEADS, ML_V)
    li = ig.astype(F32)
    lf = jax.nn.log_sigmoid(fg.astype(F32))
    L = min(ML_CHUNK, t)
    nc = -(-t // L)
    pad = nc * L - t

    def chunks(a, val=0.0):
        a = jnp.pad(a, [(0, 0), (0, pad)] + [(0, 0)] * (a.ndim - 2), constant_values=val)
        return jnp.moveaxis(a.reshape((b, nc, L) + a.shape[2:]), 1, 0)

    tri = jnp.tril(jnp.ones((L, L), dtype=bool))

    def step(carry, xs):
        c, n, m = carry
        qc, kc, vc, lic, lfc = xs
        cb = jnp.cumsum(lfc, axis=1)
        a = cb + m[:, None, :]
        d = cb[:, :, None, :] - cb[:, None, :, :] + lic[:, None, :, :]
        d = jnp.where(tri[None, :, :, None], d, NEG)
        mt = jnp.maximum(a, d.max(axis=2))
        w = jnp.exp(d - mt[:, :, None, :])
        inter = jnp.exp(a - mt)
        sc = w * jnp.einsum('bthd,bshd->btsh', qc, kc).astype(F32)
        num = jnp.einsum('btsh,bshv->bthv', sc, vc) + inter[..., None] * jnp.einsum('bhvd,bthd->bthv', c, qc)
        den = sc.sum(axis=2) + inter * jnp.einsum('bhd,bthd->bth', n, qc)
        h = num / jnp.maximum(jnp.abs(den), jnp.exp(-mt))[..., None]
        wl, il = w[:, -1], inter[:, -1]
        c = il[..., None, None] * c + jnp.einsum('bsh,bshv,bshd->bhvd', wl, vc, kc)
        n = il[..., None] * n + jnp.einsum('bsh,bshd->bhd', wl, kc)
        return (c, n, mt[:, -1]), h

    carry0 = (c0.astype(F32), n0.astype(F32), m0.astype(F32))
    (c, n, m), h = lax.scan(step, carry0, (chunks(q), chunks(k), chunks(v), chunks(li, NEG), chunks(lf)))
    h = jnp.moveaxis(h, 0, 1).reshape(b, nc * L, ML_HEADS, ML_V)[:, :t]
    h = h - h.mean(axis=-1, keepdims=True)
    h = h * lax.rsqrt(jnp.mean(h * h, axis=-1, keepdims=True) + EPS) * g_mh.astype(F32)
    out = jax.nn.sigmoid(og.astype(F32)).reshape(b, t, ML_HEADS, ML_V) * h
    return out.reshape(b, t, ML_HEADS * ML_V).astype(dt), c, n, m


def sb_mixer(q, k_new, v_new, pos, past_kv):
    b, t = q.shape[:2]
    q = q.reshape(b, t, SB_KV_HEADS, SB_HEADS // SB_KV_HEADS, SB_HD)
    new_rows = jnp.stack([k_new.reshape(b, t, SB_KV_HEADS, SB_HD), v_new.reshape(b, t, SB_KV_HEADS, SB_HD)], axis=2)
    kv = jnp.concatenate([past_kv, new_rows], axis=1)
    k, v = kv[:, :, 0], kv[:, :, 1]
    k_pos = jnp.arange(kv.shape[1])

    def blk(bi, qh, qp):
        z = jnp.einsum('bqgrd,bsgd->bgrqs', qh, k).astype(F32) * (SB_HD ** -0.5)
        mask = k_pos[None, :] < qp[0][:, None]
        l1mb = jnp.where(mask, jax.nn.log_sigmoid(-z), 0.0)
        after = lax.cumsum(l1mb, axis=z.ndim - 1, reverse=True) - l1mb
        att = jnp.where(mask, jnp.exp(jax.nn.log_sigmoid(z) + after), 0.0).astype(v.dtype)
        return jnp.einsum('bgrqs,bsgd->bqgrd', att, v)

    o = map_query_blocks(blk, (q, pos[None]), qblock(t))
    return o.reshape(b, t, SB_HEADS * SB_HD), new_rows


def nsa_mixer(q, kv_new, win_new, gate, pos, past_kv, win_prior, pe, wc1, wc2):
    b, t = q.shape[:2]
    past = past_kv.shape[1]
    s_len = past + t
    scale = NSA_HD ** -0.5
    slopes = alibi_slopes(NSA_HEADS)
    kv_all = jnp.concatenate([past_kv, kv_new], axis=1)
    n_chunk = max(-(-s_len // CMP_STRIDE), 2)
    ck = jnp.pad(kv_all[:, :, :2], ((0, 0), (0, n_chunk * CMP_STRIDE - s_len), (0, 0), (0, 0)))
    ck = ck.reshape(b, n_chunk, CMP_STRIDE, 2, NSA_HD)
    w1 = wc1.reshape(2, 2, CMP_STRIDE, NSA_HD, CMP_HID)
    proj = jnp.einsum('bcrkd,kzrdh->bczkh', ck, w1)
    pe_term = jnp.einsum('kpd,kpdh->kh', pe, wc1.reshape(2, CMP_LEN, NSA_HD, CMP_HID))
    hid = jax.nn.gelu(proj[:, :-1, 0] + proj[:, 1:, 1] + pe_term)
    cmp = jnp.einsum('bnkh,khd->bnkd', hid, wc2)
    k_cmp, v_cmp = cmp[:, :, 0], cmp[:, :, 1]
    cmp_start = CMP_STRIDE * jnp.arange(n_chunk - 1)
    cmp_end = cmp_start + CMP_LEN - 1
    n_sel = -(-s_len // SEL_BLOCK)
    sel = jnp.pad(kv_all[:, :, 2:], ((0, 0), (0, n_sel * SEL_BLOCK - s_len), (0, 0), (0, 0)))
    sel = sel.reshape(b, n_sel, SEL_BLOCK, 2, NSA_HD)
    sel_start = SEL_BLOCK * jnp.arange(n_sel)
    overlap = ((cmp_start[:, None] < sel_start[None, :] + SEL_BLOCK) & (cmp_start[:, None] + CMP_LEN > sel_start[None, :])).astype(F32)
    k_top = min(N_SEL, n_sel)
    bidx = jnp.arange(b)[:, None, None]
    pw = win_prior.shape[1]
    qb = qblock(t)
    nb = -(-t // qb)
    band = jnp.concatenate([jnp.zeros((b, WINDOW - pw, 2, NSA_HD), win_new.dtype), win_prior, win_new, jnp.zeros((b, nb * qb - t, 2, NSA_HD), win_new.dtype)], axis=1)
    band_pos = past - WINDOW + jnp.arange(WINDOW + nb * qb)

    def blk(bi, qh, gh, qp):
        tq = qp[0]
        valid_c = cmp_end[None, :] <= tq[:, None]
        dist_c = (tq[:, None] - cmp_end[None, :]).astype(F32)
        s_c = jnp.einsum('bqhd,bnd->bhqn', qh, k_cmp).astype(F32) * scale - slopes[:, None, None] * dist_c
        p_c = jnp.where(valid_c, jax.nn.softmax(jnp.where(valid_c, s_c, NEG), axis=-1), 0.0)
        o_c = jnp.einsum('bhqn,bnd->bqhd', p_c.astype(v_cmp.dtype), v_cmp)
        imp = jnp.einsum('bhqn,nj->bqj', p_c, overlap)
        cur = tq // SEL_BLOCK
        j = jnp.arange(n_sel)
        forced = (j[None, :] == 0) | (j[None, :] == cur[:, None]) | (j[None, :] == cur[:, None] - 1)
        cand = sel_start[None, :] <= tq[:, None]
        score = jnp.where(cand[None], imp + jnp.where(forced, FORCE_BONUS, 0.0)[None], NEG)
        top_s, idx = lax.top_k(score, k_top)
        kv_sel = sel[bidx, idx]
        kpos = idx[..., None] * SEL_BLOCK + jnp.arange(SEL_BLOCK)
        ok = (top_s > 0.5 * NEG)[..., None] & (kpos <= tq[None, :, None, None])
        dist_s = (tq[None, :, None, None] - kpos).astype(F32)
        s_s = jnp.einsum('bqhd,bqkld->bqhkl', qh, kv_sel[..., 0, :]).astype(F32) * scale - slopes[None, None, :, None, None] * dist_s[:, :, None]
        s_s = jnp.where(ok[:, :, None], s_s, NEG)
        p_s = jax.nn.softmax(s_s.reshape(s_s.shape[:3] + (-1,)), axis=-1).reshape(s_s.shape)
        o_s = jnp.einsum('bqhkl,bqkld->bqhd', p_s.astype(kv_sel.dtype), kv_sel[..., 1, :])
        bw = lax.dynamic_slice_in_dim(band, bi * qb, qb + WINDOW, axis=1)
        bp = lax.dynamic_slice_in_dim(band_pos, bi * qb, qb + WINDOW, axis=0)
        dist_w = tq[:, None] - bp[None, :]
        ok_w = (bp[None, :] >= 0) & (dist_w >= 0) & (dist_w < WINDOW)
        s_w = jnp.einsum('bqhd,bsd->bhqs', qh, bw[:, :, 0]).astype(F32) * scale - slopes[:, None, None] * dist_w.astype(F32)
        p_w = jax.nn.softmax(jnp.where(ok_w, s_w, NEG), axis=-1)
        o_w = jnp.einsum('bhqs,bsd->bqhd', p_w.astype(bw.dtype), bw[:, :, 1])
        g = jax.nn.sigmoid(gh.astype(F32))
        return (g[..., 0:1] * o_c + g[..., 1:2] * o_s + g[..., 2:3] * o_w).astype(qh.dtype)

    o = map_query_blocks(blk, (q, gate, pos[None]), qb)
    new_win = jnp.concatenate([win_prior, win_new], axis=1)[:, -min(WINDOW, pw + t):]
    return o.reshape(b, t, NSA_HEADS * NSA_HD), kv_new, new_win


def mem_kv_rows(mem, g, wk, wv):
    b, nm, d = mem.shape
    kv = rms_matmul(mem.reshape(b * nm, d), jnp.concatenate([wk, wv], axis=1), g=g)
    hd = X_HEADS * X_HD
    k = kv[:, :hd].reshape(b, nm, X_HEADS, X_HD)
    v = kv[:, hd:].reshape(b, nm, X_HEADS, X_HD)
    return jnp.stack([k, v], axis=2)


def mem_attend(x, g, mkv, wq, wo):
    b, t, d = x.shape
    q = rms_matmul(x.reshape(b * t, d), wq, g=g).reshape(b, t, X_HEADS, X_HD)
    s = jnp.einsum('bthd,bmhd->bhtm', q, mkv[:, :, 0]).astype(F32) * (X_HD ** -0.5)
    p = jax.nn.softmax(s, axis=-1).astype(mkv.dtype)
    o = jnp.einsum('bhtm,bmhd->bthd', p, mkv[:, :, 1]).reshape(b * t, X_HEADS * X_HD)
    return rms_matmul(o, wo, res=x.reshape(b * t, d)).reshape(b, t, d)


def odd_mixers_prompt(proj, b, t, pe, wc1, wc2):
    n = b * t
    sb_w = SB_HEADS * SB_HD
    kv_w = SB_KV_HEADS * SB_HD
    q_col = sb_w + 2 * kv_w
    kv_col = q_col + NSA_HEADS * NSA_HD
    win_col = kv_col + 4 * NSA_HD
    gate_col = win_col + 2 * NSA_HD
    o_sb = sb_attn_prompt(proj, b, t, 0, sb_w, sb_w + kv_w)
    sb_rows = proj[:, sb_w:sb_w + 2 * kv_w].reshape(b, t, 2, SB_KV_HEADS, SB_HD)
    nsa_rows = proj[:, kv_col:win_col].reshape(b, t, 4, NSA_HD)
    win_rows = proj[:, win_col:gate_col].reshape(b, t, 2, NSA_HD)
    ck = proj[:, kv_col:kv_col + 2 * NSA_HD].reshape(b, t // CMP_STRIDE, CMP_STRIDE, 2, NSA_HD)
    k_cmp, v_cmp = nsa_compress(ck, pe, wc1, wc2)
    o_c, sel = nsa_cmp_select(proj, q_col, k_cmp, v_cmp, b, t, 0, t // SEL_BLOCK)
    slopes = _alibi_slopes_np(NSA_HEADS)
    common = dict(heads=NSA_HEADS, dk=NSA_HD, dv=NSA_HD, k_off=0, v_off=NSA_HD, scale=NSA_HD ** -0.5, slopes=slopes)
    o_s = mqa_flash(proj, q_col, proj, kv_col + 2 * NSA_HD, 2 * NSA_HD, b, t, mode="select", sel=sel, **common)
    o_w = mqa_flash(proj, q_col, proj, win_col, 2 * NSA_HD, b, t, mode="window", **common)
    g = jax.nn.sigmoid(proj[:, gate_col:gate_col + 3 * NSA_HEADS]).reshape(n, NSA_HEADS, 3)
    sh = (n, NSA_HEADS, NSA_HD)
    o_nsa = (g[..., 0:1] * o_c.reshape(sh) + g[..., 1:2] * o_s.reshape(sh) + g[..., 2:3] * o_w.reshape(sh)).reshape(n, -1)
    return o_sb, sb_rows, o_nsa, nsa_rows, win_rows[:, -min(WINDOW, t):]


def trunk(x, start, mla_past, ml_state, sb_past, nsa_past, win_prior, mem_kv, p, is_prompt=False):
    b, t, d = x.shape
    pos = start + jnp.arange(t, dtype=jnp.int32)
    depth = p['g_mix'].shape[0]
    mla_rows, ml_c, ml_n, ml_m, sb_rows, nsa_rows, wins = [], [], [], [], [], [], []
    for l in range(depth):
        e = l // 2
        x2 = x.reshape(b * t, d)
        if l % 2 == 0:
            proj = rms_matmul(x2, p['w_in_even'][e], g=p['g_mix'][l]).reshape(b, t, -1)
            c_q, c_kv, k_r, mq, mk, mv, mi, mf, mo = split_cols(proj, EVEN_COLS)
            o_a, rows = mla_mixer(c_q, c_kv, k_r, pos, mla_past[e], p['g_cq'][e], p['w_uq'][e], p['g_ckv'][e], p['w_uk'][e], p['w_uv'][e])
            c0, n0, m0 = ml_state[e]
            o_b, c, n, m = mlstm_mixer(mq, mk, mv, mi + p['b_ml_i'][e], mf + p['b_ml_f'][e], mo, c0, n0, m0, p['g_mh'][e])
            mla_rows.append(rows)
            ml_c.append(c)
            ml_n.append(n)
            ml_m.append(m)
            mix = jnp.concatenate([o_a, o_b], axis=-1).reshape(b * t, -1)
            x = rms_matmul(mix, p['w_out_even'][e], res=x2).reshape(b, t, d)
        elif is_prompt:
            proj = rms_matmul(x2, p['w_in_odd'][e], g=p['g_mix'][l], keep_pad=True)
            o_c, srows, o_d, nrows, win = odd_mixers_prompt(proj, b, t, p['nsa_pe'][e], p['nsa_wc1'][e], p['nsa_wc2'][e])
            sb_rows.append(srows)
            nsa_rows.append(nrows)
            wins.append(win)
            mix = jnp.concatenate([o_c, o_d], axis=-1)
            x = rms_matmul(mix, p['w_out_odd'][e], res=x2).reshape(b, t, d)
        else:
            proj = rms_matmul(x2, p['w_in_odd'][e], g=p['g_mix'][l]).reshape(b, t, -1)
            sq, sk, sv, nq, nkv, nwin, ngate = split_cols(proj, ODD_COLS)
            o_c, srows = sb_mixer(sq, sk, sv, pos, sb_past[e])
            o_d, nrows, win = nsa_mixer(nq.reshape(b, t, NSA_HEADS, NSA_HD), nkv.reshape(b, t, 4, NSA_HD), nwin.reshape(b, t, 2, NSA_HD), ngate.reshape(b, t, NSA_HEADS, 3), pos, nsa_past[e], win_prior[e], p['nsa_pe'][e], p['nsa_wc1'][e], p['nsa_wc2'][e])
            sb_rows.append(srows)
            nsa_rows.append(nrows)
            wins.append(win)
            mix = jnp.concatenate([o_c, o_d], axis=-1).reshape(b * t, -1)
            x = rms_matmul(mix, p['w_out_odd'][e], res=x2).reshape(b, t, d)
        x = mem_attend(x, p['g_xattn'][l], mem_kv[l], p['w_xq'][l], p['w_xo'][l])
        x2 = x.reshape(b * t, d)
        x = (x2 + moe(x2, p['g_ffn'][l], p['w_rg'][l], p['b_rg'][l], p['w_re'][l], p['b_re'][l], p['w_e1'][l], p['w_e3'][l], p['w_e2'][l])).reshape(b, t, d)
    y = rmsnorm(x, p['g_final'])
    return y, mla_rows, ml_c, ml_n, ml_m, sb_rows, nsa_rows, wins


def kernel(x_prompt, x_sample, mem_prompt, cache_mla, state_mlstm_c, state_mlstm_n, state_mlstm_m, cache_sb_kv, cache_nsa_kv, state_nsa_win, cache_mem_kv, page_table, g_mix, w_in_even, b_ml_i, b_ml_f, g_cq, w_uq, g_ckv, w_uk, w_uv, g_mh, w_out_even, w_in_odd, nsa_pe, nsa_wc1, nsa_wc2, w_out_odd, g_xattn, g_memnorm, w_xq, w_xk, w_xv, w_xo, g_ffn, w_rg, b_rg, w_re, b_re, w_e1, w_e3, w_e2, g_final):
    p = dict(g_mix=g_mix, w_in_even=w_in_even, b_ml_i=b_ml_i, b_ml_f=b_ml_f, g_cq=g_cq, w_uq=w_uq, g_ckv=g_ckv, w_uk=w_uk, w_uv=w_uv, g_mh=g_mh, w_out_even=w_out_even, w_in_odd=w_in_odd, nsa_pe=nsa_pe, nsa_wc1=nsa_wc1, nsa_wc2=nsa_wc2, w_out_odd=w_out_odd, g_xattn=g_xattn, w_xq=w_xq, w_xo=w_xo, g_ffn=g_ffn, w_rg=w_rg, b_rg=b_rg, w_re=w_re, b_re=b_re, w_e1=w_e1, w_e3=w_e3, w_e2=w_e2, g_final=g_final)
    dt = x_prompt.dtype
    bp = x_prompt.shape[0]
    depth = g_mix.shape[0]
    n_even = (depth + 1) // 2
    n_odd = depth // 2
    mem_kv_list_p = [mem_kv_rows(mem_prompt, g_memnorm[l], w_xk[l], w_xv[l]) for l in range(depth)]
    y_prompt, mla_p, c_p, n_p, m_p, sb_p, nsa_p, win_pl = trunk(
        x_prompt, 0,
        [jnp.zeros((bp, 0, KV_RANK + ROPE_DIM), dt) for _ in range(n_even)],
        [(jnp.zeros((bp, ML_HEADS, ML_V, ML_QK), F32), jnp.zeros((bp, ML_HEADS, ML_QK), F32), jnp.zeros((bp, ML_HEADS), F32)) for _ in range(n_even)],
        [jnp.zeros((bp, 0, 2, SB_KV_HEADS, SB_HD), dt) for _ in range(n_odd)],
        [jnp.zeros((bp, 0, 4, NSA_HD), dt) for _ in range(n_odd)],
        [jnp.zeros((bp, 0, 2, NSA_HD), dt) for _ in range(n_odd)],
        mem_kv_list_p, p, is_prompt=True)
    past_len = page_table.shape[1] * cache_mla.shape[2]
    y_sample, mla_s, c_s, n_s, m_s, sb_s, nsa_s, win_sl = trunk(
        x_sample, past_len,
        [gather_pages(cache_mla[e], page_table) for e in range(n_even)],
        [(state_mlstm_c[e], state_mlstm_n[e], state_mlstm_m[e]) for e in range(n_even)],
        [gather_pages(cache_sb_kv[e], page_table) for e in range(n_odd)],
        [gather_pages(cache_nsa_kv[e], page_table) for e in range(n_odd)],
        [state_nsa_win[e] for e in range(n_odd)],
        [cache_mem_kv[l] for l in range(depth)], p)
    return (y_prompt, y_sample, jnp.stack(mla_p), jnp.stack(mla_s), jnp.stack(c_p), jnp.stack(c_s),
            jnp.stack(n_p), jnp.stack(n_s), jnp.stack(m_p), jnp.stack(m_s), jnp.stack(sb_p), jnp.stack(sb_s),
            jnp.stack(nsa_p), jnp.stack(nsa_s), jnp.stack(win_pl), jnp.stack(win_sl), jnp.stack(mem_kv_list_p))
```

```python
import functools

import jax
import jax.numpy as jnp
import numpy as np
from jax import lax
from jax.experimental import pallas as pl
from jax.experimental.pallas import tpu as pltpu

F32 = jnp.float32
BF16 = jnp.bfloat16

D_MODEL = 2048
QBLOCK = 128
NEG = -1e30
EPS = 1e-6
MLA_HEADS = 8
Q_RANK = 512
KV_RANK = 256
NOPE_DIM = 128
ROPE_DIM = 64
MLA_VDIM = 128
ROPE_THETA = 10000.0
ML_HEADS = 4
ML_QK = 128
ML_V = 256
ML_CHUNK = 64
SB_HEADS = 8
SB_KV_HEADS = 4
SB_HD = 64
NSA_HEADS = 16
NSA_HD = 64
CMP_STRIDE = 16
CMP_LEN = 2 * CMP_STRIDE
CMP_HID = 128
SEL_BLOCK = 64
N_SEL = 16
WINDOW = 512
FORCE_BONUS = 1000.0
X_HEADS = 4
X_HD = 128
N_GROUPS = 4
N_EXP = 8
TOP_K = 2
D_EXP = 512

EVEN_COLS = (Q_RANK, KV_RANK, ROPE_DIM, ML_HEADS * ML_QK, ML_HEADS * ML_QK, ML_HEADS * ML_V, ML_HEADS, ML_HEADS, ML_HEADS * ML_V)
ODD_COLS = (SB_HEADS * SB_HD, SB_KV_HEADS * SB_HD, SB_KV_HEADS * SB_HD, NSA_HEADS * NSA_HD, 4 * NSA_HD, 2 * NSA_HD, 3 * NSA_HEADS)

LANE = 128
VMEM_LIMIT = 56 * 1024 * 1024
ROW_TILE = 512
EXPERT_TILE = 256


def _pick_tile(n, candidates):
    for c in candidates:
        if n % c == 0:
            return c
    raise ValueError(f"no tile in {candidates} divides {n}")


def _rms_matmul_kernel(*refs, do_norm, has_bias, has_res, act, pre_act):
    x_ref, g_ref, w_ref = refs[:3]
    rest = list(refs[3:])
    b_ref = rest.pop(0) if has_bias else None
    r_ref = rest.pop(0) if has_res else None
    o_ref, xn_ref = rest

    @pl.when(pl.program_id(1) == 0)
    def _():
        x = x_ref[...].astype(F32)
        if do_norm:
            ms = jnp.mean(x * x, axis=-1, keepdims=True)
            x = (x * lax.rsqrt(ms + EPS)) * g_ref[...]
        if pre_act == "gelu":
            x = jax.nn.gelu(x)
        xn_ref[...] = x.astype(BF16)

    acc = jnp.dot(xn_ref[...], w_ref[...], preferred_element_type=F32)
    if has_bias:
        acc = acc + b_ref[...]
    if act == "gelu":
        acc = jax.nn.gelu(acc)
    if has_res:
        acc = acc + r_ref[...]
    o_ref[...] = acc


def rms_matmul(x, w, g=None, res=None, bias=None, act=None, pre_act=None, keep_pad=False):
    n, k = x.shape
    m = w.shape[1]
    mp = -(-m // LANE) * LANE
    wb = w.astype(BF16)
    if mp != m:
        wb = jnp.pad(wb, ((0, 0), (0, mp - m)))
        if res is not None:
            res = jnp.pad(res, ((0, 0), (0, mp - m)))
        if bias is not None:
            bias = jnp.pad(bias, ((0, mp - m),))
    tm = _pick_tile(n, (ROW_TILE, 256, 128, 64, 32, 16, 8))
    tn = _pick_tile(mp, (512, 384, 256, 128))
    do_norm = g is not None
    gg = (g if do_norm else jnp.ones((k,), F32)).astype(F32).reshape(1, k)
    in_specs = [
        pl.BlockSpec((tm, k), lambda i, j: (i, 0)),
        pl.BlockSpec((1, k), lambda i, j: (0, 0)),
        pl.BlockSpec((k, tn), lambda i, j: (0, j)),
    ]
    args = [x, gg, wb]
    if bias is not None:
        in_specs.append(pl.BlockSpec((1, tn), lambda i, j: (0, j)))
        args.append(bias.astype(F32).reshape(1, mp))
    if res is not None:
        in_specs.append(pl.BlockSpec((tm, tn), lambda i, j: (i, j)))
        args.append(res)
    out = pl.pallas_call(
        functools.partial(_rms_matmul_kernel, do_norm=do_norm, has_bias=bias is not None,
                          has_res=res is not None, act=act, pre_act=pre_act),
        grid=(n // tm, mp // tn),
        in_specs=in_specs,
        out_specs=pl.BlockSpec((tm, tn), lambda i, j: (i, j)),
        out_shape=jax.ShapeDtypeStruct((n, mp), F32),
        scratch_shapes=[pltpu.VMEM((tm, k), BF16)],
        compiler_params=pltpu.CompilerParams(
            dimension_semantics=("parallel", "arbitrary"), vmem_limit_bytes=VMEM_LIMIT),
        name="rms_matmul",
    )(*args)
    return out if (keep_pad or mp == m) else out[:, :m]


ATT_TILE = 256


def _sb_prompt_kernel(q_ref, k_ref, v_ref, o_ref, qs_ref, acc_ref, run_ref, *, tq, groups, rep, hd, scale):
    qi = pl.program_id(1)
    kk = pl.program_id(2)
    nk = pl.num_programs(2)

    @pl.when(kk == 0)
    def _():
        for gi in range(groups):
            for ri in range(rep):
                c0 = (gi * rep + ri) * hd
                qs_ref[gi, ri * tq:(ri + 1) * tq, :] = (q_ref[:, c0:c0 + hd] * scale).astype(BF16)
        acc_ref[...] = jnp.zeros_like(acc_ref)
        run_ref[...] = jnp.zeros_like(run_ref)

    @pl.when(kk <= qi)
    def _():
        rows = rep * tq
        row_t = lax.broadcasted_iota(jnp.int32, (rows, tq), 0) % tq
        col = lax.broadcasted_iota(jnp.int32, (rows, tq), 1)
        mask = (col < row_t) | (kk > 0)
        later = (lax.broadcasted_iota(jnp.int32, (tq, tq), 0) >
                 lax.broadcasted_iota(jnp.int32, (tq, tq), 1)).astype(BF16)
        for gi in range(groups):
            kt = k_ref[:, gi * hd:(gi + 1) * hd].astype(BF16)
            vt = v_ref[:, gi * hd:(gi + 1) * hd].astype(BF16)
            z = lax.dot_general(qs_ref[gi], kt, (((1,), (1,)), ((), ())), preferred_element_type=F32)
            ls = jnp.minimum(z, 0.0) - jnp.log(1.0 + jnp.exp(-jnp.abs(z)))
            l1mb = jnp.where(mask, ls - z, 0.0)
            hi = l1mb.astype(BF16)
            lo = (l1mb - hi.astype(F32)).astype(BF16)
            aft = jnp.dot(hi, later, preferred_element_type=F32) + jnp.dot(lo, later, preferred_element_type=F32)
            run = run_ref[gi][:, :1]
            att = jnp.where(mask, jnp.exp(ls + aft + run), 0.0)
            acc_ref[gi] += jnp.dot(att.astype(BF16), vt, preferred_element_type=F32)
            run_ref[gi] = jnp.broadcast_to(run + aft[:, :1] + l1mb[:, :1], (rows, LANE))

    @pl.when(kk == nk - 1)
    def _():
        for gi in range(groups):
            for ri in range(rep):
                c0 = (gi * rep + ri) * hd
                o_ref[:, c0:c0 + hd] = acc_ref[gi, ri * tq:(ri + 1) * tq, :]


def sb_attn_prompt(proj, b, t, q_col, k_col, v_col):
    tq = ATT_TILE
    nq = t // tq
    qw = SB_HEADS * SB_HD
    kw = SB_KV_HEADS * SB_HD
    rep = SB_HEADS // SB_KV_HEADS
    assert t % tq == 0 and q_col % qw == 0 and k_col % kw == 0 and v_col % kw == 0
    kern = functools.partial(_sb_prompt_kernel, tq=tq, groups=SB_KV_HEADS, rep=rep, hd=SB_HD, scale=SB_HD ** -0.5)
    return pl.pallas_call(
        kern,
        grid=(b, nq, nq),
        in_specs=[
            pl.BlockSpec((tq, qw), lambda bi, qi, kk: (bi * nq + qi, q_col // qw)),
            pl.BlockSpec((tq, kw), lambda bi, qi, kk: (bi * nq + jnp.maximum(qi - kk, 0), k_col // kw)),
            pl.BlockSpec((tq, kw), lambda bi, qi, kk: (bi * nq + jnp.maximum(qi - kk, 0), v_col // kw)),
        ],
        out_specs=pl.BlockSpec((tq, qw), lambda bi, qi, kk: (bi * nq + qi, 0)),
        out_shape=jax.ShapeDtypeStruct((b * t, qw), F32),
        scratch_shapes=[
            pltpu.VMEM((SB_KV_HEADS, rep * tq, SB_HD), BF16),
            pltpu.VMEM((SB_KV_HEADS, rep * tq, SB_HD), F32),
            pltpu.VMEM((SB_KV_HEADS, rep * tq, LANE), F32),
        ],
        compiler_params=pltpu.CompilerParams(
            dimension_semantics=("parallel", "parallel", "arbitrary"), vmem_limit_bytes=VMEM_LIMIT),
        name="sb_attn_prompt",
    )(proj, proj, proj)


def _alibi_slopes_np(n):
    return [float(v) for v in (np.float32(2.0) ** (np.float32(-8.0) * np.arange(1, n + 1, dtype=np.float32) / np.float32(n)))]


def _mqa_flash_kernel(*refs, mode, heads, dk, dv, k_off, v_off, tq, scale, slopes, window, sel_block, nks):
    if mode == "select":
        q_ref, kv_ref, sel_ref, o_ref, qs_ref, m_ref, l_ref, acc_ref = refs
    else:
        q_ref, kv_ref, o_ref, qs_ref, m_ref, l_ref, acc_ref = refs
    qi = pl.program_id(1)
    kk = pl.program_id(2)
    kj = qi - (nks - 1) + kk if mode == "window" else kk

    @pl.when(kk == 0)
    def _():
        for h in range(heads):
            qs_ref[h] = q_ref[:, h * dk:(h + 1) * dk].astype(BF16)
        m_ref[...] = jnp.full_like(m_ref, NEG)
        l_ref[...] = jnp.zeros_like(l_ref)
        acc_ref[...] = jnp.zeros_like(acc_ref)

    active = (kj >= 0) if mode == "window" else (kj <= qi)

    @pl.when(active)
    def _():
        q_pos = qi * tq + lax.broadcasted_iota(jnp.int32, (tq, tq), 0)
        k_pos = kj * tq + lax.broadcasted_iota(jnp.int32, (tq, tq), 1)
        dist_i = q_pos - k_pos
        mask = dist_i >= 0
        if mode == "window":
            mask = mask & (dist_i < window)
        if mode == "select":
            nblk = sel_ref.shape[1]
            blk_of_key = (kj * tq + lax.broadcasted_iota(jnp.int32, (nblk, tq), 1)) // sel_block
            expand = (lax.broadcasted_iota(jnp.int32, (nblk, tq), 0) == blk_of_key).astype(BF16)
            chosen = jnp.dot(sel_ref[...].astype(BF16), expand, preferred_element_type=F32)
            mask = mask & (chosen > 0.5)
        dist = dist_i.astype(F32)
        kt = kv_ref[:, k_off:k_off + dk].astype(BF16)
        vt = kv_ref[:, v_off:v_off + dv].astype(BF16)
        for h in range(heads):
            s = lax.dot_general(qs_ref[h], kt, (((1,), (1,)), ((), ())), preferred_element_type=F32) * scale
            if slopes is not None:
                s = s - slopes[h] * dist
            s = jnp.where(mask, s, NEG)
            m_old = m_ref[h][:, :1]
            m_new = jnp.maximum(m_old, jnp.max(s, axis=-1, keepdims=True))
            alpha = jnp.exp(m_old - m_new)
            p = jnp.exp(s - m_new)
            l_ref[h] = jnp.broadcast_to(alpha * l_ref[h][:, :1] + jnp.sum(p, axis=-1, keepdims=True), (tq, LANE))
            acc_ref[h] = alpha * acc_ref[h] + jnp.dot(p.astype(BF16), vt, preferred_element_type=F32)
            m_ref[h] = jnp.broadcast_to(m_new, (tq, LANE))

    @pl.when(kk == nks - 1)
    def _():
        for h in range(heads):
            o_ref[:, h * dv:(h + 1) * dv] = acc_ref[h] / l_ref[h][:, :1]


def mqa_flash(q, q_col, kv, kv_col, kv_w, b, t, *, mode, heads, dk, dv, k_off, v_off, scale, slopes=None, sel=None):
    tq = ATT_TILE
    nq = t // tq
    qw = heads * dk
    assert t % tq == 0 and q_col % qw == 0 and kv_col % kv_w == 0
    nks = (WINDOW // tq + 1) if mode == "window" else nq
    if mode == "window":
        kv_idx = lambda bi, qi, kk: (bi * nq + jnp.maximum(qi - (nks - 1) + kk, 0), kv_col // kv_w)
    else:
        kv_idx = lambda bi, qi, kk: (bi * nq + jnp.minimum(kk, qi), kv_col // kv_w)
    in_specs = [
        pl.BlockSpec((tq, qw), lambda bi, qi, kk: (bi * nq + qi, q_col // qw)),
        pl.BlockSpec((tq, kv_w), kv_idx),
    ]
    args = [q, kv]
    if mode == "select":
        in_specs.append(pl.BlockSpec((tq, sel.shape[1]), lambda bi, qi, kk: (bi * nq + qi, 0)))
        args.append(sel)
    kern = functools.partial(_mqa_flash_kernel, mode=mode, heads=heads, dk=dk, dv=dv, k_off=k_off, v_off=v_off,
                             tq=tq, scale=scale, slopes=slopes, window=WINDOW, sel_block=SEL_BLOCK, nks=nks)
    return pl.pallas_call(
        kern,
        grid=(b, nq, nks),
        in_specs=in_specs,
        out_specs=pl.BlockSpec((tq, heads * dv), lambda bi, qi, kk: (bi * nq + qi, 0)),
        out_shape=jax.ShapeDtypeStruct((b * t, heads * dv), F32),
        scratch_shapes=[
            pltpu.VMEM((heads, tq, dk), BF16),
            pltpu.VMEM((heads, tq, LANE), F32),
            pltpu.VMEM((heads, tq, LANE), F32),
            pltpu.VMEM((heads, tq, dv), F32),
        ],
        compiler_params=pltpu.CompilerParams(
            dimension_semantics=("parallel", "parallel", "arbitrary"), vmem_limit_bytes=VMEM_LIMIT),
        name="mqa_flash_" + mode,
    )(*args)


def _nsa_cmp_kernel(q_ref, kc_ref, vc_ref, oc_ref, sel_ref, *, tq, heads, hd, scale, slopes, pos0, n_sel, k_top):
    qi = pl.program_id(1)
    nc = kc_ref.shape[1]
    lsel = sel_ref.shape[1]
    pos = pos0 + qi * tq + lax.broadcasted_iota(jnp.int32, (tq, nc), 0)
    cmp_end = CMP_STRIDE * lax.broadcasted_iota(jnp.int32, (tq, nc), 1) + (CMP_LEN - 1)
    valid = cmp_end <= pos
    dist = (pos - cmp_end).astype(F32)
    kc = kc_ref[0].astype(BF16)
    vc = vc_ref[0].astype(BF16)
    psum = jnp.zeros((tq, nc), F32)
    for h in range(heads):
        qh = (q_ref[:, h * hd:(h + 1) * hd] * scale).astype(BF16)
        s = lax.dot_general(qh, kc, (((1,), (1,)), ((), ())), preferred_element_type=F32) - slopes[h] * dist
        s = jnp.where(valid, s, NEG)
        e = jnp.exp(s - jnp.max(s, axis=-1, keepdims=True))
        p = jnp.where(valid, e / jnp.sum(e, axis=-1, keepdims=True), 0.0)
        oc_ref[:, h * hd:(h + 1) * hd] = jnp.dot(p.astype(BF16), vc, preferred_element_type=F32)
        psum = psum + p
    sel_ref[...] = _select_blocks(psum, pos0, qi * tq, tq, nc, lsel, n_sel, k_top)


def nsa_cmp_select(q, q_col, k_cmp, v_cmp, b, t, pos0, n_sel):
    tq = min(ATT_TILE, t)
    nq = t // tq
    qw = NSA_HEADS * NSA_HD
    nc = k_cmp.shape[1]
    lsel = -(-n_sel // LANE) * LANE
    assert t % tq == 0 and q_col % qw == 0
    kern = functools.partial(_nsa_cmp_kernel, tq=tq, heads=NSA_HEADS, hd=NSA_HD, scale=NSA_HD ** -0.5,
                             slopes=_alibi_slopes_np(NSA_HEADS), pos0=pos0, n_sel=n_sel, k_top=min(N_SEL, n_sel))
    return pl.pallas_call(
        kern,
        grid=(b, nq),
        in_specs=[
            pl.BlockSpec((tq, qw), lambda bi, qi: (bi * nq + qi, q_col // qw)),
            pl.BlockSpec((1, nc, NSA_HD), lambda bi, qi: (bi, 0, 0)),
            pl.BlockSpec((1, nc, NSA_HD), lambda bi, qi: (bi, 0, 0)),
        ],
        out_specs=[
            pl.BlockSpec((tq, qw), lambda bi, qi: (bi * nq + qi, 0)),
            pl.BlockSpec((tq, lsel), lambda bi, qi: (bi * nq + qi, 0)),
        ],
        out_shape=[jax.ShapeDtypeStruct((b * t, qw), F32), jax.ShapeDtypeStruct((b * t, lsel), F32)],
        compiler_params=pltpu.CompilerParams(
            dimension_semantics=("parallel", "parallel"), vmem_limit_bytes=VMEM_LIMIT),
        name="nsa_cmp_select",
    )(q, k_cmp, v_cmp)


def nsa_compress(ck, pe, wc1, wc2):
    b, n_chunk = ck.shape[:2]
    feat = CMP_STRIDE * NSA_HD
    x = jnp.moveaxis(ck, 3, 0).reshape(2, b * n_chunk, feat)
    pe_flat = jnp.pad(pe.reshape(2, 1, 2 * feat), ((0, 0), (0, 7), (0, 0)))
    out = []
    for kind in range(2):
        pe_term = rms_matmul(pe_flat[kind], wc1[kind])[0]
        w_halves = jnp.concatenate([wc1[kind][:feat], wc1[kind][feat:]], axis=1)
        pr = rms_matmul(x[kind], w_halves).reshape(b, n_chunk, 2 * CMP_HID)
        nxt = jnp.concatenate([pr[:, 1:, CMP_HID:], jnp.zeros((b, 1, CMP_HID), F32)], axis=1)
        pre = (pr[:, :, :CMP_HID] + nxt + pe_term).reshape(b * n_chunk, CMP_HID)
        out.append(rms_matmul(pre, wc2[kind], pre_act="gelu").reshape(b, n_chunk, NSA_HD))
    return out[0], out[1]


PAGES_PER_STEP = 8


def _paged_mqa_kernel(pt_ref, *refs, mode, pg, t_new, dk, dv, k_off, v_off, scale, kbase, q_base, page, use_slopes):
    refs = list(refs)
    q_ref = refs.pop(0)
    slope_ref = refs.pop(0)
    sel_ref = refs.pop(0) if mode == "select" else None
    page_refs = [refs.pop(0) for _ in range(pg)]
    new_ref, o_ref, m_ref, l_ref, acc_ref = refs
    j = pl.program_id(1)
    n_steps = pl.num_programs(1)
    rows = q_ref.shape[1]

    @pl.when(j == 0)
    def _():
        m_ref[...] = jnp.full_like(m_ref, NEG)
        l_ref[...] = jnp.zeros_like(l_ref)
        acc_ref[...] = jnp.zeros_like(acc_ref)

    qb = q_ref[0].astype(BF16)
    q_pos = q_base + lax.broadcasted_iota(jnp.int32, (rows, page), 0) % t_new
    lane = lax.broadcasted_iota(jnp.int32, (rows, page), 1)

    def tile(kv, kpos0, is_new, carry):
        m_old, l_old, acc = carry
        kt = kv[:, k_off:k_off + dk].astype(BF16)
        vt = kv[:, v_off:v_off + dv].astype(BF16)
        s = lax.dot_general(qb, kt, (((1,), (1,)), ((), ())), preferred_element_type=F32) * scale
        dist_i = q_pos - (kpos0 + lane)
        if use_slopes:
            s = s - slope_ref[:, :1] * dist_i.astype(F32)
        mask = None
        if is_new:
            mask = (lane < t_new) & (dist_i >= 0)
        if mode == "window":
            wmask = dist_i < WINDOW
            mask = wmask if mask is None else mask & wmask
        if mode == "select":
            selr = sel_ref[0]
            blk = lax.broadcasted_iota(jnp.int32, selr.shape, 1)
            blk0 = kpos0 // SEL_BLOCK
            c0 = jnp.sum(jnp.where(blk == blk0, selr, 0.0), axis=1, keepdims=True)
            c1 = jnp.sum(jnp.where(blk == blk0 + 1, selr, 0.0), axis=1, keepdims=True)
            cmask = jnp.where(lane < SEL_BLOCK, c0, c1) > 0.5
            mask = cmask if mask is None else mask & cmask
        if mask is not None:
            s = jnp.where(mask, s, NEG)
        m_new = jnp.maximum(m_old, jnp.max(s, axis=-1, keepdims=True))
        alpha = jnp.exp(m_old - m_new)
        p = jnp.exp(s - m_new)
        l_new = alpha * l_old + jnp.sum(p, axis=-1, keepdims=True)
        acc = alpha * acc + jnp.dot(p.astype(BF16), vt, preferred_element_type=F32)
        return m_new, l_new, acc

    carry = (m_ref[:, :1], l_ref[:, :1], acc_ref[...])
    for i in range(pg):
        carry = tile(page_refs[i][0], kbase + (j * pg + i) * page, False, carry)
    m_ref[...] = jnp.broadcast_to(carry[0], m_ref.shape)
    l_ref[...] = jnp.broadcast_to(carry[1], l_ref.shape)
    acc_ref[...] = carry[2]

    @pl.when(j == n_steps - 1)
    def _():
        m_f, l_f, acc_f = tile(new_ref[0], q_base, True, (m_ref[:, :1], l_ref[:, :1], acc_ref[...]))
        o_ref[0] = acc_f / l_f


def paged_mqa(q, slopes_rows, pool, lane_blk, kv_w, page_table, new_tile, *, mode, t_new, dk, dv, k_off, v_off,
              scale, kbase, q_base, sel_rows=None, use_slopes=True):
    b, rows, _ = q.shape
    n_pages = page_table.shape[1]
    page = pool.shape[1]
    pg = min(PAGES_PER_STEP, n_pages)
    assert n_pages % pg == 0
    n_steps = n_pages // pg
    in_specs = [
        pl.BlockSpec((1, rows, dk), lambda bi, j, pt: (bi, 0, 0)),
        pl.BlockSpec((rows, 1), lambda bi, j, pt: (0, 0)),
    ]
    args = [q, slopes_rows]
    if mode == "select":
        in_specs.append(pl.BlockSpec((1, rows, sel_rows.shape[2]), lambda bi, j, pt: (bi, 0, 0)))
        args.append(sel_rows)
    for i in range(pg):
        in_specs.append(pl.BlockSpec(
            (1, page, kv_w), lambda bi, j, pt, i=i: (pt[bi * n_pages + j * pg + i], 0, lane_blk)))
        args.append(pool)
    in_specs.append(pl.BlockSpec((1, page, kv_w), lambda bi, j, pt: (bi, 0, 0)))
    args.append(new_tile)
    kern = functools.partial(_paged_mqa_kernel, mode=mode, pg=pg, t_new=t_new, dk=dk, dv=dv, k_off=k_off, v_off=v_off,
                             scale=scale, kbase=kbase, q_base=q_base, page=page, use_slopes=use_slopes)
    return pl.pallas_call(
        kern,
        grid_spec=pltpu.PrefetchScalarGridSpec(
            num_scalar_prefetch=1,
            grid=(b, n_steps),
            in_specs=in_specs,
            out_specs=pl.BlockSpec((1, rows, dv), lambda bi, j, pt: (bi, 0, 0)),
            scratch_shapes=[
                pltpu.VMEM((rows, LANE), F32),
                pltpu.VMEM((rows, LANE), F32),
                pltpu.VMEM((rows, dv), F32),
            ],
        ),
        out_shape=jax.ShapeDtypeStruct((b, rows, dv), F32),
        compiler_params=pltpu.CompilerParams(
            dimension_semantics=("parallel", "arbitrary"), vmem_limit_bytes=VMEM_LIMIT),
        name="paged_mqa_" + mode,
    )(page_table.reshape(-1), *args)


def _sb_decode_kernel(pt_ref, *refs, pg, t_new, kw, scale, page):
    refs = list(refs)
    q_ref = refs.pop(0)
    page_refs = [refs.pop(0) for _ in range(pg)]
    new_ref, o_ref, run_ref, acc_ref = refs
    j = pl.program_id(1)
    n_steps = pl.num_programs(1)
    rows = q_ref.shape[1]
    qb = (q_ref[0] * scale).astype(BF16)
    later = (lax.broadcasted_iota(jnp.int32, (page, page), 0) >
             lax.broadcasted_iota(jnp.int32, (page, page), 1)).astype(BF16)

    def tile(kv, is_new, carry):
        run, acc = carry
        kt = kv[:, :kw].astype(BF16)
        vt = kv[:, kw:2 * kw].astype(BF16)
        z = lax.dot_general(qb, kt, (((1,), (1,)), ((), ())), preferred_element_type=F32)
        ls = jnp.minimum(z, 0.0) - jnp.log(1.0 + jnp.exp(-jnp.abs(z)))
        l1mb = ls - z
        if is_new:
            mask = (lax.broadcasted_iota(jnp.int32, (rows, page), 1) <
                    lax.broadcasted_iota(jnp.int32, (rows, page), 0) % t_new)
            l1mb = jnp.where(mask, l1mb, 0.0)
        hi = l1mb.astype(BF16)
        lo = (l1mb - hi.astype(F32)).astype(BF16)
        aft = jnp.dot(hi, later, preferred_element_type=F32) + jnp.dot(lo, later, preferred_element_type=F32)
        att = jnp.exp(ls + aft + run)
        if is_new:
            att = jnp.where(mask, att, 0.0)
        acc = acc + jnp.dot(att.astype(BF16), vt, preferred_element_type=F32)
        return run + aft[:, :1] + l1mb[:, :1], acc

    @pl.when(j == 0)
    def _():
        run0, acc0 = tile(new_ref[0], True, (jnp.zeros((rows, 1), F32), jnp.zeros((rows, kw), F32)))
        run_ref[...] = jnp.broadcast_to(run0, run_ref.shape)
        acc_ref[...] = acc0

    carry = (run_ref[:, :1], acc_ref[...])
    for i in range(pg):
        carry = tile(page_refs[i][0], False, carry)
    run_ref[...] = jnp.broadcast_to(carry[0], run_ref.shape)
    acc_ref[...] = carry[1]

    @pl.when(j == n_steps - 1)
    def _():
        o_ref[0] = acc_ref[...]


def sb_decode(q_rows, pool, page_table, new_tile, t_new):
    b, rows, kw = q_rows.shape
    n_pages = page_table.shape[1]
    page = pool.shape[1]
    pg = min(PAGES_PER_STEP, n_pages)
    assert n_pages % pg == 0
    n_steps = n_pages // pg
    in_specs = [pl.BlockSpec((1, rows, kw), lambda bi, j, pt: (bi, 0, 0))]
    args = [q_rows]
    for i in range(pg):
        in_specs.append(pl.BlockSpec(
            (1, page, 2 * kw), lambda bi, j, pt, i=i: (pt[bi * n_pages + n_pages - 1 - (j * pg + i)], 0, 0)))
        args.append(pool)
    in_specs.append(pl.BlockSpec((1, page, 2 * kw), lambda bi, j, pt: (bi, 0, 0)))
    args.append(new_tile)
    kern = functools.partial(_sb_decode_kernel, pg=pg, t_new=t_new, kw=kw, scale=SB_HD ** -0.5, page=page)
    return pl.pallas_call(
        kern,
        grid_spec=pltpu.PrefetchScalarGridSpec(
            num_scalar_prefetch=1,
            grid=(b, n_steps),
            in_specs=in_specs,
            out_specs=pl.BlockSpec((1, rows, kw), lambda bi, j, pt: (bi, 0, 0)),
            scratch_shapes=[pltpu.VMEM((rows, LANE), F32), pltpu.VMEM((rows, kw), F32)],
        ),
        out_shape=jax.ShapeDtypeStruct((b, rows, kw), F32),
        compiler_params=pltpu.CompilerParams(
            dimension_semantics=("parallel", "arbitrary"), vmem_limit_bytes=VMEM_LIMIT),
        name="sb_decode",
    )(page_table.reshape(-1), *args)


def _nsa_cmp_decode_kernel(q_ref, slope_ref, kc_ref, vc_ref, oc_ref, sel_ref, *, t_new, heads, scale, q_base, n_sel, k_top):
    rows = q_ref.shape[1]
    nc = kc_ref.shape[1]
    lsel = sel_ref.shape[2]
    pos = q_base + lax.broadcasted_iota(jnp.int32, (rows, nc), 0) % t_new
    cmp_end = CMP_STRIDE * lax.broadcasted_iota(jnp.int32, (rows, nc), 1) + (CMP_LEN - 1)
    valid = cmp_end <= pos
    dist = (pos - cmp_end).astype(F32)
    qb = (q_ref[0] * scale).astype(BF16)
    kc = kc_ref[0].astype(BF16)
    vc = vc_ref[0].astype(BF16)
    s = lax.dot_general(qb, kc, (((1,), (1,)), ((), ())), preferred_element_type=F32) - slope_ref[:, :1] * dist
    s = jnp.where(valid, s, NEG)
    e = jnp.exp(s - jnp.max(s, axis=-1, keepdims=True))
    p = jnp.where(valid, e / jnp.sum(e, axis=-1, keepdims=True), 0.0)
    oc_ref[0] = jnp.dot(p.astype(BF16), vc, preferred_element_type=F32)
    psum = p[0:t_new]
    for h in range(1, heads):
        psum = psum + p[h * t_new:(h + 1) * t_new]
    sel_ref[0] = _select_blocks(psum, q_base, 0, t_new, nc, lsel, n_sel, k_top)


def _select_blocks(psum, pos0, row0, tq, nc, lsel, n_sel, k_top):
    c_start = CMP_STRIDE * lax.broadcasted_iota(jnp.int32, (nc, lsel), 0)
    s_start = SEL_BLOCK * lax.broadcasted_iota(jnp.int32, (nc, lsel), 1)
    overlap = ((c_start < s_start + SEL_BLOCK) & (c_start + CMP_LEN > s_start)).astype(F32)
    imp = jnp.dot(psum, overlap, preferred_element_type=F32, precision=lax.Precision.HIGHEST)
    tpos = pos0 + row0 + lax.broadcasted_iota(jnp.int32, (tq, lsel), 0)
    j = lax.broadcasted_iota(jnp.int32, (tq, lsel), 1)
    cur = tpos // SEL_BLOCK
    forced = (j == 0) | (j == cur) | (j == cur - 1)
    cand = (SEL_BLOCK * j <= tpos) & (j < n_sel)
    score = jnp.where(cand, imp + jnp.where(forced, FORCE_BONUS, 0.0), NEG)
    work = score
    chosen = jnp.zeros((tq, lsel), F32)
    for _ in range(k_top):
        mx = jnp.max(work, axis=-1, keepdims=True)
        first = jnp.min(jnp.where(work == mx, j, lsel), axis=-1, keepdims=True)
        hit = j == first
        chosen = jnp.where(hit, 1.0, chosen)
        work = jnp.where(hit, -3.0e38, work)
    return jnp.where(score > 0.5 * NEG, chosen, 0.0)


def nsa_cmp_decode(q_rows, slopes_rows, k_cmp, v_cmp, t_new, q_base, n_sel):
    b, rows, hd = q_rows.shape
    nc = k_cmp.shape[1]
    lsel = -(-n_sel // LANE) * LANE
    kern = functools.partial(_nsa_cmp_decode_kernel, t_new=t_new, heads=rows // t_new, scale=hd ** -0.5,
                             q_base=q_base, n_sel=n_sel, k_top=min(N_SEL, n_sel))
    return pl.pallas_call(
        kern,
        grid=(b,),
        in_specs=[
            pl.BlockSpec((1, rows, hd), lambda bi: (bi, 0, 0)),
            pl.BlockSpec((rows, 1), lambda bi: (0, 0)),
            pl.BlockSpec((1, nc, hd), lambda bi: (bi, 0, 0)),
            pl.BlockSpec((1, nc, hd), lambda bi: (bi, 0, 0)),
        ],
        out_specs=[
            pl.BlockSpec((1, rows, hd), lambda bi: (bi, 0, 0)),
            pl.BlockSpec((1, t_new, lsel), lambda bi: (bi, 0, 0)),
        ],
        out_shape=[jax.ShapeDtypeStruct((b, rows, hd), F32), jax.ShapeDtypeStruct((b, t_new, lsel), F32)],
        compiler_params=pltpu.CompilerParams(dimension_semantics=("parallel",), vmem_limit_bytes=VMEM_LIMIT),
        name="nsa_cmp_decode",
    )(q_rows, slopes_rows, k_cmp, v_cmp)


def _router_kernel(x_ref, g_ref, w_ref, b_ref, lg_ref, h_ref):
    x = x_ref[...]
    ms = jnp.mean(x * x, axis=-1, keepdims=True)
    h = (x * lax.rsqrt(ms + EPS)) * g_ref[...]
    h_ref[...] = h.astype(BF16)
    lg_ref[...] = jnp.dot(h, w_ref[...], preferred_element_type=F32,
                          precision=lax.Precision.HIGHEST) + b_ref[...]


def _experts_kernel(te_ref, tv_ref, xs_ref, sw_ref, w1_ref, w3_ref, w2_ref, o_ref):
    i = pl.program_id(0)

    @pl.when(tv_ref[i] != 0)
    def _():
        x = xs_ref[...]
        a = jnp.dot(x, w1_ref[0], preferred_element_type=F32)
        c = jnp.dot(x, w3_ref[0], preferred_element_type=F32)
        hid = (jax.nn.silu(a) * c) * sw_ref[...]
        o_ref[...] = jnp.dot(hid.astype(BF16), w2_ref[0], preferred_element_type=F32)

    @pl.when(tv_ref[i] == 0)
    def _():
        o_ref[...] = jnp.zeros_like(o_ref)


def moe(x, g, w_rg, b_rg, w_re, b_re, w_e1, w_e3, w_e2):
    n, d = x.shape
    ne = N_GROUPS * N_EXP
    tm = _pick_tile(n, (ROW_TILE, 256, 128))
    wr = jnp.zeros((d, LANE), F32).at[:, :N_GROUPS].set(w_rg).at[:, N_GROUPS:N_GROUPS + ne].set(w_re)
    br = jnp.zeros((1, LANE), F32).at[0, :N_GROUPS].set(b_rg).at[0, N_GROUPS:N_GROUPS + ne].set(b_re)
    logits, h = pl.pallas_call(
        _router_kernel,
        grid=(n // tm,),
        in_specs=[
            pl.BlockSpec((tm, d), lambda i: (i, 0)),
            pl.BlockSpec((1, d), lambda i: (0, 0)),
            pl.BlockSpec((d, LANE), lambda i: (0, 0)),
            pl.BlockSpec((1, LANE), lambda i: (0, 0)),
        ],
        out_specs=[pl.BlockSpec((tm, LANE), lambda i: (i, 0)), pl.BlockSpec((tm, d), lambda i: (i, 0))],
        out_shape=[jax.ShapeDtypeStruct((n, LANE), F32), jax.ShapeDtypeStruct((n, d), BF16)],
        compiler_params=pltpu.CompilerParams(dimension_semantics=("parallel",), vmem_limit_bytes=VMEM_LIMIT),
        name="moe_router",
    )(x, g.astype(F32).reshape(1, d), wr, br)

    lg = logits[:, :N_GROUPS]
    le = logits[:, N_GROUPS:N_GROUPS + ne].reshape(n, N_GROUPS, N_EXP)
    pg = jax.nn.softmax(lg, axis=-1)
    gi = jnp.argmax(lg, axis=-1)
    gw = jnp.take_along_axis(pg, gi[:, None], axis=1)[:, 0]
    le_g = jnp.take_along_axis(le, gi[:, None, None], axis=1)[:, 0]
    top_v, top_i = lax.top_k(le_g, TOP_K)
    we = jax.nn.softmax(top_v, axis=-1)
    eid = (gi[:, None] * N_EXP + top_i).astype(jnp.int32)
    wt = we * gw[:, None]

    ts = EXPERT_TILE
    n_tiles = -(-(TOP_K * n) // ts) + ne
    member = (eid[:, :, None] == jnp.arange(ne)[None, None, :]).any(axis=1).astype(jnp.int32)
    cnt = member.sum(axis=0)
    rank = jnp.cumsum(member, axis=0) - member
    tiles_e = (cnt + ts - 1) // ts
    tile_end = jnp.cumsum(tiles_e)
    pad_off = (tile_end - tiles_e) * ts
    pos = pad_off[eid] + jnp.take_along_axis(rank, eid, axis=1)
    tok = jnp.broadcast_to(jnp.arange(n, dtype=jnp.int32)[:, None], (n, TOP_K))
    slot_tok = jnp.zeros((n_tiles * ts,), jnp.int32).at[pos.reshape(-1)].set(tok.reshape(-1))
    slot_w = jnp.zeros((n_tiles * ts,), F32).at[pos.reshape(-1)].set(wt.reshape(-1))
    tile_id = jnp.arange(n_tiles, dtype=jnp.int32)
    tile_e = jnp.minimum(jnp.searchsorted(tile_end, tile_id, side="right"), ne - 1).astype(jnp.int32)
    tile_v = (tile_id < tile_end[-1]).astype(jnp.int32)

    xs = jnp.take(h, slot_tok, axis=0)
    w1 = w_e1.reshape(ne, d, D_EXP).astype(BF16)
    w3 = w_e3.reshape(ne, d, D_EXP).astype(BF16)
    w2 = w_e2.reshape(ne, D_EXP, d).astype(BF16)
    ys = pl.pallas_call(
        _experts_kernel,
        grid_spec=pltpu.PrefetchScalarGridSpec(
            num_scalar_prefetch=2,
            grid=(n_tiles,),
            in_specs=[
                pl.BlockSpec((ts, d), lambda i, te, tv: (i, 0)),
                pl.BlockSpec((ts, 1), lambda i, te, tv: (i, 0)),
                pl.BlockSpec((1, d, D_EXP), lambda i, te, tv: (te[i], 0, 0)),
                pl.BlockSpec((1, d, D_EXP), lambda i, te, tv: (te[i], 0, 0)),
                pl.BlockSpec((1, D_EXP, d), lambda i, te, tv: (te[i], 0, 0)),
            ],
            out_specs=pl.BlockSpec((ts, d), lambda i, te, tv: (i, 0)),
        ),
        out_shape=jax.ShapeDtypeStruct((n_tiles * ts, d), F32),
        compiler_params=pltpu.CompilerParams(dimension_semantics=("arbitrary",), vmem_limit_bytes=VMEM_LIMIT),
        name="moe_experts",
    )(tile_e, tile_v, xs, slot_w.reshape(-1, 1), w1, w3, w2)
    return jnp.take(ys, pos[:, 0], axis=0) + jnp.take(ys, pos[:, 1], axis=0)


def rmsnorm(x, g):
    xf = x.astype(F32)
    y = xf * lax.rsqrt(jnp.mean(xf * xf, axis=-1, keepdims=True) + EPS)
    return (y * g.astype(F32)).astype(x.dtype)


def split_cols(a, widths):
    return jnp.split(a, np.cumsum(widths)[:-1].tolist(), axis=-1)


def qblock(t):
    return min(QBLOCK, t)


def map_query_blocks(fn, arrays, block):
    t = arrays[0].shape[1]
    nb = -(-t // block)
    pad = nb * block - t

    def to_blocks(a):
        a = jnp.pad(a, [(0, 0), (0, pad)] + [(0, 0)] * (a.ndim - 2))
        a = a.reshape((a.shape[0], nb, block) + a.shape[2:])
        return jnp.moveaxis(a, 1, 0)

    xs = (jnp.arange(nb),) + tuple(to_blocks(a) for a in arrays)
    out = lax.map(lambda args: fn(*args), xs)
    out = jnp.moveaxis(out, 0, 1)
    out = out.reshape((out.shape[0], nb * block) + out.shape[3:])
    return out[:, :t]


def rope_angles(pos, dim):
    inv = ROPE_THETA ** (-jnp.arange(0, dim, 2, dtype=F32) / dim)
    ang = pos.astype(F32)[:, None] * inv[None, :]
    return jnp.cos(ang), jnp.sin(ang)


def apply_rope(x, cos, sin):
    half = x.shape[-1] // 2
    x1 = x[..., :half].astype(F32)
    x2 = x[..., half:].astype(F32)
    return jnp.concatenate([x1 * cos - x2 * sin, x1 * sin + x2 * cos], axis=-1).astype(x.dtype)


def alibi_slopes(n):
    return 2.0 ** (-8.0 * jnp.arange(1, n + 1, dtype=F32) / n)


def gather_pages(pool, page_table):
    g = pool[page_table]
    return g.reshape((page_table.shape[0], page_table.shape[1] * pool.shape[1]) + pool.shape[2:])


def mla_mixer(c_q, c_kv_raw, k_r_raw, pos, past, g_cq, w_uq, g_ckv, w_uk, w_uv):
    b, t, _ = c_q.shape
    cos, sin = rope_angles(pos, ROPE_DIM)
    q = jnp.einsum('btr,rhd->bthd', rmsnorm(c_q, g_cq), w_uq)
    q_nope = q[..., :NOPE_DIM]
    q_rope = apply_rope(q[..., NOPE_DIM:], cos[None, :, None, :], sin[None, :, None, :])
    q_lat = jnp.einsum('bthd,rhd->bthr', q_nope, w_uk)
    c_kv = rmsnorm(c_kv_raw, g_ckv)
    k_r = apply_rope(k_r_raw, cos[None], sin[None])
    new_rows = jnp.concatenate([c_kv, k_r], axis=-1)
    keys = jnp.concatenate([past, new_rows], axis=1)
    k_pos = jnp.arange(keys.shape[1])
    lat, kr = keys[..., :KV_RANK], keys[..., KV_RANK:]
    scale = (NOPE_DIM + ROPE_DIM) ** -0.5

    def blk(bi, ql, qr, qp):
        s = (jnp.einsum('bqhr,bsr->bhqs', ql, lat) + jnp.einsum('bqhd,bsd->bhqs', qr, kr)).astype(F32) * scale
        mask = k_pos[None, :] <= qp[0][:, None]
        p = jax.nn.softmax(jnp.where(mask, s, NEG), axis=-1).astype(lat.dtype)
        return jnp.einsum('bhqs,bsr->bqhr', p, lat)

    o_lat = map_query_blocks(blk, (q_lat, q_rope, pos[None]), qblock(t))
    out = jnp.einsum('bthr,rhd->bthd', o_lat, w_uv).reshape(b, t, MLA_HEADS * MLA_VDIM)
    return out, new_rows


def mlstm_mixer(q, k, v, ig, fg, og, c0, n0, m0, g_mh):
    b, t = q.shape[:2]
    dt = q.dtype
    q = q.reshape(b, t, ML_HEADS, ML_QK)
    k = k.reshape(b, t, ML_HEADS, ML_QK) * (ML_QK ** -0.5)
    v = v.reshape(b, t, ML_HEADS, ML_V)
    li = ig.astype(F32)
    lf = jax.nn.log_sigmoid(fg.astype(F32))
    L = min(ML_CHUNK, t)
    nc = -(-t // L)
    pad = nc * L - t

    def chunks(a, val=0.0):
        a = jnp.pad(a, [(0, 0), (0, pad)] + [(0, 0)] * (a.ndim - 2), constant_values=val)
        return jnp.moveaxis(a.reshape((b, nc, L) + a.shape[2:]), 1, 0)

    tri = jnp.tril(jnp.ones((L, L), dtype=bool))

    def step(carry, xs):
        c, n, m = carry
        qc, kc, vc, lic, lfc = xs
        cb = jnp.cumsum(lfc, axis=1)
        a = cb + m[:, None, :]
        d = cb[:, :, None, :] - cb[:, None, :, :] + lic[:, None, :, :]
        d = jnp.where(tri[None, :, :, None], d, NEG)
        mt = jnp.maximum(a, d.max(axis=2))
        w = jnp.exp(d - mt[:, :, None, :])
        inter = jnp.exp(a - mt)
        sc = w * jnp.einsum('bthd,bshd->btsh', qc, kc).astype(F32)
        num = jnp.einsum('btsh,bshv->bthv', sc, vc) + inter[..., None] * jnp.einsum('bhvd,bthd->bthv', c, qc)
        den = sc.sum(axis=2) + inter * jnp.einsum('bhd,bthd->bth', n, qc)
        h = num / jnp.maximum(jnp.abs(den), jnp.exp(-mt))[..., None]
        wl, il = w[:, -1], inter[:, -1]
        c = il[..., None, None] * c + jnp.einsum('bsh,bshv,bshd->bhvd', wl, vc, kc)
        n = il[..., None] * n + jnp.einsum('bsh,bshd->bhd', wl, kc)
        return (c, n, mt[:, -1]), h

    carry0 = (c0.astype(F32), n0.astype(F32), m0.astype(F32))
    (c, n, m), h = lax.scan(step, carry0, (chunks(q), chunks(k), chunks(v), chunks(li, NEG), chunks(lf)))
    h = jnp.moveaxis(h, 0, 1).reshape(b, nc * L, ML_HEADS, ML_V)[:, :t]
    h = h - h.mean(axis=-1, keepdims=True)
    h = h * lax.rsqrt(jnp.mean(h * h, axis=-1, keepdims=True) + EPS) * g_mh.astype(F32)
    out = jax.nn.sigmoid(og.astype(F32)).reshape(b, t, ML_HEADS, ML_V) * h
    return out.reshape(b, t, ML_HEADS * ML_V).astype(dt), c, n, m


def sb_mixer(q, k_new, v_new, pos, past_kv):
    b, t = q.shape[:2]
    q = q.reshape(b, t, SB_KV_HEADS, SB_HEADS // SB_KV_HEADS, SB_HD)
    new_rows = jnp.stack([k_new.reshape(b, t, SB_KV_HEADS, SB_HD), v_new.reshape(b, t, SB_KV_HEADS, SB_HD)], axis=2)
    kv = jnp.concatenate([past_kv, new_rows], axis=1)
    k, v = kv[:, :, 0], kv[:, :, 1]
    k_pos = jnp.arange(kv.shape[1])

    def blk(bi, qh, qp):
        z = jnp.einsum('bqgrd,bsgd->bgrqs', qh, k).astype(F32) * (SB_HD ** -0.5)
        mask = k_pos[None, :] < qp[0][:, None]
        l1mb = jnp.where(mask, jax.nn.log_sigmoid(-z), 0.0)
        after = lax.cumsum(l1mb, axis=z.ndim - 1, reverse=True) - l1mb
        att = jnp.where(mask, jnp.exp(jax.nn.log_sigmoid(z) + after), 0.0).astype(v.dtype)
        return jnp.einsum('bgrqs,bsgd->bqgrd', att, v)

    o = map_query_blocks(blk, (q, pos[None]), qblock(t))
    return o.reshape(b, t, SB_HEADS * SB_HD), new_rows


def nsa_mixer(q, kv_new, win_new, gate, pos, past_kv, win_prior, pe, wc1, wc2):
    b, t = q.shape[:2]
    past = past_kv.shape[1]
    s_len = past + t
    scale = NSA_HD ** -0.5
    slopes = alibi_slopes(NSA_HEADS)
    kv_all = jnp.concatenate([past_kv, kv_new], axis=1)
    n_chunk = max(-(-s_len // CMP_STRIDE), 2)
    ck = jnp.pad(kv_all[:, :, :2], ((0, 0), (0, n_chunk * CMP_STRIDE - s_len), (0, 0), (0, 0)))
    ck = ck.reshape(b, n_chunk, CMP_STRIDE, 2, NSA_HD)
    w1 = wc1.reshape(2, 2, CMP_STRIDE, NSA_HD, CMP_HID)
    proj = jnp.einsum('bcrkd,kzrdh->bczkh', ck, w1)
    pe_term = jnp.einsum('kpd,kpdh->kh', pe, wc1.reshape(2, CMP_LEN, NSA_HD, CMP_HID))
    hid = jax.nn.gelu(proj[:, :-1, 0] + proj[:, 1:, 1] + pe_term)
    cmp = jnp.einsum('bnkh,khd->bnkd', hid, wc2)
    k_cmp, v_cmp = cmp[:, :, 0], cmp[:, :, 1]
    cmp_start = CMP_STRIDE * jnp.arange(n_chunk - 1)
    cmp_end = cmp_start + CMP_LEN - 1
    n_sel = -(-s_len // SEL_BLOCK)
    sel = jnp.pad(kv_all[:, :, 2:], ((0, 0), (0, n_sel * SEL_BLOCK - s_len), (0, 0), (0, 0)))
    sel = sel.reshape(b, n_sel, SEL_BLOCK, 2, NSA_HD)
    sel_start = SEL_BLOCK * jnp.arange(n_sel)
    overlap = ((cmp_start[:, None] < sel_start[None, :] + SEL_BLOCK) & (cmp_start[:, None] + CMP_LEN > sel_start[None, :])).astype(F32)
    k_top = min(N_SEL, n_sel)
    bidx = jnp.arange(b)[:, None, None]
    pw = win_prior.shape[1]
    qb = qblock(t)
    nb = -(-t // qb)
    band = jnp.concatenate([jnp.zeros((b, WINDOW - pw, 2, NSA_HD), win_new.dtype), win_prior, win_new, jnp.zeros((b, nb * qb - t, 2, NSA_HD), win_new.dtype)], axis=1)
    band_pos = past - WINDOW + jnp.arange(WINDOW + nb * qb)

    def blk(bi, qh, gh, qp):
        tq = qp[0]
        valid_c = cmp_end[None, :] <= tq[:, None]
        dist_c = (tq[:, None] - cmp_end[None, :]).astype(F32)
        s_c = jnp.einsum('bqhd,bnd->bhqn', qh, k_cmp).astype(F32) * scale - slopes[:, None, None] * dist_c
        p_c = jnp.where(valid_c, jax.nn.softmax(jnp.where(valid_c, s_c, NEG), axis=-1), 0.0)
        o_c = jnp.einsum('bhqn,bnd->bqhd', p_c.astype(v_cmp.dtype), v_cmp)
        imp = jnp.einsum('bhqn,nj->bqj', p_c, overlap)
        cur = tq // SEL_BLOCK
        j = jnp.arange(n_sel)
        forced = (j[None, :] == 0) | (j[None, :] == cur[:, None]) | (j[None, :] == cur[:, None] - 1)
        cand = sel_start[None, :] <= tq[:, None]
        score = jnp.where(cand[None], imp + jnp.where(forced, FORCE_BONUS, 0.0)[None], NEG)
        top_s, idx = lax.top_k(score, k_top)
        kv_sel = sel[bidx, idx]
        kpos = idx[..., None] * SEL_BLOCK + jnp.arange(SEL_BLOCK)
        ok = (top_s > 0.5 * NEG)[..., None] & (kpos <= tq[None, :, None, None])
        dist_s = (tq[None, :, None, None] - kpos).astype(F32)
        s_s = jnp.einsum('bqhd,bqkld->bqhkl', qh, kv_sel[..., 0, :]).astype(F32) * scale - slopes[None, None, :, None, None] * dist_s[:, :, None]
        s_s = jnp.where(ok[:, :, None], s_s, NEG)
        p_s = jax.nn.softmax(s_s.reshape(s_s.shape[:3] + (-1,)), axis=-1).reshape(s_s.shape)
        o_s = jnp.einsum('bqhkl,bqkld->bqhd', p_s.astype(kv_sel.dtype), kv_sel[..., 1, :])
        bw = lax.dynamic_slice_in_dim(band, bi * qb, qb + WINDOW, axis=1)
        bp = lax.dynamic_slice_in_dim(band_pos, bi * qb, qb + WINDOW, axis=0)
        dist_w = tq[:, None] - bp[None, :]
        ok_w = (bp[None, :] >= 0) & (dist_w >= 0) & (dist_w < WINDOW)
        s_w = jnp.einsum('bqhd,bsd->bhqs', qh, bw[:, :, 0]).astype(F32) * scale - slopes[:, None, None] * dist_w.astype(F32)
        p_w = jax.nn.softmax(jnp.where(ok_w, s_w, NEG), axis=-1)
        o_w = jnp.einsum('bhqs,bsd->bqhd', p_w.astype(bw.dtype), bw[:, :, 1])
        g = jax.nn.sigmoid(gh.astype(F32))
        return (g[..., 0:1] * o_c + g[..., 1:2] * o_s + g[..., 2:3] * o_w).astype(qh.dtype)

    o = map_query_blocks(blk, (q, gate, pos[None]), qb)
    new_win = jnp.concatenate([win_prior, win_new], axis=1)[:, -min(WINDOW, pw + t):]
    return o.reshape(b, t, NSA_HEADS * NSA_HD), kv_new, new_win


def mem_kv_rows(mem, g, wk, wv):
    b, nm, d = mem.shape
    kv = rms_matmul(mem.reshape(b * nm, d), jnp.concatenate([wk, wv], axis=1), g=g)
    hd = X_HEADS * X_HD
    k = kv[:, :hd].reshape(b, nm, X_HEADS, X_HD)
    v = kv[:, hd:].reshape(b, nm, X_HEADS, X_HD)
    return jnp.stack([k, v], axis=2)


def mem_attend(x, g, mkv, wq, wo):
    b, t, d = x.shape
    q = rms_matmul(x.reshape(b * t, d), wq, g=g).reshape(b, t, X_HEADS, X_HD)
    s = jnp.einsum('bthd,bmhd->bhtm', q, mkv[:, :, 0]).astype(F32) * (X_HD ** -0.5)
    p = jax.nn.softmax(s, axis=-1).astype(mkv.dtype)
    o = jnp.einsum('bhtm,bmhd->bthd', p, mkv[:, :, 1]).reshape(b * t, X_HEADS * X_HD)
    return rms_matmul(o, wo, res=x.reshape(b * t, d)).reshape(b, t, d)


def odd_mixers_prompt(proj, b, t, pe, wc1, wc2):
    n = b * t
    sb_w = SB_HEADS * SB_HD
    kv_w = SB_KV_HEADS * SB_HD
    q_col = sb_w + 2 * kv_w
    kv_col = q_col + NSA_HEADS * NSA_HD
    win_col = kv_col + 4 * NSA_HD
    gate_col = win_col + 2 * NSA_HD
    o_sb = sb_attn_prompt(proj, b, t, 0, sb_w, sb_w + kv_w)
    sb_rows = proj[:, sb_w:sb_w + 2 * kv_w].reshape(b, t, 2, SB_KV_HEADS, SB_HD)
    nsa_rows = proj[:, kv_col:win_col].reshape(b, t, 4, NSA_HD)
    win_rows = proj[:, win_col:gate_col].reshape(b, t, 2, NSA_HD)
    ck = proj[:, kv_col:kv_col + 2 * NSA_HD].reshape(b, t // CMP_STRIDE, CMP_STRIDE, 2, NSA_HD)
    k_cmp, v_cmp = nsa_compress(ck, pe, wc1, wc2)
    o_c, sel = nsa_cmp_select(proj, q_col, k_cmp, v_cmp, b, t, 0, t // SEL_BLOCK)
    slopes = _alibi_slopes_np(NSA_HEADS)
    common = dict(heads=NSA_HEADS, dk=NSA_HD, dv=NSA_HD, k_off=0, v_off=NSA_HD, scale=NSA_HD ** -0.5, slopes=slopes)
    o_s = mqa_flash(proj, q_col, proj, kv_col + 2 * NSA_HD, 2 * NSA_HD, b, t, mode="select", sel=sel, **common)
    o_w = mqa_flash(proj, q_col, proj, win_col, 2 * NSA_HD, b, t, mode="window", **common)
    g = jax.nn.sigmoid(proj[:, gate_col:gate_col + 3 * NSA_HEADS]).reshape(n, NSA_HEADS, 3)
    sh = (n, NSA_HEADS, NSA_HD)
    o_nsa = (g[..., 0:1] * o_c.reshape(sh) + g[..., 1:2] * o_s.reshape(sh) + g[..., 2:3] * o_w.reshape(sh)).reshape(n, -1)
    return o_sb, sb_rows, o_nsa, nsa_rows, win_rows[:, -min(WINDOW, t):]


def _pad_rows(a, rows):
    return jnp.pad(a, ((0, 0), (0, rows - a.shape[1]), (0, 0)))


def _rows_head_major(a, b, t, heads):
    w = a.shape[1] // heads
    return jnp.transpose(a.reshape(b, t, heads, w), (0, 2, 1, 3)).reshape(b, heads * t, w)


def _rows_token_major(a, b, t, heads):
    w = a.shape[2]
    return jnp.transpose(a.reshape(b, heads, t, w), (0, 2, 1, 3)).reshape(b * t, heads * w)


def odd_mixers_sample(proj, b, t, past, page_table, pool_sb, pool_nsa, win_prior, pe, wc1, wc2):
    n = b * t
    page = pool_sb.shape[1]
    sb_w = SB_HEADS * SB_HD
    kv_w = SB_KV_HEADS * SB_HD
    rep = SB_HEADS // SB_KV_HEADS
    q_col = sb_w + 2 * kv_w
    kv_col = q_col + NSA_HEADS * NSA_HD
    win_col = kv_col + 4 * NSA_HD
    gate_col = win_col + 2 * NSA_HD
    assert past % page == 0 and past % SEL_BLOCK == 0 and t <= CMP_STRIDE and t <= page

    q = jnp.transpose(proj[:, :sb_w].reshape(b, t, SB_KV_HEADS, rep, SB_HD), (0, 2, 3, 1, 4))
    eye = jnp.eye(SB_KV_HEADS, dtype=F32)
    q_rows = (q[:, :, :, :, None, :] * eye[None, :, None, None, :, None]).reshape(b, SB_HEADS * t, kv_w)
    new_sb = _pad_rows(proj[:, sb_w:sb_w + 2 * kv_w].reshape(b, t, 2 * kv_w), page)
    o = sb_decode(q_rows, pool_sb.reshape(pool_sb.shape[0], page, 2 * kv_w), page_table, new_sb, t)
    o = o.reshape(b, SB_KV_HEADS, rep, t, SB_KV_HEADS, SB_HD)
    o = jnp.stack([o[:, g, :, :, g] for g in range(SB_KV_HEADS)], axis=1)
    o_sb = jnp.transpose(o, (0, 3, 1, 2, 4)).reshape(n, sb_w)
    sb_rows = proj[:, sb_w:sb_w + 2 * kv_w].reshape(b, t, 2, SB_KV_HEADS, SB_HD)

    nsa_rows = proj[:, kv_col:win_col].reshape(b, t, 4, NSA_HD)
    win_new = proj[:, win_col:gate_col].reshape(b, t, 2, NSA_HD)
    pool3 = pool_nsa.reshape(pool_nsa.shape[0], page, 4 * NSA_HD)
    n_pages = page_table.shape[1]
    past_ck = jnp.take(pool3, page_table.reshape(-1), axis=0)[:, :, :2 * NSA_HD]
    ck = past_ck.reshape(b, n_pages * page // CMP_STRIDE, CMP_STRIDE, 2, NSA_HD)
    k_cmp, v_cmp = nsa_compress(ck, pe, wc1, wc2)
    slopes_rows = jnp.repeat(jnp.asarray(_alibi_slopes_np(NSA_HEADS), F32), t).reshape(NSA_HEADS * t, 1)
    q_nsa = _rows_head_major(proj[:, q_col:kv_col], b, t, NSA_HEADS)
    n_sel = -(-(past + t) // SEL_BLOCK)
    o_c, sel = nsa_cmp_decode(q_nsa, slopes_rows, k_cmp, v_cmp, t, past, n_sel)
    common = dict(t_new=t, dk=NSA_HD, dv=NSA_HD, k_off=0, v_off=NSA_HD, scale=NSA_HD ** -0.5, q_base=past)
    new_sel = _pad_rows(proj[:, kv_col + 2 * NSA_HD:win_col].reshape(b, t, 2 * NSA_HD), page)
    o_s = paged_mqa(q_nsa, slopes_rows, pool3, 1, 2 * NSA_HD, page_table, new_sel, mode="select", kbase=0,
                    sel_rows=jnp.tile(sel, (1, NSA_HEADS, 1)), **common)
    pw = win_prior.shape[1]
    assert pw % page == 0 and pw <= past
    win_pool = win_prior.reshape(b * (pw // page), page, 2 * NSA_HD)
    win_pt = jnp.arange(b * (pw // page), dtype=jnp.int32).reshape(b, pw // page)
    new_win = _pad_rows(proj[:, win_col:gate_col].reshape(b, t, 2 * NSA_HD), page)
    o_w = paged_mqa(q_nsa, slopes_rows, win_pool, 0, 2 * NSA_HD, win_pt, new_win, mode="window", kbase=past - pw,
                    **common)
    g = jax.nn.sigmoid(proj[:, gate_col:gate_col + 3 * NSA_HEADS]).reshape(n, NSA_HEADS, 3)
    sh = (n, NSA_HEADS, NSA_HD)
    o_nsa = (g[..., 0:1] * _rows_token_major(o_c, b, t, NSA_HEADS).reshape(sh)
             + g[..., 1:2] * _rows_token_major(o_s, b, t, NSA_HEADS).reshape(sh)
             + g[..., 2:3] * _rows_token_major(o_w, b, t, NSA_HEADS).reshape(sh)).reshape(n, -1)
    win = jnp.concatenate([win_prior, win_new], axis=1)[:, -min(WINDOW, pw + t):]
    return o_sb, sb_rows, o_nsa, nsa_rows, win


def mla_attend(c_q, c_kv_raw, k_r_raw, b, t, start, g_cq, w_uq, g_ckv, w_uk, w_uv, paged=None):
    n = b * t
    pos = start + jnp.arange(t, dtype=jnp.int32)
    cos, sin = rope_angles(pos, ROPE_DIM)
    cos = jnp.tile(cos, (b, 1))
    sin = jnp.tile(sin, (b, 1))
    q = rms_matmul(c_q, w_uq.reshape(Q_RANK, -1), g=g_cq).reshape(n, MLA_HEADS, NOPE_DIM + ROPE_DIM)
    q_rope = apply_rope(q[..., NOPE_DIM:], cos[:, None, :], sin[:, None, :])
    q_lat = jnp.einsum('nhd,rhd->nhr', q[..., :NOPE_DIM], w_uk)
    new_rows = jnp.concatenate([rmsnorm(c_kv_raw, g_ckv), apply_rope(k_r_raw, cos, sin)], axis=-1)
    qf = jnp.concatenate([q_lat, q_rope], axis=-1).reshape(n, -1)
    dk = KV_RANK + ROPE_DIM
    scale = (NOPE_DIM + ROPE_DIM) ** -0.5
    if paged is None:
        o_lat = mqa_flash(qf, 0, new_rows, 0, dk, b, t, mode="causal", heads=MLA_HEADS, dk=dk, dv=KV_RANK,
                          k_off=0, v_off=0, scale=scale)
    else:
        pool, page_table = paged
        page = pool.shape[1]
        o = paged_mqa(_rows_head_major(qf, b, t, MLA_HEADS), jnp.zeros((MLA_HEADS * t, 1), F32), pool, 0, dk,
                      page_table, _pad_rows(new_rows.reshape(b, t, dk), page), mode="causal", t_new=t, dk=dk,
                      dv=KV_RANK, k_off=0, v_off=0, scale=scale, kbase=0, q_base=start, use_slopes=False)
        o_lat = _rows_token_major(o, b, t, MLA_HEADS)
    out = jnp.einsum('nhr,rhd->nhd', o_lat.reshape(n, MLA_HEADS, KV_RANK), w_uv).reshape(n, MLA_HEADS * MLA_VDIM)
    return out, new_rows


def trunk(x, start, ml_state, mem_kv, p, caches=None):
    b, t, d = x.shape
    depth = p['g_mix'].shape[0]
    mla_rows, ml_c, ml_n, ml_m, sb_rows, nsa_rows, wins = [], [], [], [], [], [], []
    for l in range(depth):
        e = l // 2
        x2 = x.reshape(b * t, d)
        if l % 2 == 0:
            proj = rms_matmul(x2, p['w_in_even'][e], g=p['g_mix'][l])
            c_q, c_kv, k_r, mq, mk, mv, mi, mf, mo = split_cols(proj, EVEN_COLS)
            paged = None if caches is None else (caches['mla'][e], caches['page_table'])
            o_a, rows = mla_attend(c_q, c_kv, k_r, b, t, start, p['g_cq'][e], p['w_uq'][e], p['g_ckv'][e], p['w_uk'][e], p['w_uv'][e], paged=paged)
            rows = rows.reshape(b, t, -1)
            mq, mk, mv, mi, mf, mo = (a.reshape(b, t, -1) for a in (mq, mk, mv, mi, mf, mo))
            c0, n0, m0 = ml_state[e]
            o_b, c, n, m = mlstm_mixer(mq, mk, mv, mi + p['b_ml_i'][e], mf + p['b_ml_f'][e], mo, c0, n0, m0, p['g_mh'][e])
            mla_rows.append(rows)
            ml_c.append(c)
            ml_n.append(n)
            ml_m.append(m)
            mix = jnp.concatenate([o_a, o_b.reshape(b * t, -1)], axis=-1)
            x = rms_matmul(mix, p['w_out_even'][e], res=x2).reshape(b, t, d)
        else:
            proj = rms_matmul(x2, p['w_in_odd'][e], g=p['g_mix'][l], keep_pad=True)
            if caches is None:
                o_c, srows, o_d, nrows, win = odd_mixers_prompt(proj, b, t, p['nsa_pe'][e], p['nsa_wc1'][e], p['nsa_wc2'][e])
            else:
                o_c, srows, o_d, nrows, win = odd_mixers_sample(
                    proj, b, t, start, caches['page_table'], caches['sb'][e], caches['nsa'][e], caches['win'][e],
                    p['nsa_pe'][e], p['nsa_wc1'][e], p['nsa_wc2'][e])
            sb_rows.append(srows)
            nsa_rows.append(nrows)
            wins.append(win)
            mix = jnp.concatenate([o_c, o_d], axis=-1)
            x = rms_matmul(mix, p['w_out_odd'][e], res=x2).reshape(b, t, d)
        x = mem_attend(x, p['g_xattn'][l], mem_kv[l], p['w_xq'][l], p['w_xo'][l])
        x2 = x.reshape(b * t, d)
        x = (x2 + moe(x2, p['g_ffn'][l], p['w_rg'][l], p['b_rg'][l], p['w_re'][l], p['b_re'][l], p['w_e1'][l], p['w_e3'][l], p['w_e2'][l])).reshape(b, t, d)
    y = rmsnorm(x, p['g_final'])
    return y, mla_rows, ml_c, ml_n, ml_m, sb_rows, nsa_rows, wins


def kernel(x_prompt, x_sample, mem_prompt, cache_mla, state_mlstm_c, state_mlstm_n, state_mlstm_m, cache_sb_kv, cache_nsa_kv, state_nsa_win, cache_mem_kv, page_table, g_mix, w_in_even, b_ml_i, b_ml_f, g_cq, w_uq, g_ckv, w_uk, w_uv, g_mh, w_out_even, w_in_odd, nsa_pe, nsa_wc1, nsa_wc2, w_out_odd, g_xattn, g_memnorm, w_xq, w_xk, w_xv, w_xo, g_ffn, w_rg, b_rg, w_re, b_re, w_e1, w_e3, w_e2, g_final):
    p = dict(g_mix=g_mix, w_in_even=w_in_even, b_ml_i=b_ml_i, b_ml_f=b_ml_f, g_cq=g_cq, w_uq=w_uq, g_ckv=g_ckv, w_uk=w_uk, w_uv=w_uv, g_mh=g_mh, w_out_even=w_out_even, w_in_odd=w_in_odd, nsa_pe=nsa_pe, nsa_wc1=nsa_wc1, nsa_wc2=nsa_wc2, w_out_odd=w_out_odd, g_xattn=g_xattn, w_xq=w_xq, w_xo=w_xo, g_ffn=g_ffn, w_rg=w_rg, b_rg=b_rg, w_re=w_re, b_re=b_re, w_e1=w_e1, w_e3=w_e3, w_e2=w_e2, g_final=g_final)
    dt = x_prompt.dtype
    bp = x_prompt.shape[0]
    depth = g_mix.shape[0]
    n_even = (depth + 1) // 2
    n_odd = depth // 2
    mem_kv_list_p = [mem_kv_rows(mem_prompt, g_memnorm[l], w_xk[l], w_xv[l]) for l in range(depth)]
    y_prompt, mla_p, c_p, n_p, m_p, sb_p, nsa_p, win_pl = trunk(
        x_prompt, 0,
        [(jnp.zeros((bp, ML_HEADS, ML_V, ML_QK), F32), jnp.zeros((bp, ML_HEADS, ML_QK), F32), jnp.zeros((bp, ML_HEADS), F32)) for _ in range(n_even)],
        mem_kv_list_p, p)
    past_len = page_table.shape[1] * cache_mla.shape[2]
    caches = dict(page_table=page_table, mla=[cache_mla[e] for e in range(n_even)],
                  sb=[cache_sb_kv[e] for e in range(n_odd)], nsa=[cache_nsa_kv[e] for e in range(n_odd)],
                  win=[state_nsa_win[e] for e in range(n_odd)])
    y_sample, mla_s, c_s, n_s, m_s, sb_s, nsa_s, win_sl = trunk(
        x_sample, past_len,
        [(state_mlstm_c[e], state_mlstm_n[e], state_mlstm_m[e]) for e in range(n_even)],
        [cache_mem_kv[l] for l in range(depth)], p, caches=caches)
    return (y_prompt, y_sample, jnp.stack(mla_p), jnp.stack(mla_s), jnp.stack(c_p), jnp.stack(c_s),
            jnp.stack(n_p), jnp.stack(n_s), jnp.stack(m_p), jnp.stack(m_s), jnp.stack(sb_p), jnp.stack(sb_s),
            jnp.stack(nsa_p), jnp.stack(nsa_s), jnp.stack(win_pl), jnp.stack(win_sl), jnp.stack(mem_kv_list_p))
```

```python
import functools

import jax
import jax.numpy as jnp
import numpy as np
from jax import lax
from jax.experimental import pallas as pl
from jax.experimental.pallas import tpu as pltpu

F32 = jnp.float32
BF16 = jnp.bfloat16

D_MODEL = 2048
QBLOCK = 128
NEG = -1e30
EPS = 1e-6
MLA_HEADS = 8
Q_RANK = 512
KV_RANK = 256
NOPE_DIM = 128
ROPE_DIM = 64
MLA_VDIM = 128
ROPE_THETA = 10000.0
ML_HEADS = 4
ML_QK = 128
ML_V = 256
ML_CHUNK = 64
SB_HEADS = 8
SB_KV_HEADS = 4
SB_HD = 64
NSA_HEADS = 16
NSA_HD = 64
CMP_STRIDE = 16
CMP_LEN = 2 * CMP_STRIDE
CMP_HID = 128
SEL_BLOCK = 64
N_SEL = 16
WINDOW = 512
FORCE_BONUS = 1000.0
X_HEADS = 4
X_HD = 128
N_GROUPS = 4
N_EXP = 8
TOP_K = 2
D_EXP = 512

EVEN_COLS = (Q_RANK, KV_RANK, ROPE_DIM, ML_HEADS * ML_QK, ML_HEADS * ML_QK, ML_HEADS * ML_V, ML_HEADS, ML_HEADS, ML_HEADS * ML_V)
ODD_COLS = (SB_HEADS * SB_HD, SB_KV_HEADS * SB_HD, SB_KV_HEADS * SB_HD, NSA_HEADS * NSA_HD, 4 * NSA_HD, 2 * NSA_HD, 3 * NSA_HEADS)

LANE = 128
VMEM_LIMIT = 56 * 1024 * 1024
ROW_TILE = 512
EXPERT_TILE = 256


def _pick_tile(n, candidates):
    for c in candidates:
        if n % c == 0:
            return c
    raise ValueError(f"no tile in {candidates} divides {n}")


def _rms_matmul_kernel(*refs, do_norm, has_bias, has_res, act, pre_act):
    x_ref, g_ref, w_ref = refs[:3]
    rest = list(refs[3:])
    b_ref = rest.pop(0) if has_bias else None
    r_ref = rest.pop(0) if has_res else None
    o_ref, xn_ref = rest

    @pl.when(pl.program_id(1) == 0)
    def _():
        x = x_ref[...].astype(F32)
        if do_norm:
            ms = jnp.mean(x * x, axis=-1, keepdims=True)
            x = (x * lax.rsqrt(ms + EPS)) * g_ref[...]
        if pre_act == "gelu":
            x = jax.nn.gelu(x)
        xn_ref[...] = x.astype(BF16)

    acc = jnp.dot(xn_ref[...], w_ref[...], preferred_element_type=F32)
    if has_bias:
        acc = acc + b_ref[...]
    if act == "gelu":
        acc = jax.nn.gelu(acc)
    if has_res:
        acc = acc + r_ref[...]
    o_ref[...] = acc


def rms_matmul(x, w, g=None, res=None, bias=None, act=None, pre_act=None, keep_pad=False):
    n, k = x.shape
    m = w.shape[1]
    mp = -(-m // LANE) * LANE
    wb = w.astype(BF16)
    if mp != m:
        wb = jnp.pad(wb, ((0, 0), (0, mp - m)))
        if res is not None:
            res = jnp.pad(res, ((0, 0), (0, mp - m)))
        if bias is not None:
            bias = jnp.pad(bias, ((0, mp - m),))
    tm = _pick_tile(n, (ROW_TILE, 256, 128, 64, 32, 16, 8))
    tn = _pick_tile(mp, (512, 384, 256, 128))
    do_norm = g is not None
    gg = (g if do_norm else jnp.ones((k,), F32)).astype(F32).reshape(1, k)
    in_specs = [
        pl.BlockSpec((tm, k), lambda i, j: (i, 0)),
        pl.BlockSpec((1, k), lambda i, j: (0, 0)),
        pl.BlockSpec((k, tn), lambda i, j: (0, j)),
    ]
    args = [x, gg, wb]
    if bias is not None:
        in_specs.append(pl.BlockSpec((1, tn), lambda i, j: (0, j)))
        args.append(bias.astype(F32).reshape(1, mp))
    if res is not None:
        in_specs.append(pl.BlockSpec((tm, tn), lambda i, j: (i, j)))
        args.append(res)
    out = pl.pallas_call(
        functools.partial(_rms_matmul_kernel, do_norm=do_norm, has_bias=bias is not None,
                          has_res=res is not None, act=act, pre_act=pre_act),
        grid=(n // tm, mp // tn),
        in_specs=in_specs,
        out_specs=pl.BlockSpec((tm, tn), lambda i, j: (i, j)),
        out_shape=jax.ShapeDtypeStruct((n, mp), F32),
        scratch_shapes=[pltpu.VMEM((tm, k), BF16)],
        compiler_params=pltpu.CompilerParams(
            dimension_semantics=("parallel", "arbitrary"), vmem_limit_bytes=VMEM_LIMIT),
        name="rms_matmul",
    )(*args)
    return out if (keep_pad or mp == m) else out[:, :m]


ATT_TILE = 256


def _sb_prompt_kernel(q_ref, k_ref, v_ref, o_ref, qs_ref, acc_ref, run_ref, *, tq, groups, rep, hd, scale):
    qi = pl.program_id(1)
    kk = pl.program_id(2)
    nk = pl.num_programs(2)

    @pl.when(kk == 0)
    def _():
        for gi in range(groups):
            for ri in range(rep):
                c0 = (gi * rep + ri) * hd
                qs_ref[gi, ri * tq:(ri + 1) * tq, :] = (q_ref[:, c0:c0 + hd] * scale).astype(BF16)
        acc_ref[...] = jnp.zeros_like(acc_ref)
        run_ref[...] = jnp.zeros_like(run_ref)

    @pl.when(kk <= qi)
    def _():
        rows = rep * tq
        row_t = lax.broadcasted_iota(jnp.int32, (rows, tq), 0) % tq
        col = lax.broadcasted_iota(jnp.int32, (rows, tq), 1)
        mask = (col < row_t) | (kk > 0)
        later = (lax.broadcasted_iota(jnp.int32, (tq, tq), 0) >
                 lax.broadcasted_iota(jnp.int32, (tq, tq), 1)).astype(BF16)
        for gi in range(groups):
            kt = k_ref[:, gi * hd:(gi + 1) * hd].astype(BF16)
            vt = v_ref[:, gi * hd:(gi + 1) * hd].astype(BF16)
            z = lax.dot_general(qs_ref[gi], kt, (((1,), (1,)), ((), ())), preferred_element_type=F32)
            ls = jnp.minimum(z, 0.0) - jnp.log(1.0 + jnp.exp(-jnp.abs(z)))
            l1mb = jnp.where(mask, ls - z, 0.0)
            hi = l1mb.astype(BF16)
            lo = (l1mb - hi.astype(F32)).astype(BF16)
            aft = jnp.dot(hi, later, preferred_element_type=F32) + jnp.dot(lo, later, preferred_element_type=F32)
            run = run_ref[gi][:, :1]
            att = jnp.where(mask, jnp.exp(ls + aft + run), 0.0)
            acc_ref[gi] += jnp.dot(att.astype(BF16), vt, preferred_element_type=F32)
            run_ref[gi] = jnp.broadcast_to(run + aft[:, :1] + l1mb[:, :1], (rows, LANE))

    @pl.when(kk == nk - 1)
    def _():
        for gi in range(groups):
            for ri in range(rep):
                c0 = (gi * rep + ri) * hd
                o_ref[:, c0:c0 + hd] = acc_ref[gi, ri * tq:(ri + 1) * tq, :]


def sb_attn_prompt(proj, b, t, q_col, k_col, v_col):
    tq = ATT_TILE
    nq = t // tq
    qw = SB_HEADS * SB_HD
    kw = SB_KV_HEADS * SB_HD
    rep = SB_HEADS // SB_KV_HEADS
    assert t % tq == 0 and q_col % qw == 0 and k_col % kw == 0 and v_col % kw == 0
    kern = functools.partial(_sb_prompt_kernel, tq=tq, groups=SB_KV_HEADS, rep=rep, hd=SB_HD, scale=SB_HD ** -0.5)
    return pl.pallas_call(
        kern,
        grid=(b, nq, nq),
        in_specs=[
            pl.BlockSpec((tq, qw), lambda bi, qi, kk: (bi * nq + qi, q_col // qw)),
            pl.BlockSpec((tq, kw), lambda bi, qi, kk: (bi * nq + jnp.maximum(qi - kk, 0), k_col // kw)),
            pl.BlockSpec((tq, kw), lambda bi, qi, kk: (bi * nq + jnp.maximum(qi - kk, 0), v_col // kw)),
        ],
        out_specs=pl.BlockSpec((tq, qw), lambda bi, qi, kk: (bi * nq + qi, 0)),
        out_shape=jax.ShapeDtypeStruct((b * t, qw), F32),
        scratch_shapes=[
            pltpu.VMEM((SB_KV_HEADS, rep * tq, SB_HD), BF16),
            pltpu.VMEM((SB_KV_HEADS, rep * tq, SB_HD), F32),
            pltpu.VMEM((SB_KV_HEADS, rep * tq, LANE), F32),
        ],
        compiler_params=pltpu.CompilerParams(
            dimension_semantics=("parallel", "parallel", "arbitrary"), vmem_limit_bytes=VMEM_LIMIT),
        name="sb_attn_prompt",
    )(proj, proj, proj)


def _alibi_slopes_np(n):
    return [float(v) for v in (np.float32(2.0) ** (np.float32(-8.0) * np.arange(1, n + 1, dtype=np.float32) / np.float32(n)))]


def _mqa_flash_kernel(*refs, mode, heads, dk, dv, k_off, v_off, tq, scale, slopes, window, sel_block, nks):
    if mode == "select":
        q_ref, kv_ref, sel_ref, o_ref, qs_ref, m_ref, l_ref, acc_ref = refs
    else:
        q_ref, kv_ref, o_ref, qs_ref, m_ref, l_ref, acc_ref = refs
    qi = pl.program_id(1)
    kk = pl.program_id(2)
    kj = qi - (nks - 1) + kk if mode == "window" else kk

    @pl.when(kk == 0)
    def _():
        for h in range(heads):
            qs_ref[h] = q_ref[:, h * dk:(h + 1) * dk].astype(BF16)
        m_ref[...] = jnp.full_like(m_ref, NEG)
        l_ref[...] = jnp.zeros_like(l_ref)
        acc_ref[...] = jnp.zeros_like(acc_ref)

    active = (kj >= 0) if mode == "window" else (kj <= qi)

    @pl.when(active)
    def _():
        q_pos = qi * tq + lax.broadcasted_iota(jnp.int32, (tq, tq), 0)
        k_pos = kj * tq + lax.broadcasted_iota(jnp.int32, (tq, tq), 1)
        dist_i = q_pos - k_pos
        mask = dist_i >= 0
        if mode == "window":
            mask = mask & (dist_i < window)
        if mode == "select":
            nblk = sel_ref.shape[1]
            blk_of_key = (kj * tq + lax.broadcasted_iota(jnp.int32, (nblk, tq), 1)) // sel_block
            expand = (lax.broadcasted_iota(jnp.int32, (nblk, tq), 0) == blk_of_key).astype(BF16)
            chosen = jnp.dot(sel_ref[...].astype(BF16), expand, preferred_element_type=F32)
            mask = mask & (chosen > 0.5)
        dist = dist_i.astype(F32)
        kt = kv_ref[:, k_off:k_off + dk].astype(BF16)
        vt = kv_ref[:, v_off:v_off + dv].astype(BF16)
        for h in range(heads):
            s = lax.dot_general(qs_ref[h], kt, (((1,), (1,)), ((), ())), preferred_element_type=F32) * scale
            if slopes is not None:
                s = s - slopes[h] * dist
            s = jnp.where(mask, s, NEG)
            m_old = m_ref[h][:, :1]
            m_new = jnp.maximum(m_old, jnp.max(s, axis=-1, keepdims=True))
            alpha = jnp.exp(m_old - m_new)
            p = jnp.exp(s - m_new)
            l_ref[h] = jnp.broadcast_to(alpha * l_ref[h][:, :1] + jnp.sum(p, axis=-1, keepdims=True), (tq, LANE))
            acc_ref[h] = alpha * acc_ref[h] + jnp.dot(p.astype(BF16), vt, preferred_element_type=F32)
            m_ref[h] = jnp.broadcast_to(m_new, (tq, LANE))

    @pl.when(kk == nks - 1)
    def _():
        for h in range(heads):
            o_ref[:, h * dv:(h + 1) * dv] = acc_ref[h] / l_ref[h][:, :1]


def mqa_flash(q, q_col, kv, kv_col, kv_w, b, t, *, mode, heads, dk, dv, k_off, v_off, scale, slopes=None, sel=None):
    tq = ATT_TILE
    nq = t // tq
    qw = heads * dk
    assert t % tq == 0 and q_col % qw == 0 and kv_col % kv_w == 0
    nks = (WINDOW // tq + 1) if mode == "window" else nq
    if mode == "window":
        kv_idx = lambda bi, qi, kk: (bi * nq + jnp.maximum(qi - (nks - 1) + kk, 0), kv_col // kv_w)
    else:
        kv_idx = lambda bi, qi, kk: (bi * nq + jnp.minimum(kk, qi), kv_col // kv_w)
    in_specs = [
        pl.BlockSpec((tq, qw), lambda bi, qi, kk: (bi * nq + qi, q_col // qw)),
        pl.BlockSpec((tq, kv_w), kv_idx),
    ]
    args = [q, kv]
    if mode == "select":
        in_specs.append(pl.BlockSpec((tq, sel.shape[1]), lambda bi, qi, kk: (bi * nq + qi, 0)))
        args.append(sel)
    kern = functools.partial(_mqa_flash_kernel, mode=mode, heads=heads, dk=dk, dv=dv, k_off=k_off, v_off=v_off,
                             tq=tq, scale=scale, slopes=slopes, window=WINDOW, sel_block=SEL_BLOCK, nks=nks)
    return pl.pallas_call(
        kern,
        grid=(b, nq, nks),
        in_specs=in_specs,
        out_specs=pl.BlockSpec((tq, heads * dv), lambda bi, qi, kk: (bi * nq + qi, 0)),
        out_shape=jax.ShapeDtypeStruct((b * t, heads * dv), F32),
        scratch_shapes=[
            pltpu.VMEM((heads, tq, dk), BF16),
            pltpu.VMEM((heads, tq, LANE), F32),
            pltpu.VMEM((heads, tq, LANE), F32),
            pltpu.VMEM((heads, tq, dv), F32),
        ],
        compiler_params=pltpu.CompilerParams(
            dimension_semantics=("parallel", "parallel", "arbitrary"), vmem_limit_bytes=VMEM_LIMIT),
        name="mqa_flash_" + mode,
    )(*args)


def _nsa_cmp_kernel(q_ref, kc_ref, vc_ref, oc_ref, sel_ref, *, tq, heads, hd, scale, slopes, pos0, n_sel, k_top):
    qi = pl.program_id(1)
    nc = kc_ref.shape[1]
    lsel = sel_ref.shape[1]
    pos = pos0 + qi * tq + lax.broadcasted_iota(jnp.int32, (tq, nc), 0)
    cmp_end = CMP_STRIDE * lax.broadcasted_iota(jnp.int32, (tq, nc), 1) + (CMP_LEN - 1)
    valid = cmp_end <= pos
    dist = (pos - cmp_end).astype(F32)
    kc = kc_ref[0].astype(BF16)
    vc = vc_ref[0].astype(BF16)
    psum = jnp.zeros((tq, nc), F32)
    for h in range(heads):
        qh = (q_ref[:, h * hd:(h + 1) * hd] * scale).astype(BF16)
        s = lax.dot_general(qh, kc, (((1,), (1,)), ((), ())), preferred_element_type=F32) - slopes[h] * dist
        s = jnp.where(valid, s, NEG)
        e = jnp.exp(s - jnp.max(s, axis=-1, keepdims=True))
        p = jnp.where(valid, e / jnp.sum(e, axis=-1, keepdims=True), 0.0)
        oc_ref[:, h * hd:(h + 1) * hd] = jnp.dot(p.astype(BF16), vc, preferred_element_type=F32)
        psum = psum + p
    sel_ref[...] = _select_blocks(psum, pos0, qi * tq, tq, nc, lsel, n_sel, k_top)


def nsa_cmp_select(q, q_col, k_cmp, v_cmp, b, t, pos0, n_sel):
    tq = min(ATT_TILE, t)
    nq = t // tq
    qw = NSA_HEADS * NSA_HD
    nc = k_cmp.shape[1]
    lsel = -(-n_sel // LANE) * LANE
    assert t % tq == 0 and q_col % qw == 0
    kern = functools.partial(_nsa_cmp_kernel, tq=tq, heads=NSA_HEADS, hd=NSA_HD, scale=NSA_HD ** -0.5,
                             slopes=_alibi_slopes_np(NSA_HEADS), pos0=pos0, n_sel=n_sel, k_top=min(N_SEL, n_sel))
    return pl.pallas_call(
        kern,
        grid=(b, nq),
        in_specs=[
            pl.BlockSpec((tq, qw), lambda bi, qi: (bi * nq + qi, q_col // qw)),
            pl.BlockSpec((1, nc, NSA_HD), lambda bi, qi: (bi, 0, 0)),
            pl.BlockSpec((1, nc, NSA_HD), lambda bi, qi: (bi, 0, 0)),
        ],
        out_specs=[
            pl.BlockSpec((tq, qw), lambda bi, qi: (bi * nq + qi, 0)),
            pl.BlockSpec((tq, lsel), lambda bi, qi: (bi * nq + qi, 0)),
        ],
        out_shape=[jax.ShapeDtypeStruct((b * t, qw), F32), jax.ShapeDtypeStruct((b * t, lsel), F32)],
        compiler_params=pltpu.CompilerParams(
            dimension_semantics=("parallel", "parallel"), vmem_limit_bytes=VMEM_LIMIT),
        name="nsa_cmp_select",
    )(q, k_cmp, v_cmp)


def nsa_compress(ck, pe, wc1, wc2):
    b, n_chunk = ck.shape[:2]
    feat = CMP_STRIDE * NSA_HD
    x = jnp.moveaxis(ck, 3, 0).reshape(2, b * n_chunk, feat)
    w_halves = _cmp_weight_halves(wc1)
    pr = [rms_matmul(x[kind], w_halves[kind]).reshape(b, n_chunk, 2 * CMP_HID) for kind in range(2)]
    return nsa_compress_tail(pr, pe, wc1, wc2)


def _cmp_weight_halves(wc1):
    feat = CMP_STRIDE * NSA_HD
    return jnp.concatenate([wc1[:, :feat], wc1[:, feat:]], axis=2)


def nsa_compress_tail(pr, pe, wc1, wc2):
    b, n_chunk = pr[0].shape[:2]
    feat = CMP_STRIDE * NSA_HD
    pe_flat = jnp.pad(pe.reshape(2, 1, 2 * feat), ((0, 0), (0, 7), (0, 0)))
    out = []
    for kind in range(2):
        pe_term = rms_matmul(pe_flat[kind], wc1[kind])[0]
        nxt = jnp.concatenate([pr[kind][:, 1:, CMP_HID:], jnp.zeros((b, 1, CMP_HID), F32)], axis=1)
        pre = (pr[kind][:, :, :CMP_HID] + nxt + pe_term).reshape(b * n_chunk, CMP_HID)
        out.append(rms_matmul(pre, wc2[kind], pre_act="gelu").reshape(b, n_chunk, NSA_HD))
    return out[0], out[1]


PAGES_PER_STEP = 8


def _paged_attn_kernel(pt_ref, vis_ref, cnt_ref, *refs, mode, pg, t_new, kv_of, scale, kbase, q_base, page,
                       use_slopes, n_vis):
    refs = list(refs)
    q_ref = refs.pop(0)
    slope_ref = refs.pop(0)
    sel_ref = refs.pop(0) if mode == "select" else None
    page_refs = [refs.pop(0) for _ in range(pg)]
    newk_ref, newv_ref, o_ref, m_ref, l_ref, acc_ref = refs
    bi = pl.program_id(0)
    j = pl.program_id(1)
    n_steps = pl.num_programs(1)
    rows = q_ref.shape[1]
    cnt = cnt_ref[bi]

    @pl.when(j == 0)
    def _():
        m_ref[...] = jnp.full_like(m_ref, NEG)
        l_ref[...] = jnp.zeros_like(l_ref)
        acc_ref[...] = jnp.zeros_like(acc_ref)

    qb = q_ref[0].astype(BF16)
    q_pos = q_base + lax.broadcasted_iota(jnp.int32, (rows, page), 0) % t_new
    lane = lax.broadcasted_iota(jnp.int32, (rows, page), 1)

    def scores(k, kpos0, mask):
        s = lax.dot_general(qb, k.astype(BF16), (((1,), (1,)), ((), ())), preferred_element_type=F32) * scale
        dist_i = q_pos - (kpos0 + lane)
        if use_slopes:
            s = s - slope_ref[:, :1] * dist_i.astype(F32)
        if mode == "window":
            wmask = dist_i < WINDOW
            mask = wmask if mask is None else mask & wmask
        if mode == "select":
            selr = sel_ref[0]
            blk = lax.broadcasted_iota(jnp.int32, selr.shape, 1)
            blk0 = kpos0 // SEL_BLOCK
            c0 = jnp.sum(jnp.where(blk == blk0, selr, 0.0), axis=1, keepdims=True)
            c1 = jnp.sum(jnp.where(blk == blk0 + 1, selr, 0.0), axis=1, keepdims=True)
            cmask = jnp.where(lane < SEL_BLOCK, c0, c1) > 0.5
            mask = cmask if mask is None else mask & cmask
        if mask is not None:
            s = jnp.where(mask, s, NEG)
        return s

    def update(s_list, v_list):
        m_old = m_ref[:, :1]
        mx = s_list[0]
        for s in s_list[1:]:
            mx = jnp.maximum(mx, s)
        m_new = jnp.maximum(m_old, jnp.max(mx, axis=-1, keepdims=True))
        alpha = jnp.exp(m_old - m_new)
        p_list = [jnp.exp(s - m_new) for s in s_list]
        ps = p_list[0]
        for p in p_list[1:]:
            ps = ps + p
        acc = alpha * acc_ref[...]
        for p, v in zip(p_list, v_list):
            acc = acc + jnp.dot(p.astype(BF16), v.astype(BF16), preferred_element_type=F32)
        l_ref[...] = jnp.broadcast_to(alpha * l_ref[:, :1] + jnp.sum(ps, axis=-1, keepdims=True), l_ref.shape)
        m_ref[...] = jnp.broadcast_to(m_new, m_ref.shape)
        acc_ref[...] = acc

    @pl.when(j * pg < cnt)
    def _():
        s_list, v_list = [], []
        for i in range(pg):
            slot = j * pg + i
            k, v = kv_of(page_refs[i])
            s = scores(k, kbase + vis_ref[bi * n_vis + slot] * page, None)
            if mode == "select":
                s = jnp.where(slot < cnt, s, NEG)
            s_list.append(s)
            v_list.append(v)
        update(s_list, v_list)

    @pl.when(j == n_steps - 1)
    def _():
        dist_new = q_pos - (q_base + lane)
        s = scores(newk_ref[0], q_base, (lane < t_new) & (dist_new >= 0))
        update([s], [newv_ref[0]])
        o_ref[0] = acc_ref[...] / l_ref[:, :1]


def paged_attn(q, slopes_rows, pool, kv_of, dv, page_table, new_k, new_v, *, mode, t_new, scale, kbase, q_base,
               sel_rows=None, visit=None, use_slopes=True, pg=PAGES_PER_STEP):
    b, rows, dk = q.shape
    n_pages = page_table.shape[1]
    page = pool.shape[1]
    pg = min(pg, n_pages)
    assert n_pages % pg == 0
    n_steps = n_pages // pg
    if visit is None:
        visit = (jnp.tile(jnp.arange(n_pages, dtype=jnp.int32)[None], (b, 1)), jnp.full((b,), n_pages, jnp.int32))
    vis, cnt = visit
    blk = (1,) + pool.shape[1:]
    zeros = (0,) * (len(blk) - 1)
    in_specs = [
        pl.BlockSpec((1, rows, dk), lambda bi, j, pt, vs, ct: (bi, 0, 0)),
        pl.BlockSpec((rows, 1), lambda bi, j, pt, vs, ct: (0, 0)),
    ]
    args = [q, slopes_rows]
    if mode == "select":
        in_specs.append(pl.BlockSpec((1, rows, sel_rows.shape[2]), lambda bi, j, pt, vs, ct: (bi, 0, 0)))
        args.append(sel_rows)
    for i in range(pg):
        in_specs.append(pl.BlockSpec(
            blk, lambda bi, j, pt, vs, ct, i=i: (pt[bi * n_pages + vs[bi * n_pages + j * pg + i]],) + zeros))
        args.append(pool)
    in_specs.append(pl.BlockSpec((1, page, dk), lambda bi, j, pt, vs, ct: (bi, 0, 0)))
    in_specs.append(pl.BlockSpec((1, page, dv), lambda bi, j, pt, vs, ct: (bi, 0, 0)))
    args += [new_k, new_v]
    kern = functools.partial(_paged_attn_kernel, mode=mode, pg=pg, t_new=t_new, kv_of=kv_of, scale=scale, kbase=kbase,
                             q_base=q_base, page=page, use_slopes=use_slopes, n_vis=n_pages)
    return pl.pallas_call(
        kern,
        grid_spec=pltpu.PrefetchScalarGridSpec(
            num_scalar_prefetch=3,
            grid=(b, n_steps),
            in_specs=in_specs,
            out_specs=pl.BlockSpec((1, rows, dv), lambda bi, j, pt, vs, ct: (bi, 0, 0)),
            scratch_shapes=[
                pltpu.VMEM((rows, LANE), F32),
                pltpu.VMEM((rows, LANE), F32),
                pltpu.VMEM((rows, dv), F32),
            ],
        ),
        out_shape=jax.ShapeDtypeStruct((b, rows, dv), F32),
        compiler_params=pltpu.CompilerParams(
            dimension_semantics=("parallel", "arbitrary"), vmem_limit_bytes=VMEM_LIMIT),
        name="paged_attn_" + mode,
    )(page_table.reshape(-1), vis.reshape(-1), cnt, *args)


def _sb_paged_kernel(pt_ref, *refs, pg, t_new, kw, scale, page):
    refs = list(refs)
    q_ref = refs.pop(0)
    page_refs = [refs.pop(0) for _ in range(pg)]
    newk_ref, newv_ref, o_ref, run_ref, acc_ref = refs
    j = pl.program_id(1)
    n_steps = pl.num_programs(1)
    rows = q_ref.shape[1]
    groups = page_refs[0].shape[3]
    qb = (q_ref[0] * scale).astype(BF16)
    later = (lax.broadcasted_iota(jnp.int32, (page, page), 0) >
             lax.broadcasted_iota(jnp.int32, (page, page), 1)).astype(BF16)

    def logits(k, mask):
        z = lax.dot_general(qb, k.astype(BF16), (((1,), (1,)), ((), ())), preferred_element_type=F32)
        ls = jnp.minimum(z, 0.0) - jnp.log(1.0 + jnp.exp(-jnp.abs(z)))
        l1mb = ls - z
        if mask is not None:
            l1mb = jnp.where(mask, l1mb, 0.0)
        hi = l1mb.astype(BF16)
        lo = (l1mb - hi.astype(F32)).astype(BF16)
        aft = jnp.dot(hi, later, preferred_element_type=F32) + jnp.dot(lo, later, preferred_element_type=F32)
        return ls, aft, aft[:, :1] + l1mb[:, :1]

    @pl.when(j == 0)
    def _():
        mask = (lax.broadcasted_iota(jnp.int32, (rows, page), 1) <
                lax.broadcasted_iota(jnp.int32, (rows, page), 0) % t_new)
        ls, aft, tot = logits(newk_ref[0], mask)
        att = jnp.where(mask, jnp.exp(ls + aft), 0.0)
        acc_ref[...] = jnp.dot(att.astype(BF16), newv_ref[0].astype(BF16), preferred_element_type=F32)
        run_ref[...] = jnp.broadcast_to(tot, run_ref.shape)

    run = run_ref[:, :1]
    acc = acc_ref[...]
    for i in range(pg):
        pr = page_refs[i]
        k = jnp.concatenate([pr[0, :, 0, g, :] for g in range(groups)], axis=-1)
        v = jnp.concatenate([pr[0, :, 1, g, :] for g in range(groups)], axis=-1)
        ls, aft, tot = logits(k, None)
        att = jnp.exp(ls + aft + run)
        acc = acc + jnp.dot(att.astype(BF16), v.astype(BF16), preferred_element_type=F32)
        run = run + tot
    run_ref[...] = jnp.broadcast_to(run, run_ref.shape)
    acc_ref[...] = acc

    @pl.when(j == n_steps - 1)
    def _():
        o_ref[0] = acc_ref[...]


def sb_paged(q_rows, pool, page_table, new_k, new_v, t_new, pg=PAGES_PER_STEP):
    b, rows, kw = q_rows.shape
    n_pages = page_table.shape[1]
    page = pool.shape[1]
    pg = min(pg, n_pages)
    assert n_pages % pg == 0
    n_steps = n_pages // pg
    in_specs = [pl.BlockSpec((1, rows, kw), lambda bi, j, pt: (bi, 0, 0))]
    args = [q_rows]
    blk = (1,) + pool.shape[1:]
    for i in range(pg):
        in_specs.append(pl.BlockSpec(
            blk, lambda bi, j, pt, i=i: (pt[bi * n_pages + n_pages - 1 - (j * pg + i)], 0, 0, 0, 0)))
        args.append(pool)
    in_specs.append(pl.BlockSpec((1, page, kw), lambda bi, j, pt: (bi, 0, 0)))
    in_specs.append(pl.BlockSpec((1, page, kw), lambda bi, j, pt: (bi, 0, 0)))
    args += [new_k, new_v]
    kern = functools.partial(_sb_paged_kernel, pg=pg, t_new=t_new, kw=kw, scale=SB_HD ** -0.5, page=page)
    return pl.pallas_call(
        kern,
        grid_spec=pltpu.PrefetchScalarGridSpec(
            num_scalar_prefetch=1,
            grid=(b, n_steps),
            in_specs=in_specs,
            out_specs=pl.BlockSpec((1, rows, kw), lambda bi, j, pt: (bi, 0, 0)),
            scratch_shapes=[pltpu.VMEM((rows, LANE), F32), pltpu.VMEM((rows, kw), F32)],
        ),
        out_shape=jax.ShapeDtypeStruct((b, rows, kw), F32),
        compiler_params=pltpu.CompilerParams(
            dimension_semantics=("parallel", "arbitrary"), vmem_limit_bytes=VMEM_LIMIT),
        name="sb_paged",
    )(page_table.reshape(-1), *args)


def _chunk_proj_kernel(pt_ref, *refs, pg, rows_per_chunk, kinds):
    page_refs = refs[:pg]
    w_ref, o_ref = refs[pg:]
    page = page_refs[0].shape[1]
    n_c = page // rows_per_chunk
    for kind in range(kinds):
        acc = jnp.zeros((pg * n_c, w_ref.shape[3]), F32)
        for r in range(rows_per_chunk):
            a = jnp.concatenate([pr[0, pl.ds(r, n_c, stride=rows_per_chunk), kind, :] for pr in page_refs], axis=0)
            acc = acc + jnp.dot(a.astype(BF16), w_ref[kind, r], preferred_element_type=F32)
        o_ref[0, kind] = acc


def nsa_chunk_proj(pool, page_table, w_rows, pg=PAGES_PER_STEP):
    b, n_pages = page_table.shape
    page = pool.shape[1]
    n_c = page // CMP_STRIDE
    m = w_rows.shape[3]
    pg = min(pg, n_pages)
    assert n_pages % pg == 0
    blk = (1,) + pool.shape[1:]
    in_specs = [pl.BlockSpec(blk, lambda bi, j, pt, i=i: (pt[bi * n_pages + j * pg + i], 0, 0, 0)) for i in range(pg)]
    in_specs.append(pl.BlockSpec(w_rows.shape, lambda bi, j, pt: (0, 0, 0, 0)))
    return pl.pallas_call(
        functools.partial(_chunk_proj_kernel, pg=pg, rows_per_chunk=CMP_STRIDE, kinds=2),
        grid_spec=pltpu.PrefetchScalarGridSpec(
            num_scalar_prefetch=1,
            grid=(b, n_pages // pg),
            in_specs=in_specs,
            out_specs=pl.BlockSpec((1, 2, pg * n_c, m), lambda bi, j, pt: (bi, 0, j, 0)),
        ),
        out_shape=jax.ShapeDtypeStruct((b, 2, n_pages * n_c, m), F32),
        compiler_params=pltpu.CompilerParams(
            dimension_semantics=("parallel", "arbitrary"), vmem_limit_bytes=VMEM_LIMIT),
        name="nsa_chunk_proj",
    )(page_table.reshape(-1), *([pool] * pg), w_rows)


def _old_paged_mqa_kernel(pt_ref, *refs, mode, pg, t_new, dk, dv, k_off, v_off, scale, kbase, q_base, page, use_slopes):
    refs = list(refs)
    q_ref = refs.pop(0)
    slope_ref = refs.pop(0)
    sel_ref = refs.pop(0) if mode == "select" else None
    page_refs = [refs.pop(0) for _ in range(pg)]
    new_ref, o_ref, m_ref, l_ref, acc_ref = refs
    j = pl.program_id(1)
    n_steps = pl.num_programs(1)
    rows = q_ref.shape[1]

    @pl.when(j == 0)
    def _():
        m_ref[...] = jnp.full_like(m_ref, NEG)
        l_ref[...] = jnp.zeros_like(l_ref)
        acc_ref[...] = jnp.zeros_like(acc_ref)

    qb = q_ref[0].astype(BF16)
    q_pos = q_base + lax.broadcasted_iota(jnp.int32, (rows, page), 0) % t_new
    lane = lax.broadcasted_iota(jnp.int32, (rows, page), 1)

    def tile(kv, kpos0, is_new, carry):
        m_old, l_old, acc = carry
        kt = kv[:, k_off:k_off + dk].astype(BF16)
        vt = kv[:, v_off:v_off + dv].astype(BF16)
        s = lax.dot_general(qb, kt, (((1,), (1,)), ((), ())), preferred_element_type=F32) * scale
        dist_i = q_pos - (kpos0 + lane)
        if use_slopes:
            s = s - slope_ref[:, :1] * dist_i.astype(F32)
        mask = None
        if is_new:
            mask = (lane < t_new) & (dist_i >= 0)
        if mode == "window":
            wmask = dist_i < WINDOW
            mask = wmask if mask is None else mask & wmask
        if mode == "select":
            selr = sel_ref[0]
            blk = lax.broadcasted_iota(jnp.int32, selr.shape, 1)
            blk0 = kpos0 // SEL_BLOCK
            c0 = jnp.sum(jnp.where(blk == blk0, selr, 0.0), axis=1, keepdims=True)
            c1 = jnp.sum(jnp.where(blk == blk0 + 1, selr, 0.0), axis=1, keepdims=True)
            cmask = jnp.where(lane < SEL_BLOCK, c0, c1) > 0.5
            mask = cmask if mask is None else mask & cmask
        if mask is not None:
            s = jnp.where(mask, s, NEG)
        m_new = jnp.maximum(m_old, jnp.max(s, axis=-1, keepdims=True))
        alpha = jnp.exp(m_old - m_new)
        p = jnp.exp(s - m_new)
        l_new = alpha * l_old + jnp.sum(p, axis=-1, keepdims=True)
        acc = alpha * acc + jnp.dot(p.astype(BF16), vt, preferred_element_type=F32)
        return m_new, l_new, acc

    carry = (m_ref[:, :1], l_ref[:, :1], acc_ref[...])
    for i in range(pg):
        carry = tile(page_refs[i][0], kbase + (j * pg + i) * page, False, carry)
    m_ref[...] = jnp.broadcast_to(carry[0], m_ref.shape)
    l_ref[...] = jnp.broadcast_to(carry[1], l_ref.shape)
    acc_ref[...] = carry[2]

    @pl.when(j == n_steps - 1)
    def _():
        m_f, l_f, acc_f = tile(new_ref[0], q_base, True, (m_ref[:, :1], l_ref[:, :1], acc_ref[...]))
        o_ref[0] = acc_f / l_f


def paged_mqa(q, slopes_rows, pool, lane_blk, kv_w, page_table, new_tile, *, mode, t_new, dk, dv, k_off, v_off,
              scale, kbase, q_base, sel_rows=None, use_slopes=True):
    b, rows, _ = q.shape
    n_pages = page_table.shape[1]
    page = pool.shape[1]
    pg = min(PAGES_PER_STEP, n_pages)
    assert n_pages % pg == 0
    n_steps = n_pages // pg
    in_specs = [
        pl.BlockSpec((1, rows, dk), lambda bi, j, pt: (bi, 0, 0)),
        pl.BlockSpec((rows, 1), lambda bi, j, pt: (0, 0)),
    ]
    args = [q, slopes_rows]
    if mode == "select":
        in_specs.append(pl.BlockSpec((1, rows, sel_rows.shape[2]), lambda bi, j, pt: (bi, 0, 0)))
        args.append(sel_rows)
    for i in range(pg):
        in_specs.append(pl.BlockSpec(
            (1, page, kv_w), lambda bi, j, pt, i=i: (pt[bi * n_pages + j * pg + i], 0, lane_blk)))
        args.append(pool)
    in_specs.append(pl.BlockSpec((1, page, kv_w), lambda bi, j, pt: (bi, 0, 0)))
    args.append(new_tile)
    kern = functools.partial(_paged_mqa_kernel, mode=mode, pg=pg, t_new=t_new, dk=dk, dv=dv, k_off=k_off, v_off=v_off,
                             scale=scale, kbase=kbase, q_base=q_base, page=page, use_slopes=use_slopes)
    return pl.pallas_call(
        kern,
        grid_spec=pltpu.PrefetchScalarGridSpec(
            num_scalar_prefetch=1,
            grid=(b, n_steps),
            in_specs=in_specs,
            out_specs=pl.BlockSpec((1, rows, dv), lambda bi, j, pt: (bi, 0, 0)),
            scratch_shapes=[
                pltpu.VMEM((rows, LANE), F32),
                pltpu.VMEM((rows, LANE), F32),
                pltpu.VMEM((rows, dv), F32),
            ],
        ),
        out_shape=jax.ShapeDtypeStruct((b, rows, dv), F32),
        compiler_params=pltpu.CompilerParams(
            dimension_semantics=("parallel", "arbitrary"), vmem_limit_bytes=VMEM_LIMIT),
        name="paged_mqa_" + mode,
    )(page_table.reshape(-1), *args)


def _sb_decode_kernel(pt_ref, *refs, pg, t_new, kw, scale, page):
    refs = list(refs)
    q_ref = refs.pop(0)
    page_refs = [refs.pop(0) for _ in range(pg)]
    new_ref, o_ref, run_ref, acc_ref = refs
    j = pl.program_id(1)
    n_steps = pl.num_programs(1)
    rows = q_ref.shape[1]
    qb = (q_ref[0] * scale).astype(BF16)
    later = (lax.broadcasted_iota(jnp.int32, (page, page), 0) >
             lax.broadcasted_iota(jnp.int32, (page, page), 1)).astype(BF16)

    def tile(kv, is_new, carry):
        run, acc = carry
        kt = kv[:, :kw].astype(BF16)
        vt = kv[:, kw:2 * kw].astype(BF16)
        z = lax.dot_general(qb, kt, (((1,), (1,)), ((), ())), preferred_element_type=F32)
        ls = jnp.minimum(z, 0.0) - jnp.log(1.0 + jnp.exp(-jnp.abs(z)))
        l1mb = ls - z
        if is_new:
            mask = (lax.broadcasted_iota(jnp.int32, (rows, page), 1) <
                    lax.broadcasted_iota(jnp.int32, (rows, page), 0) % t_new)
            l1mb = jnp.where(mask, l1mb, 0.0)
        hi = l1mb.astype(BF16)
        lo = (l1mb - hi.astype(F32)).astype(BF16)
        aft = jnp.dot(hi, later, preferred_element_type=F32) + jnp.dot(lo, later, preferred_element_type=F32)
        att = jnp.exp(ls + aft + run)
        if is_new:
            att = jnp.where(mask, att, 0.0)
        acc = acc + jnp.dot(att.astype(BF16), vt, preferred_element_type=F32)
        return run + aft[:, :1] + l1mb[:, :1], acc

    @pl.when(j == 0)
    def _():
        run0, acc0 = tile(new_ref[0], True, (jnp.zeros((rows, 1), F32), jnp.zeros((rows, kw), F32)))
        run_ref[...] = jnp.broadcast_to(run0, run_ref.shape)
        acc_ref[...] = acc0

    carry = (run_ref[:, :1], acc_ref[...])
    for i in range(pg):
        carry = tile(page_refs[i][0], False, carry)
    run_ref[...] = jnp.broadcast_to(carry[0], run_ref.shape)
    acc_ref[...] = carry[1]

    @pl.when(j == n_steps - 1)
    def _():
        o_ref[0] = acc_ref[...]


def sb_decode(q_rows, pool, page_table, new_tile, t_new):
    b, rows, kw = q_rows.shape
    n_pages = page_table.shape[1]
    page = pool.shape[1]
    pg = min(PAGES_PER_STEP, n_pages)
    assert n_pages % pg == 0
    n_steps = n_pages // pg
    in_specs = [pl.BlockSpec((1, rows, kw), lambda bi, j, pt: (bi, 0, 0))]
    args = [q_rows]
    for i in range(pg):
        in_specs.append(pl.BlockSpec(
            (1, page, 2 * kw), lambda bi, j, pt, i=i: (pt[bi * n_pages + n_pages - 1 - (j * pg + i)], 0, 0)))
        args.append(pool)
    in_specs.append(pl.BlockSpec((1, page, 2 * kw), lambda bi, j, pt: (bi, 0, 0)))
    args.append(new_tile)
    kern = functools.partial(_sb_decode_kernel, pg=pg, t_new=t_new, kw=kw, scale=SB_HD ** -0.5, page=page)
    return pl.pallas_call(
        kern,
        grid_spec=pltpu.PrefetchScalarGridSpec(
            num_scalar_prefetch=1,
            grid=(b, n_steps),
            in_specs=in_specs,
            out_specs=pl.BlockSpec((1, rows, kw), lambda bi, j, pt: (bi, 0, 0)),
            scratch_shapes=[pltpu.VMEM((rows, LANE), F32), pltpu.VMEM((rows, kw), F32)],
        ),
        out_shape=jax.ShapeDtypeStruct((b, rows, kw), F32),
        compiler_params=pltpu.CompilerParams(
            dimension_semantics=("parallel", "arbitrary"), vmem_limit_bytes=VMEM_LIMIT),
        name="sb_decode",
    )(page_table.reshape(-1), *args)


def _nsa_cmp_decode_kernel(q_ref, slope_ref, kc_ref, vc_ref, oc_ref, sel_ref, *, t_new, heads, scale, q_base, n_sel, k_top):
    rows = q_ref.shape[1]
    nc = kc_ref.shape[1]
    lsel = sel_ref.shape[2]
    pos = q_base + lax.broadcasted_iota(jnp.int32, (rows, nc), 0) % t_new
    cmp_end = CMP_STRIDE * lax.broadcasted_iota(jnp.int32, (rows, nc), 1) + (CMP_LEN - 1)
    valid = cmp_end <= pos
    dist = (pos - cmp_end).astype(F32)
    qb = (q_ref[0] * scale).astype(BF16)
    kc = kc_ref[0].astype(BF16)
    vc = vc_ref[0].astype(BF16)
    s = lax.dot_general(qb, kc, (((1,), (1,)), ((), ())), preferred_element_type=F32) - slope_ref[:, :1] * dist
    s = jnp.where(valid, s, NEG)
    e = jnp.exp(s - jnp.max(s, axis=-1, keepdims=True))
    p = jnp.where(valid, e / jnp.sum(e, axis=-1, keepdims=True), 0.0)
    oc_ref[0] = jnp.dot(p.astype(BF16), vc, preferred_element_type=F32)
    psum = p[0:t_new]
    for h in range(1, heads):
        psum = psum + p[h * t_new:(h + 1) * t_new]
    sel_ref[0] = _select_blocks(psum, q_base, 0, t_new, nc, lsel, n_sel, k_top)


def _select_blocks(psum, pos0, row0, tq, nc, lsel, n_sel, k_top):
    c_start = CMP_STRIDE * lax.broadcasted_iota(jnp.int32, (nc, lsel), 0)
    s_start = SEL_BLOCK * lax.broadcasted_iota(jnp.int32, (nc, lsel), 1)
    overlap = ((c_start < s_start + SEL_BLOCK) & (c_start + CMP_LEN > s_start)).astype(F32)
    imp = jnp.dot(psum, overlap, preferred_element_type=F32, precision=lax.Precision.HIGHEST)
    tpos = pos0 + row0 + lax.broadcasted_iota(jnp.int32, (tq, lsel), 0)
    j = lax.broadcasted_iota(jnp.int32, (tq, lsel), 1)
    cur = tpos // SEL_BLOCK
    forced = (j == 0) | (j == cur) | (j == cur - 1)
    cand = (SEL_BLOCK * j <= tpos) & (j < n_sel)
    score = jnp.where(cand, imp + jnp.where(forced, FORCE_BONUS, 0.0), NEG)
    work = score
    chosen = jnp.zeros((tq, lsel), F32)
    for _ in range(k_top):
        mx = jnp.max(work, axis=-1, keepdims=True)
        first = jnp.min(jnp.where(work == mx, j, lsel), axis=-1, keepdims=True)
        hit = j == first
        chosen = jnp.where(hit, 1.0, chosen)
        work = jnp.where(hit, -3.0e38, work)
    return jnp.where(score > 0.5 * NEG, chosen, 0.0)


def nsa_cmp_decode(q_rows, slopes_rows, k_cmp, v_cmp, t_new, q_base, n_sel):
    b, rows, hd = q_rows.shape
    nc = k_cmp.shape[1]
    lsel = -(-n_sel // LANE) * LANE
    kern = functools.partial(_nsa_cmp_decode_kernel, t_new=t_new, heads=rows // t_new, scale=hd ** -0.5,
                             q_base=q_base, n_sel=n_sel, k_top=min(N_SEL, n_sel))
    return pl.pallas_call(
        kern,
        grid=(b,),
        in_specs=[
            pl.BlockSpec((1, rows, hd), lambda bi: (bi, 0, 0)),
            pl.BlockSpec((rows, 1), lambda bi: (0, 0)),
            pl.BlockSpec((1, nc, hd), lambda bi: (bi, 0, 0)),
            pl.BlockSpec((1, nc, hd), lambda bi: (bi, 0, 0)),
        ],
        out_specs=[
            pl.BlockSpec((1, rows, hd), lambda bi: (bi, 0, 0)),
            pl.BlockSpec((1, t_new, lsel), lambda bi: (bi, 0, 0)),
        ],
        out_shape=[jax.ShapeDtypeStruct((b, rows, hd), F32), jax.ShapeDtypeStruct((b, t_new, lsel), F32)],
        compiler_params=pltpu.CompilerParams(dimension_semantics=("parallel",), vmem_limit_bytes=VMEM_LIMIT),
        name="nsa_cmp_decode",
    )(q_rows, slopes_rows, k_cmp, v_cmp)


def _router_kernel(x_ref, g_ref, w_ref, b_ref, lg_ref, h_ref):
    x = x_ref[...]
    ms = jnp.mean(x * x, axis=-1, keepdims=True)
    h = (x * lax.rsqrt(ms + EPS)) * g_ref[...]
    h_ref[...] = h.astype(BF16)
    lg_ref[...] = jnp.dot(h, w_ref[...], preferred_element_type=F32,
                          precision=lax.Precision.HIGHEST) + b_ref[...]


def _experts_kernel(te_ref, tv_ref, xs_ref, sw_ref, w1_ref, w3_ref, w2_ref, o_ref):
    i = pl.program_id(0)

    @pl.when(tv_ref[i] != 0)
    def _():
        x = xs_ref[...]
        a = jnp.dot(x, w1_ref[0], preferred_element_type=F32)
        c = jnp.dot(x, w3_ref[0], preferred_element_type=F32)
        hid = (jax.nn.silu(a) * c) * sw_ref[...]
        o_ref[...] = jnp.dot(hid.astype(BF16), w2_ref[0], preferred_element_type=F32)

    @pl.when(tv_ref[i] == 0)
    def _():
        o_ref[...] = jnp.zeros_like(o_ref)


def moe(x, g, w_rg, b_rg, w_re, b_re, w_e1, w_e3, w_e2):
    n, d = x.shape
    ne = N_GROUPS * N_EXP
    tm = _pick_tile(n, (ROW_TILE, 256, 128))
    wr = jnp.zeros((d, LANE), F32).at[:, :N_GROUPS].set(w_rg).at[:, N_GROUPS:N_GROUPS + ne].set(w_re)
    br = jnp.zeros((1, LANE), F32).at[0, :N_GROUPS].set(b_rg).at[0, N_GROUPS:N_GROUPS + ne].set(b_re)
    logits, h = pl.pallas_call(
        _router_kernel,
        grid=(n // tm,),
        in_specs=[
            pl.BlockSpec((tm, d), lambda i: (i, 0)),
            pl.BlockSpec((1, d), lambda i: (0, 0)),
            pl.BlockSpec((d, LANE), lambda i: (0, 0)),
            pl.BlockSpec((1, LANE), lambda i: (0, 0)),
        ],
        out_specs=[pl.BlockSpec((tm, LANE), lambda i: (i, 0)), pl.BlockSpec((tm, d), lambda i: (i, 0))],
        out_shape=[jax.ShapeDtypeStruct((n, LANE), F32), jax.ShapeDtypeStruct((n, d), BF16)],
        compiler_params=pltpu.CompilerParams(dimension_semantics=("parallel",), vmem_limit_bytes=VMEM_LIMIT),
        name="moe_router",
    )(x, g.astype(F32).reshape(1, d), wr, br)

    lg = logits[:, :N_GROUPS]
    le = logits[:, N_GROUPS:N_GROUPS + ne].reshape(n, N_GROUPS, N_EXP)
    pg = jax.nn.softmax(lg, axis=-1)
    gi = jnp.argmax(lg, axis=-1)
    gw = jnp.take_along_axis(pg, gi[:, None], axis=1)[:, 0]
    le_g = jnp.take_along_axis(le, gi[:, None, None], axis=1)[:, 0]
    top_v, top_i = lax.top_k(le_g, TOP_K)
    we = jax.nn.softmax(top_v, axis=-1)
    eid = (gi[:, None] * N_EXP + top_i).astype(jnp.int32)
    wt = we * gw[:, None]

    ts = EXPERT_TILE
    n_tiles = -(-(TOP_K * n) // ts) + ne
    member = (eid[:, :, None] == jnp.arange(ne)[None, None, :]).any(axis=1).astype(jnp.int32)
    cnt = member.sum(axis=0)
    rank = jnp.cumsum(member, axis=0) - member
    tiles_e = (cnt + ts - 1) // ts
    tile_end = jnp.cumsum(tiles_e)
    pad_off = (tile_end - tiles_e) * ts
    pos = pad_off[eid] + jnp.take_along_axis(rank, eid, axis=1)
    tok = jnp.broadcast_to(jnp.arange(n, dtype=jnp.int32)[:, None], (n, TOP_K))
    slot_tok = jnp.zeros((n_tiles * ts,), jnp.int32).at[pos.reshape(-1)].set(tok.reshape(-1))
    slot_w = jnp.zeros((n_tiles * ts,), F32).at[pos.reshape(-1)].set(wt.reshape(-1))
    tile_id = jnp.arange(n_tiles, dtype=jnp.int32)
    tile_e = jnp.minimum(jnp.searchsorted(tile_end, tile_id, side="right"), ne - 1).astype(jnp.int32)
    tile_v = (tile_id < tile_end[-1]).astype(jnp.int32)

    xs = jnp.take(h, slot_tok, axis=0)
    w1 = w_e1.reshape(ne, d, D_EXP).astype(BF16)
    w3 = w_e3.reshape(ne, d, D_EXP).astype(BF16)
    w2 = w_e2.reshape(ne, D_EXP, d).astype(BF16)
    ys = pl.pallas_call(
        _experts_kernel,
        grid_spec=pltpu.PrefetchScalarGridSpec(
            num_scalar_prefetch=2,
            grid=(n_tiles,),
            in_specs=[
                pl.BlockSpec((ts, d), lambda i, te, tv: (i, 0)),
                pl.BlockSpec((ts, 1), lambda i, te, tv: (i, 0)),
                pl.BlockSpec((1, d, D_EXP), lambda i, te, tv: (te[i], 0, 0)),
                pl.BlockSpec((1, d, D_EXP), lambda i, te, tv: (te[i], 0, 0)),
                pl.BlockSpec((1, D_EXP, d), lambda i, te, tv: (te[i], 0, 0)),
            ],
            out_specs=pl.BlockSpec((ts, d), lambda i, te, tv: (i, 0)),
        ),
        out_shape=jax.ShapeDtypeStruct((n_tiles * ts, d), F32),
        compiler_params=pltpu.CompilerParams(dimension_semantics=("arbitrary",), vmem_limit_bytes=VMEM_LIMIT),
        name="moe_experts",
    )(tile_e, tile_v, xs, slot_w.reshape(-1, 1), w1, w3, w2)
    return jnp.take(ys, pos[:, 0], axis=0) + jnp.take(ys, pos[:, 1], axis=0)


def rmsnorm(x, g):
    xf = x.astype(F32)
    y = xf * lax.rsqrt(jnp.mean(xf * xf, axis=-1, keepdims=True) + EPS)
    return (y * g.astype(F32)).astype(x.dtype)


def split_cols(a, widths):
    return jnp.split(a, np.cumsum(widths)[:-1].tolist(), axis=-1)


def qblock(t):
    return min(QBLOCK, t)


def map_query_blocks(fn, arrays, block):
    t = arrays[0].shape[1]
    nb = -(-t // block)
    pad = nb * block - t

    def to_blocks(a):
        a = jnp.pad(a, [(0, 0), (0, pad)] + [(0, 0)] * (a.ndim - 2))
        a = a.reshape((a.shape[0], nb, block) + a.shape[2:])
        return jnp.moveaxis(a, 1, 0)

    xs = (jnp.arange(nb),) + tuple(to_blocks(a) for a in arrays)
    out = lax.map(lambda args: fn(*args), xs)
    out = jnp.moveaxis(out, 0, 1)
    out = out.reshape((out.shape[0], nb * block) + out.shape[3:])
    return out[:, :t]


def rope_angles(pos, dim):
    inv = ROPE_THETA ** (-jnp.arange(0, dim, 2, dtype=F32) / dim)
    ang = pos.astype(F32)[:, None] * inv[None, :]
    return jnp.cos(ang), jnp.sin(ang)


def apply_rope(x, cos, sin):
    half = x.shape[-1] // 2
    x1 = x[..., :half].astype(F32)
    x2 = x[..., half:].astype(F32)
    return jnp.concatenate([x1 * cos - x2 * sin, x1 * sin + x2 * cos], axis=-1).astype(x.dtype)


def alibi_slopes(n):
    return 2.0 ** (-8.0 * jnp.arange(1, n + 1, dtype=F32) / n)


def gather_pages(pool, page_table):
    g = pool[page_table]
    return g.reshape((page_table.shape[0], page_table.shape[1] * pool.shape[1]) + pool.shape[2:])


def mla_mixer(c_q, c_kv_raw, k_r_raw, pos, past, g_cq, w_uq, g_ckv, w_uk, w_uv):
    b, t, _ = c_q.shape
    cos, sin = rope_angles(pos, ROPE_DIM)
    q = jnp.einsum('btr,rhd->bthd', rmsnorm(c_q, g_cq), w_uq)
    q_nope = q[..., :NOPE_DIM]
    q_rope = apply_rope(q[..., NOPE_DIM:], cos[None, :, None, :], sin[None, :, None, :])
    q_lat = jnp.einsum('bthd,rhd->bthr', q_nope, w_uk)
    c_kv = rmsnorm(c_kv_raw, g_ckv)
    k_r = apply_rope(k_r_raw, cos[None], sin[None])
    new_rows = jnp.concatenate([c_kv, k_r], axis=-1)
    keys = jnp.concatenate([past, new_rows], axis=1)
    k_pos = jnp.arange(keys.shape[1])
    lat, kr = keys[..., :KV_RANK], keys[..., KV_RANK:]
    scale = (NOPE_DIM + ROPE_DIM) ** -0.5

    def blk(bi, ql, qr, qp):
        s = (jnp.einsum('bqhr,bsr->bhqs', ql, lat) + jnp.einsum('bqhd,bsd->bhqs', qr, kr)).astype(F32) * scale
        mask = k_pos[None, :] <= qp[0][:, None]
        p = jax.nn.softmax(jnp.where(mask, s, NEG), axis=-1).astype(lat.dtype)
        return jnp.einsum('bhqs,bsr->bqhr', p, lat)

    o_lat = map_query_blocks(blk, (q_lat, q_rope, pos[None]), qblock(t))
    out = jnp.einsum('bthr,rhd->bthd', o_lat, w_uv).reshape(b, t, MLA_HEADS * MLA_VDIM)
    return out, new_rows


def mlstm_mixer(q, k, v, ig, fg, og, c0, n0, m0, g_mh):
    b, t = q.shape[:2]
    dt = q.dtype
    q = q.reshape(b, t, ML_HEADS, ML_QK)
    k = k.reshape(b, t, ML_HEADS, ML_QK) * (ML_QK ** -0.5)
    v = v.reshape(b, t, ML_HEADS, ML_V)
    li = ig.astype(F32)
    lf = jax.nn.log_sigmoid(fg.astype(F32))
    L = min(ML_CHUNK, t)
    nc = -(-t // L)
    pad = nc * L - t

    def chunks(a, val=0.0):
        a = jnp.pad(a, [(0, 0), (0, pad)] + [(0, 0)] * (a.ndim - 2), constant_values=val)
        return jnp.moveaxis(a.reshape((b, nc, L) + a.shape[2:]), 1, 0)

    tri = jnp.tril(jnp.ones((L, L), dtype=bool))

    def step(carry, xs):
        c, n, m = carry
        qc, kc, vc, lic, lfc = xs
        cb = jnp.cumsum(lfc, axis=1)
        a = cb + m[:, None, :]
        d = cb[:, :, None, :] - cb[:, None, :, :] + lic[:, None, :, :]
        d = jnp.where(tri[None, :, :, None], d, NEG)
        mt = jnp.maximum(a, d.max(axis=2))
        w = jnp.exp(d - mt[:, :, None, :])
        inter = jnp.exp(a - mt)
        sc = w * jnp.einsum('bthd,bshd->btsh', qc, kc).astype(F32)
        num = jnp.einsum('btsh,bshv->bthv', sc, vc) + inter[..., None] * jnp.einsum('bhvd,bthd->bthv', c, qc)
        den = sc.sum(axis=2) + inter * jnp.einsum('bhd,bthd->bth', n, qc)
        h = num / jnp.maximum(jnp.abs(den), jnp.exp(-mt))[..., None]
        wl, il = w[:, -1], inter[:, -1]
        c = il[..., None, None] * c + jnp.einsum('bsh,bshv,bshd->bhvd', wl, vc, kc)
        n = il[..., None] * n + jnp.einsum('bsh,bshd->bhd', wl, kc)
        return (c, n, mt[:, -1]), h

    carry0 = (c0.astype(F32), n0.astype(F32), m0.astype(F32))
    (c, n, m), h = lax.scan(step, carry0, (chunks(q), chunks(k), chunks(v), chunks(li, NEG), chunks(lf)))
    h = jnp.moveaxis(h, 0, 1).reshape(b, nc * L, ML_HEADS, ML_V)[:, :t]
    h = h - h.mean(axis=-1, keepdims=True)
    h = h * lax.rsqrt(jnp.mean(h * h, axis=-1, keepdims=True) + EPS) * g_mh.astype(F32)
    out = jax.nn.sigmoid(og.astype(F32)).reshape(b, t, ML_HEADS, ML_V) * h
    return out.reshape(b, t, ML_HEADS * ML_V).astype(dt), c, n, m


def sb_mixer(q, k_new, v_new, pos, past_kv):
    b, t = q.shape[:2]
    q = q.reshape(b, t, SB_KV_HEADS, SB_HEADS // SB_KV_HEADS, SB_HD)
    new_rows = jnp.stack([k_new.reshape(b, t, SB_KV_HEADS, SB_HD), v_new.reshape(b, t, SB_KV_HEADS, SB_HD)], axis=2)
    kv = jnp.concatenate([past_kv, new_rows], axis=1)
    k, v = kv[:, :, 0], kv[:, :, 1]
    k_pos = jnp.arange(kv.shape[1])

    def blk(bi, qh, qp):
        z = jnp.einsum('bqgrd,bsgd->bgrqs', qh, k).astype(F32) * (SB_HD ** -0.5)
        mask = k_pos[None, :] < qp[0][:, None]
        l1mb = jnp.where(mask, jax.nn.log_sigmoid(-z), 0.0)
        after = lax.cumsum(l1mb, axis=z.ndim - 1, reverse=True) - l1mb
        att = jnp.where(mask, jnp.exp(jax.nn.log_sigmoid(z) + after), 0.0).astype(v.dtype)
        return jnp.einsum('bgrqs,bsgd->bqgrd', att, v)

    o = map_query_blocks(blk, (q, pos[None]), qblock(t))
    return o.reshape(b, t, SB_HEADS * SB_HD), new_rows


def nsa_mixer(q, kv_new, win_new, gate, pos, past_kv, win_prior, pe, wc1, wc2):
    b, t = q.shape[:2]
    past = past_kv.shape[1]
    s_len = past + t
    scale = NSA_HD ** -0.5
    slopes = alibi_slopes(NSA_HEADS)
    kv_all = jnp.concatenate([past_kv, kv_new], axis=1)
    n_chunk = max(-(-s_len // CMP_STRIDE), 2)
    ck = jnp.pad(kv_all[:, :, :2], ((0, 0), (0, n_chunk * CMP_STRIDE - s_len), (0, 0), (0, 0)))
    ck = ck.reshape(b, n_chunk, CMP_STRIDE, 2, NSA_HD)
    w1 = wc1.reshape(2, 2, CMP_STRIDE, NSA_HD, CMP_HID)
    proj = jnp.einsum('bcrkd,kzrdh->bczkh', ck, w1)
    pe_term = jnp.einsum('kpd,kpdh->kh', pe, wc1.reshape(2, CMP_LEN, NSA_HD, CMP_HID))
    hid = jax.nn.gelu(proj[:, :-1, 0] + proj[:, 1:, 1] + pe_term)
    cmp = jnp.einsum('bnkh,khd->bnkd', hid, wc2)
    k_cmp, v_cmp = cmp[:, :, 0], cmp[:, :, 1]
    cmp_start = CMP_STRIDE * jnp.arange(n_chunk - 1)
    cmp_end = cmp_start + CMP_LEN - 1
    n_sel = -(-s_len // SEL_BLOCK)
    sel = jnp.pad(kv_all[:, :, 2:], ((0, 0), (0, n_sel * SEL_BLOCK - s_len), (0, 0), (0, 0)))
    sel = sel.reshape(b, n_sel, SEL_BLOCK, 2, NSA_HD)
    sel_start = SEL_BLOCK * jnp.arange(n_sel)
    overlap = ((cmp_start[:, None] < sel_start[None, :] + SEL_BLOCK) & (cmp_start[:, None] + CMP_LEN > sel_start[None, :])).astype(F32)
    k_top = min(N_SEL, n_sel)
    bidx = jnp.arange(b)[:, None, None]
    pw = win_prior.shape[1]
    qb = qblock(t)
    nb = -(-t // qb)
    band = jnp.concatenate([jnp.zeros((b, WINDOW - pw, 2, NSA_HD), win_new.dtype), win_prior, win_new, jnp.zeros((b, nb * qb - t, 2, NSA_HD), win_new.dtype)], axis=1)
    band_pos = past - WINDOW + jnp.arange(WINDOW + nb * qb)

    def blk(bi, qh, gh, qp):
        tq = qp[0]
        valid_c = cmp_end[None, :] <= tq[:, None]
        dist_c = (tq[:, None] - cmp_end[None, :]).astype(F32)
        s_c = jnp.einsum('bqhd,bnd->bhqn', qh, k_cmp).astype(F32) * scale - slopes[:, None, None] * dist_c
        p_c = jnp.where(valid_c, jax.nn.softmax(jnp.where(valid_c, s_c, NEG), axis=-1), 0.0)
        o_c = jnp.einsum('bhqn,bnd->bqhd', p_c.astype(v_cmp.dtype), v_cmp)
        imp = jnp.einsum('bhqn,nj->bqj', p_c, overlap)
        cur = tq // SEL_BLOCK
        j = jnp.arange(n_sel)
        forced = (j[None, :] == 0) | (j[None, :] == cur[:, None]) | (j[None, :] == cur[:, None] - 1)
        cand = sel_start[None, :] <= tq[:, None]
        score = jnp.where(cand[None], imp + jnp.where(forced, FORCE_BONUS, 0.0)[None], NEG)
        top_s, idx = lax.top_k(score, k_top)
        kv_sel = sel[bidx, idx]
        kpos = idx[..., None] * SEL_BLOCK + jnp.arange(SEL_BLOCK)
        ok = (top_s > 0.5 * NEG)[..., None] & (kpos <= tq[None, :, None, None])
        dist_s = (tq[None, :, None, None] - kpos).astype(F32)
        s_s = jnp.einsum('bqhd,bqkld->bqhkl', qh, kv_sel[..., 0, :]).astype(F32) * scale - slopes[None, None, :, None, None] * dist_s[:, :, None]
        s_s = jnp.where(ok[:, :, None], s_s, NEG)
        p_s = jax.nn.softmax(s_s.reshape(s_s.shape[:3] + (-1,)), axis=-1).reshape(s_s.shape)
        o_s = jnp.einsum('bqhkl,bqkld->bqhd', p_s.astype(kv_sel.dtype), kv_sel[..., 1, :])
        bw = lax.dynamic_slice_in_dim(band, bi * qb, qb + WINDOW, axis=1)
        bp = lax.dynamic_slice_in_dim(band_pos, bi * qb, qb + WINDOW, axis=0)
        dist_w = tq[:, None] - bp[None, :]
        ok_w = (bp[None, :] >= 0) & (dist_w >= 0) & (dist_w < WINDOW)
        s_w = jnp.einsum('bqhd,bsd->bhqs', qh, bw[:, :, 0]).astype(F32) * scale - slopes[:, None, None] * dist_w.astype(F32)
        p_w = jax.nn.softmax(jnp.where(ok_w, s_w, NEG), axis=-1)
        o_w = jnp.einsum('bhqs,bsd->bqhd', p_w.astype(bw.dtype), bw[:, :, 1])
        g = jax.nn.sigmoid(gh.astype(F32))
        return (g[..., 0:1] * o_c + g[..., 1:2] * o_s + g[..., 2:3] * o_w).astype(qh.dtype)

    o = map_query_blocks(blk, (q, gate, pos[None]), qb)
    new_win = jnp.concatenate([win_prior, win_new], axis=1)[:, -min(WINDOW, pw + t):]
    return o.reshape(b, t, NSA_HEADS * NSA_HD), kv_new, new_win


def mem_kv_rows(mem, g, wk, wv):
    b, nm, d = mem.shape
    kv = rms_matmul(mem.reshape(b * nm, d), jnp.concatenate([wk, wv], axis=1), g=g)
    hd = X_HEADS * X_HD
    k = kv[:, :hd].reshape(b, nm, X_HEADS, X_HD)
    v = kv[:, hd:].reshape(b, nm, X_HEADS, X_HD)
    return jnp.stack([k, v], axis=2)


def mem_attend(x, g, mkv, wq, wo):
    b, t, d = x.shape
    q = rms_matmul(x.reshape(b * t, d), wq, g=g).reshape(b, t, X_HEADS, X_HD)
    s = jnp.einsum('bthd,bmhd->bhtm', q, mkv[:, :, 0]).astype(F32) * (X_HD ** -0.5)
    p = jax.nn.softmax(s, axis=-1).astype(mkv.dtype)
    o = jnp.einsum('bhtm,bmhd->bthd', p, mkv[:, :, 1]).reshape(b * t, X_HEADS * X_HD)
    return rms_matmul(o, wo, res=x.reshape(b * t, d)).reshape(b, t, d)


def odd_mixers_prompt(proj, b, t, pe, wc1, wc2):
    n = b * t
    sb_w = SB_HEADS * SB_HD
    kv_w = SB_KV_HEADS * SB_HD
    q_col = sb_w + 2 * kv_w
    kv_col = q_col + NSA_HEADS * NSA_HD
    win_col = kv_col + 4 * NSA_HD
    gate_col = win_col + 2 * NSA_HD
    o_sb = sb_attn_prompt(proj, b, t, 0, sb_w, sb_w + kv_w)
    sb_rows = proj[:, sb_w:sb_w + 2 * kv_w].reshape(b, t, 2, SB_KV_HEADS, SB_HD)
    nsa_rows = proj[:, kv_col:win_col].reshape(b, t, 4, NSA_HD)
    win_rows = proj[:, win_col:gate_col].reshape(b, t, 2, NSA_HD)
    ck = proj[:, kv_col:kv_col + 2 * NSA_HD].reshape(b, t // CMP_STRIDE, CMP_STRIDE, 2, NSA_HD)
    k_cmp, v_cmp = nsa_compress(ck, pe, wc1, wc2)
    o_c, sel = nsa_cmp_select(proj, q_col, k_cmp, v_cmp, b, t, 0, t // SEL_BLOCK)
    slopes = _alibi_slopes_np(NSA_HEADS)
    common = dict(heads=NSA_HEADS, dk=NSA_HD, dv=NSA_HD, k_off=0, v_off=NSA_HD, scale=NSA_HD ** -0.5, slopes=slopes)
    o_s = mqa_flash(proj, q_col, proj, kv_col + 2 * NSA_HD, 2 * NSA_HD, b, t, mode="select", sel=sel, **common)
    o_w = mqa_flash(proj, q_col, proj, win_col, 2 * NSA_HD, b, t, mode="window", **common)
    g = jax.nn.sigmoid(proj[:, gate_col:gate_col + 3 * NSA_HEADS]).reshape(n, NSA_HEADS, 3)
    sh = (n, NSA_HEADS, NSA_HD)
    o_nsa = (g[..., 0:1] * o_c.reshape(sh) + g[..., 1:2] * o_s.reshape(sh) + g[..., 2:3] * o_w.reshape(sh)).reshape(n, -1)
    return o_sb, sb_rows, o_nsa, nsa_rows, win_rows[:, -min(WINDOW, t):]


def _pad_rows(a, rows):
    return jnp.pad(a, ((0, 0), (0, rows - a.shape[1]), (0, 0)))


def _rows_head_major(a, b, t, heads):
    w = a.shape[1] // heads
    return jnp.transpose(a.reshape(b, t, heads, w), (0, 2, 1, 3)).reshape(b, heads * t, w)


def _rows_token_major(a, b, t, heads):
    w = a.shape[2]
    return jnp.transpose(a.reshape(b, heads, t, w), (0, 2, 1, 3)).reshape(b * t, heads * w)


def odd_mixers_sample(proj, b, t, past, page_table, pool_sb, pool_nsa, win_prior, pe, wc1, wc2):
    n = b * t
    page = pool_sb.shape[1]
    sb_w = SB_HEADS * SB_HD
    kv_w = SB_KV_HEADS * SB_HD
    rep = SB_HEADS // SB_KV_HEADS
    q_col = sb_w + 2 * kv_w
    kv_col = q_col + NSA_HEADS * NSA_HD
    win_col = kv_col + 4 * NSA_HD
    gate_col = win_col + 2 * NSA_HD
    assert past % page == 0 and past % SEL_BLOCK == 0 and t <= CMP_STRIDE and t <= page

    q = jnp.transpose(proj[:, :sb_w].reshape(b, t, SB_KV_HEADS, rep, SB_HD), (0, 2, 3, 1, 4))
    eye = jnp.eye(SB_KV_HEADS, dtype=F32)
    q_rows = (q[:, :, :, :, None, :] * eye[None, :, None, None, :, None]).reshape(b, SB_HEADS * t, kv_w)
    new_sk = _pad_rows(proj[:, sb_w:sb_w + kv_w].reshape(b, t, kv_w), page)
    new_sv = _pad_rows(proj[:, sb_w + kv_w:sb_w + 2 * kv_w].reshape(b, t, kv_w), page)
    o = sb_paged(q_rows, pool_sb, page_table, new_sk, new_sv, t)
    o = o.reshape(b, SB_KV_HEADS, rep, t, SB_KV_HEADS, SB_HD)
    o = jnp.stack([o[:, g, :, :, g] for g in range(SB_KV_HEADS)], axis=1)
    o_sb = jnp.transpose(o, (0, 3, 1, 2, 4)).reshape(n, sb_w)
    sb_rows = proj[:, sb_w:sb_w + 2 * kv_w].reshape(b, t, 2, SB_KV_HEADS, SB_HD)

    nsa_rows = proj[:, kv_col:win_col].reshape(b, t, 4, NSA_HD)
    win_new = proj[:, win_col:gate_col].reshape(b, t, 2, NSA_HD)
    n_pages = page_table.shape[1]
    w_rows = _cmp_weight_halves(wc1).reshape(2, CMP_STRIDE, NSA_HD, 2 * CMP_HID).astype(BF16)
    pr = nsa_chunk_proj(pool_nsa, page_table, w_rows)
    k_cmp, v_cmp = nsa_compress_tail([pr[:, 0], pr[:, 1]], pe, wc1, wc2)
    slopes_rows = jnp.repeat(jnp.asarray(_alibi_slopes_np(NSA_HEADS), F32), t).reshape(NSA_HEADS * t, 1)
    q_nsa = _rows_head_major(proj[:, q_col:kv_col], b, t, NSA_HEADS)
    n_sel = -(-(past + t) // SEL_BLOCK)
    o_c, sel = nsa_cmp_decode(q_nsa, slopes_rows, k_cmp, v_cmp, t, past, n_sel)
    common = dict(t_new=t, scale=NSA_HD ** -0.5, q_base=past)

    def new_cols(c0):
        return _pad_rows(proj[:, c0:c0 + NSA_HD].reshape(b, t, NSA_HD), page)

    per_page = page // SEL_BLOCK
    picked = sel[:, :, :n_pages * per_page].reshape(b, t, n_pages, per_page).sum(axis=(1, 3)) > 0.5
    cnt = picked.sum(axis=1).astype(jnp.int32)
    order = jnp.argsort(jnp.logical_not(picked), axis=1, stable=True).astype(jnp.int32)
    last = jnp.take_along_axis(order, jnp.maximum(cnt - 1, 0)[:, None], axis=1)
    vis = jnp.where(jnp.arange(n_pages)[None, :] < cnt[:, None], order, last)
    o_s = paged_attn(q_nsa, slopes_rows, pool_nsa, lambda r: (r[0, :, 2, :], r[0, :, 3, :]), NSA_HD, page_table,
                     new_cols(kv_col + 2 * NSA_HD), new_cols(kv_col + 3 * NSA_HD), mode="select", kbase=0,
                     sel_rows=jnp.tile(sel, (1, NSA_HEADS, 1)), visit=(vis, cnt), pg=16, **common)
    pw = win_prior.shape[1]
    assert pw % page == 0 and pw <= past
    win_pool = win_prior.reshape(b * (pw // page), page, 2, NSA_HD)
    win_pt = jnp.arange(b * (pw // page), dtype=jnp.int32).reshape(b, pw // page)
    o_w = paged_attn(q_nsa, slopes_rows, win_pool, lambda r: (r[0, :, 0, :], r[0, :, 1, :]), NSA_HD, win_pt,
                     new_cols(win_col), new_cols(win_col + NSA_HD), mode="window", kbase=past - pw, **common)
    g = jax.nn.sigmoid(proj[:, gate_col:gate_col + 3 * NSA_HEADS]).reshape(n, NSA_HEADS, 3)
    sh = (n, NSA_HEADS, NSA_HD)
    o_nsa = (g[..., 0:1] * _rows_token_major(o_c, b, t, NSA_HEADS).reshape(sh)
             + g[..., 1:2] * _rows_token_major(o_s, b, t, NSA_HEADS).reshape(sh)
             + g[..., 2:3] * _rows_token_major(o_w, b, t, NSA_HEADS).reshape(sh)).reshape(n, -1)
    win = jnp.concatenate([win_prior, win_new], axis=1)[:, -min(WINDOW, pw + t):]
    return o_sb, sb_rows, o_nsa, nsa_rows, win


def mla_attend(c_q, c_kv_raw, k_r_raw, b, t, start, g_cq, w_uq, g_ckv, w_uk, w_uv, paged=None):
    n = b * t
    pos = start + jnp.arange(t, dtype=jnp.int32)
    cos, sin = rope_angles(pos, ROPE_DIM)
    cos = jnp.tile(cos, (b, 1))
    sin = jnp.tile(sin, (b, 1))
    q = rms_matmul(c_q, w_uq.reshape(Q_RANK, -1), g=g_cq).reshape(n, MLA_HEADS, NOPE_DIM + ROPE_DIM)
    q_rope = apply_rope(q[..., NOPE_DIM:], cos[:, None, :], sin[:, None, :])
    q_lat = jnp.einsum('nhd,rhd->nhr', q[..., :NOPE_DIM], w_uk)
    new_rows = jnp.concatenate([rmsnorm(c_kv_raw, g_ckv), apply_rope(k_r_raw, cos, sin)], axis=-1)
    qf = jnp.concatenate([q_lat, q_rope], axis=-1).reshape(n, -1)
    dk = KV_RANK + ROPE_DIM
    scale = (NOPE_DIM + ROPE_DIM) ** -0.5
    if paged is None:
        o_lat = mqa_flash(qf, 0, new_rows, 0, dk, b, t, mode="causal", heads=MLA_HEADS, dk=dk, dv=KV_RANK,
                          k_off=0, v_off=0, scale=scale)
    else:
        pool, page_table = paged
        page = pool.shape[1]
        new_k = _pad_rows(new_rows.reshape(b, t, dk), page)
        o = paged_attn(_rows_head_major(qf, b, t, MLA_HEADS), jnp.zeros((MLA_HEADS * t, 1), F32), pool,
                       lambda r: (r[0], r[0, :, :KV_RANK]), KV_RANK, page_table, new_k, new_k[:, :, :KV_RANK],
                       mode="causal", t_new=t, scale=scale, kbase=0, q_base=start, use_slopes=False, pg=16)
        o_lat = _rows_token_major(o, b, t, MLA_HEADS)
    out = jnp.einsum('nhr,rhd->nhd', o_lat.reshape(n, MLA_HEADS, KV_RANK), w_uv).reshape(n, MLA_HEADS * MLA_VDIM)
    return out, new_rows


def trunk(x, start, ml_state, mem_kv, p, caches=None):
    b, t, d = x.shape
    depth = p['g_mix'].shape[0]
    mla_rows, ml_c, ml_n, ml_m, sb_rows, nsa_rows, wins = [], [], [], [], [], [], []
    for l in range(depth):
        e = l // 2
        x2 = x.reshape(b * t, d)
        if l % 2 == 0:
            proj = rms_matmul(x2, p['w_in_even'][e], g=p['g_mix'][l])
            c_q, c_kv, k_r, mq, mk, mv, mi, mf, mo = split_cols(proj, EVEN_COLS)
            paged = None if caches is None else (caches['mla'][e], caches['page_table'])
            o_a, rows = mla_attend(c_q, c_kv, k_r, b, t, start, p['g_cq'][e], p['w_uq'][e], p['g_ckv'][e], p['w_uk'][e], p['w_uv'][e], paged=paged)
            rows = rows.reshape(b, t, -1)
            mq, mk, mv, mi, mf, mo = (a.reshape(b, t, -1) for a in (mq, mk, mv, mi, mf, mo))
            c0, n0, m0 = ml_state[e]
            o_b, c, n, m = mlstm_mixer(mq, mk, mv, mi + p['b_ml_i'][e], mf + p['b_ml_f'][e], mo, c0, n0, m0, p['g_mh'][e])
            mla_rows.append(rows)
            ml_c.append(c)
            ml_n.append(n)
            ml_m.append(m)
            mix = jnp.concatenate([o_a, o_b.reshape(b * t, -1)], axis=-1)
            x = rms_matmul(mix, p['w_out_even'][e], res=x2).reshape(b, t, d)
        else:
            proj = rms_matmul(x2, p['w_in_odd'][e], g=p['g_mix'][l], keep_pad=True)
            if caches is None:
                o_c, srows, o_d, nrows, win = odd_mixers_prompt(proj, b, t, p['nsa_pe'][e], p['nsa_wc1'][e], p['nsa_wc2'][e])
            else:
                o_c, srows, o_d, nrows, win = odd_mixers_sample(
                    proj, b, t, start, caches['page_table'], caches['sb'][e], caches['nsa'][e], caches['win'][e],
                    p['nsa_pe'][e], p['nsa_wc1'][e], p['nsa_wc2'][e])
            sb_rows.append(srows)
            nsa_rows.append(nrows)
            wins.append(win)
            mix = jnp.concatenate([o_c, o_d], axis=-1)
            x = rms_matmul(mix, p['w_out_odd'][e], res=x2).reshape(b, t, d)
        x = mem_attend(x, p['g_xattn'][l], mem_kv[l], p['w_xq'][l], p['w_xo'][l])
        x2 = x.reshape(b * t, d)
        x = (x2 + moe(x2, p['g_ffn'][l], p['w_rg'][l], p['b_rg'][l], p['w_re'][l], p['b_re'][l], p['w_e1'][l], p['w_e3'][l], p['w_e2'][l])).reshape(b, t, d)
    y = rmsnorm(x, p['g_final'])
    return y, mla_rows, ml_c, ml_n, ml_m, sb_rows, nsa_rows, wins


def kernel(x_prompt, x_sample, mem_prompt, cache_mla, state_mlstm_c, state_mlstm_n, state_mlstm_m, cache_sb_kv, cache_nsa_kv, state_nsa_win, cache_mem_kv, page_table, g_mix, w_in_even, b_ml_i, b_ml_f, g_cq, w_uq, g_ckv, w_uk, w_uv, g_mh, w_out_even, w_in_odd, nsa_pe, nsa_wc1, nsa_wc2, w_out_odd, g_xattn, g_memnorm, w_xq, w_xk, w_xv, w_xo, g_ffn, w_rg, b_rg, w_re, b_re, w_e1, w_e3, w_e2, g_final):
    p = dict(g_mix=g_mix, w_in_even=w_in_even, b_ml_i=b_ml_i, b_ml_f=b_ml_f, g_cq=g_cq, w_uq=w_uq, g_ckv=g_ckv, w_uk=w_uk, w_uv=w_uv, g_mh=g_mh, w_out_even=w_out_even, w_in_odd=w_in_odd, nsa_pe=nsa_pe, nsa_wc1=nsa_wc1, nsa_wc2=nsa_wc2, w_out_odd=w_out_odd, g_xattn=g_xattn, w_xq=w_xq, w_xo=w_xo, g_ffn=g_ffn, w_rg=w_rg, b_rg=b_rg, w_re=w_re, b_re=b_re, w_e1=w_e1, w_e3=w_e3, w_e2=w_e2, g_final=g_final)
    dt = x_prompt.dtype
    bp = x_prompt.shape[0]
    depth = g_mix.shape[0]
    n_even = (depth + 1) // 2
    n_odd = depth // 2
    mem_kv_list_p = [mem_kv_rows(mem_prompt, g_memnorm[l], w_xk[l], w_xv[l]) for l in range(depth)]
    y_prompt, mla_p, c_p, n_p, m_p, sb_p, nsa_p, win_pl = trunk(
        x_prompt, 0,
        [(jnp.zeros((bp, ML_HEADS, ML_V, ML_QK), F32), jnp.zeros((bp, ML_HEADS, ML_QK), F32), jnp.zeros((bp, ML_HEADS), F32)) for _ in range(n_even)],
        mem_kv_list_p, p)
    past_len = page_table.shape[1] * cache_mla.shape[2]
    caches = dict(page_table=page_table, mla=[cache_mla[e] for e in range(n_even)],
                  sb=[cache_sb_kv[e] for e in range(n_odd)], nsa=[cache_nsa_kv[e] for e in range(n_odd)],
                  win=[state_nsa_win[e] for e in range(n_odd)])
    y_sample, mla_s, c_s, n_s, m_s, sb_s, nsa_s, win_sl = trunk(
        x_sample, past_len,
        [(state_mlstm_c[e], state_mlstm_n[e], state_mlstm_m[e]) for e in range(n_even)],
        [cache_mem_kv[l] for l in range(depth)], p, caches=caches)
    return (y_prompt, y_sample, jnp.stack(mla_p), jnp.stack(mla_s), jnp.stack(c_p), jnp.stack(c_s),
            jnp.stack(n_p), jnp.stack(n_s), jnp.stack(m_p), jnp.stack(m_s), jnp.stack(sb_p), jnp.stack(sb_s),
            jnp.stack(nsa_p), jnp.stack(nsa_s), jnp.stack(win_pl), jnp.stack(win_sl), jnp.stack(mem_kv_list_p))
```

```python
import functools

import jax
import jax.numpy as jnp
import numpy as np
from jax import lax
from jax.experimental import pallas as pl
from jax.experimental.pallas import tpu as pltpu

F32 = jnp.float32
BF16 = jnp.bfloat16

D_MODEL = 2048
QBLOCK = 128
NEG = -1e30
EPS = 1e-6
MLA_HEADS = 8
Q_RANK = 512
KV_RANK = 256
NOPE_DIM = 128
ROPE_DIM = 64
MLA_VDIM = 128
ROPE_THETA = 10000.0
ML_HEADS = 4
ML_QK = 128
ML_V = 256
ML_CHUNK = 64
SB_HEADS = 8
SB_KV_HEADS = 4
SB_HD = 64
NSA_HEADS = 16
NSA_HD = 64
CMP_STRIDE = 16
CMP_LEN = 2 * CMP_STRIDE
CMP_HID = 128
SEL_BLOCK = 64
N_SEL = 16
WINDOW = 512
FORCE_BONUS = 1000.0
X_HEADS = 4
X_HD = 128
N_GROUPS = 4
N_EXP = 8
TOP_K = 2
D_EXP = 512

EVEN_COLS = (Q_RANK, KV_RANK, ROPE_DIM, ML_HEADS * ML_QK, ML_HEADS * ML_QK, ML_HEADS * ML_V, ML_HEADS, ML_HEADS, ML_HEADS * ML_V)
ODD_COLS = (SB_HEADS * SB_HD, SB_KV_HEADS * SB_HD, SB_KV_HEADS * SB_HD, NSA_HEADS * NSA_HD, 4 * NSA_HD, 2 * NSA_HD, 3 * NSA_HEADS)

LANE = 128
VMEM_LIMIT = 56 * 1024 * 1024
ROW_TILE = 512
EXPERT_TILE = 256


def _pick_tile(n, candidates):
    for c in candidates:
        if n % c == 0:
            return c
    raise ValueError(f"no tile in {candidates} divides {n}")


def _rms_matmul_kernel(*refs, do_norm, has_bias, has_res, act, pre_act):
    x_ref, g_ref, w_ref = refs[:3]
    rest = list(refs[3:])
    b_ref = rest.pop(0) if has_bias else None
    r_ref = rest.pop(0) if has_res else None
    o_ref, xn_ref = rest

    @pl.when(pl.program_id(1) == 0)
    def _():
        x = x_ref[...].astype(F32)
        if do_norm:
            ms = jnp.mean(x * x, axis=-1, keepdims=True)
            x = (x * lax.rsqrt(ms + EPS)) * g_ref[...]
        if pre_act == "gelu":
            x = jax.nn.gelu(x)
        xn_ref[...] = x.astype(BF16)

    acc = jnp.dot(xn_ref[...], w_ref[...], preferred_element_type=F32)
    if has_bias:
        acc = acc + b_ref[...]
    if act == "gelu":
        acc = jax.nn.gelu(acc)
    if has_res:
        acc = acc + r_ref[...]
    o_ref[...] = acc


def rms_matmul(x, w, g=None, res=None, bias=None, act=None, pre_act=None, keep_pad=False):
    n, k = x.shape
    m = w.shape[1]
    mp = -(-m // LANE) * LANE
    wb = w.astype(BF16)
    if mp != m:
        wb = jnp.pad(wb, ((0, 0), (0, mp - m)))
        if res is not None:
            res = jnp.pad(res, ((0, 0), (0, mp - m)))
        if bias is not None:
            bias = jnp.pad(bias, ((0, mp - m),))
    tm = _pick_tile(n, (ROW_TILE, 256, 128, 64, 32, 16, 8))
    tn = _pick_tile(mp, (512, 384, 256, 128))
    do_norm = g is not None
    gg = (g if do_norm else jnp.ones((k,), F32)).astype(F32).reshape(1, k)
    in_specs = [
        pl.BlockSpec((tm, k), lambda i, j: (i, 0)),
        pl.BlockSpec((1, k), lambda i, j: (0, 0)),
        pl.BlockSpec((k, tn), lambda i, j: (0, j)),
    ]
    args = [x, gg, wb]
    if bias is not None:
        in_specs.append(pl.BlockSpec((1, tn), lambda i, j: (0, j)))
        args.append(bias.astype(F32).reshape(1, mp))
    if res is not None:
        in_specs.append(pl.BlockSpec((tm, tn), lambda i, j: (i, j)))
        args.append(res)
    out = pl.pallas_call(
        functools.partial(_rms_matmul_kernel, do_norm=do_norm, has_bias=bias is not None,
                          has_res=res is not None, act=act, pre_act=pre_act),
        grid=(n // tm, mp // tn),
        in_specs=in_specs,
        out_specs=pl.BlockSpec((tm, tn), lambda i, j: (i, j)),
        out_shape=jax.ShapeDtypeStruct((n, mp), F32),
        scratch_shapes=[pltpu.VMEM((tm, k), BF16)],
        compiler_params=pltpu.CompilerParams(
            dimension_semantics=("parallel", "arbitrary"), vmem_limit_bytes=VMEM_LIMIT),
        name="rms_matmul",
    )(*args)
    return out if (keep_pad or mp == m) else out[:, :m]


ATT_TILE = 256


def _sb_prompt_kernel(q_ref, k_ref, v_ref, o_ref, qs_ref, acc_ref, run_ref, *, tq, groups, rep, hd, scale):
    qi = pl.program_id(1)
    kk = pl.program_id(2)
    nk = pl.num_programs(2)

    @pl.when(kk == 0)
    def _():
        for gi in range(groups):
            for ri in range(rep):
                c0 = (gi * rep + ri) * hd
                qs_ref[gi, ri * tq:(ri + 1) * tq, :] = (q_ref[:, c0:c0 + hd] * scale).astype(BF16)
        acc_ref[...] = jnp.zeros_like(acc_ref)
        run_ref[...] = jnp.zeros_like(run_ref)

    @pl.when(kk <= qi)
    def _():
        rows = rep * tq
        row_t = lax.broadcasted_iota(jnp.int32, (rows, tq), 0) % tq
        col = lax.broadcasted_iota(jnp.int32, (rows, tq), 1)
        mask = (col < row_t) | (kk > 0)
        later = (lax.broadcasted_iota(jnp.int32, (tq, tq), 0) >
                 lax.broadcasted_iota(jnp.int32, (tq, tq), 1)).astype(BF16)
        for gi in range(groups):
            kt = k_ref[:, gi * hd:(gi + 1) * hd].astype(BF16)
            vt = v_ref[:, gi * hd:(gi + 1) * hd].astype(BF16)
            z = lax.dot_general(qs_ref[gi], kt, (((1,), (1,)), ((), ())), preferred_element_type=F32)
            ls = jnp.minimum(z, 0.0) - jnp.log(1.0 + jnp.exp(-jnp.abs(z)))
            l1mb = jnp.where(mask, ls - z, 0.0)
            hi = l1mb.astype(BF16)
            lo = (l1mb - hi.astype(F32)).astype(BF16)
            aft = jnp.dot(hi, later, preferred_element_type=F32) + jnp.dot(lo, later, preferred_element_type=F32)
            run = run_ref[gi][:, :1]
            att = jnp.where(mask, jnp.exp(ls + aft + run), 0.0)
            acc_ref[gi] += jnp.dot(att.astype(BF16), vt, preferred_element_type=F32)
            run_ref[gi] = jnp.broadcast_to(run + aft[:, :1] + l1mb[:, :1], (rows, LANE))

    @pl.when(kk == nk - 1)
    def _():
        for gi in range(groups):
            for ri in range(rep):
                c0 = (gi * rep + ri) * hd
                o_ref[:, c0:c0 + hd] = acc_ref[gi, ri * tq:(ri + 1) * tq, :]


def sb_attn_prompt(proj, b, t, q_col, k_col, v_col):
    tq = ATT_TILE
    nq = t // tq
    qw = SB_HEADS * SB_HD
    kw = SB_KV_HEADS * SB_HD
    rep = SB_HEADS // SB_KV_HEADS
    assert t % tq == 0 and q_col % qw == 0 and k_col % kw == 0 and v_col % kw == 0
    kern = functools.partial(_sb_prompt_kernel, tq=tq, groups=SB_KV_HEADS, rep=rep, hd=SB_HD, scale=SB_HD ** -0.5)
    return pl.pallas_call(
        kern,
        grid=(b, nq, nq),
        in_specs=[
            pl.BlockSpec((tq, qw), lambda bi, qi, kk: (bi * nq + qi, q_col // qw)),
            pl.BlockSpec((tq, kw), lambda bi, qi, kk: (bi * nq + jnp.maximum(qi - kk, 0), k_col // kw)),
            pl.BlockSpec((tq, kw), lambda bi, qi, kk: (bi * nq + jnp.maximum(qi - kk, 0), v_col // kw)),
        ],
        out_specs=pl.BlockSpec((tq, qw), lambda bi, qi, kk: (bi * nq + qi, 0)),
        out_shape=jax.ShapeDtypeStruct((b * t, qw), F32),
        scratch_shapes=[
            pltpu.VMEM((SB_KV_HEADS, rep * tq, SB_HD), BF16),
            pltpu.VMEM((SB_KV_HEADS, rep * tq, SB_HD), F32),
            pltpu.VMEM((SB_KV_HEADS, rep * tq, LANE), F32),
        ],
        compiler_params=pltpu.CompilerParams(
            dimension_semantics=("parallel", "parallel", "arbitrary"), vmem_limit_bytes=VMEM_LIMIT),
        name="sb_attn_prompt",
    )(proj, proj, proj)


def _alibi_slopes_np(n):
    return [float(v) for v in (np.float32(2.0) ** (np.float32(-8.0) * np.arange(1, n + 1, dtype=np.float32) / np.float32(n)))]


def _mqa_flash_kernel(*refs, mode, heads, dk, dv, k_off, v_off, tq, scale, slopes, window, sel_block, nks):
    if mode == "select":
        q_ref, kv_ref, sel_ref, o_ref, qs_ref, m_ref, l_ref, acc_ref = refs
    else:
        q_ref, kv_ref, o_ref, qs_ref, m_ref, l_ref, acc_ref = refs
    qi = pl.program_id(1)
    kk = pl.program_id(2)
    kj = qi - (nks - 1) + kk if mode == "window" else kk

    @pl.when(kk == 0)
    def _():
        for h in range(heads):
            qs_ref[h] = q_ref[:, h * dk:(h + 1) * dk].astype(BF16)
        m_ref[...] = jnp.full_like(m_ref, NEG)
        l_ref[...] = jnp.zeros_like(l_ref)
        acc_ref[...] = jnp.zeros_like(acc_ref)

    active = (kj >= 0) if mode == "window" else (kj <= qi)

    @pl.when(active)
    def _():
        q_pos = qi * tq + lax.broadcasted_iota(jnp.int32, (tq, tq), 0)
        k_pos = kj * tq + lax.broadcasted_iota(jnp.int32, (tq, tq), 1)
        dist_i = q_pos - k_pos
        mask = dist_i >= 0
        if mode == "window":
            mask = mask & (dist_i < window)
        if mode == "select":
            nblk = sel_ref.shape[1]
            blk_of_key = (kj * tq + lax.broadcasted_iota(jnp.int32, (nblk, tq), 1)) // sel_block
            expand = (lax.broadcasted_iota(jnp.int32, (nblk, tq), 0) == blk_of_key).astype(BF16)
            chosen = jnp.dot(sel_ref[...].astype(BF16), expand, preferred_element_type=F32)
            mask = mask & (chosen > 0.5)
        dist = dist_i.astype(F32)
        kt = kv_ref[:, k_off:k_off + dk].astype(BF16)
        vt = kv_ref[:, v_off:v_off + dv].astype(BF16)
        for h in range(heads):
            s = lax.dot_general(qs_ref[h], kt, (((1,), (1,)), ((), ())), preferred_element_type=F32) * scale
            if slopes is not None:
                s = s - slopes[h] * dist
            s = jnp.where(mask, s, NEG)
            m_old = m_ref[h][:, :1]
            m_new = jnp.maximum(m_old, jnp.max(s, axis=-1, keepdims=True))
            alpha = jnp.exp(m_old - m_new)
            p = jnp.exp(s - m_new)
            l_ref[h] = jnp.broadcast_to(alpha * l_ref[h][:, :1] + jnp.sum(p, axis=-1, keepdims=True), (tq, LANE))
            acc_ref[h] = alpha * acc_ref[h] + jnp.dot(p.astype(BF16), vt, preferred_element_type=F32)
            m_ref[h] = jnp.broadcast_to(m_new, (tq, LANE))

    @pl.when(kk == nks - 1)
    def _():
        for h in range(heads):
            o_ref[:, h * dv:(h + 1) * dv] = acc_ref[h] / l_ref[h][:, :1]


def mqa_flash(q, q_col, kv, kv_col, kv_w, b, t, *, mode, heads, dk, dv, k_off, v_off, scale, slopes=None, sel=None):
    tq = ATT_TILE
    nq = t // tq
    qw = heads * dk
    assert t % tq == 0 and q_col % qw == 0 and kv_col % kv_w == 0
    nks = (WINDOW // tq + 1) if mode == "window" else nq
    if mode == "window":
        kv_idx = lambda bi, qi, kk: (bi * nq + jnp.maximum(qi - (nks - 1) + kk, 0), kv_col // kv_w)
    else:
        kv_idx = lambda bi, qi, kk: (bi * nq + jnp.minimum(kk, qi), kv_col // kv_w)
    in_specs = [
        pl.BlockSpec((tq, qw), lambda bi, qi, kk: (bi * nq + qi, q_col // qw)),
        pl.BlockSpec((tq, kv_w), kv_idx),
    ]
    args = [q, kv]
    if mode == "select":
        in_specs.append(pl.BlockSpec((tq, sel.shape[1]), lambda bi, qi, kk: (bi * nq + qi, 0)))
        args.append(sel)
    kern = functools.partial(_mqa_flash_kernel, mode=mode, heads=heads, dk=dk, dv=dv, k_off=k_off, v_off=v_off,
                             tq=tq, scale=scale, slopes=slopes, window=WINDOW, sel_block=SEL_BLOCK, nks=nks)
    return pl.pallas_call(
        kern,
        grid=(b, nq, nks),
        in_specs=in_specs,
        out_specs=pl.BlockSpec((tq, heads * dv), lambda bi, qi, kk: (bi * nq + qi, 0)),
        out_shape=jax.ShapeDtypeStruct((b * t, heads * dv), F32),
        scratch_shapes=[
            pltpu.VMEM((heads, tq, dk), BF16),
            pltpu.VMEM((heads, tq, LANE), F32),
            pltpu.VMEM((heads, tq, LANE), F32),
            pltpu.VMEM((heads, tq, dv), F32),
        ],
        compiler_params=pltpu.CompilerParams(
            dimension_semantics=("parallel", "parallel", "arbitrary"), vmem_limit_bytes=VMEM_LIMIT),
        name="mqa_flash_" + mode,
    )(*args)


def _nsa_cmp_kernel(q_ref, kc_ref, vc_ref, oc_ref, sel_ref, *, tq, heads, hd, scale, slopes, pos0, n_sel, k_top):
    qi = pl.program_id(1)
    nc = kc_ref.shape[1]
    lsel = sel_ref.shape[1]
    pos = pos0 + qi * tq + lax.broadcasted_iota(jnp.int32, (tq, nc), 0)
    cmp_end = CMP_STRIDE * lax.broadcasted_iota(jnp.int32, (tq, nc), 1) + (CMP_LEN - 1)
    valid = cmp_end <= pos
    dist = (pos - cmp_end).astype(F32)
    kc = kc_ref[0].astype(BF16)
    vc = vc_ref[0].astype(BF16)
    psum = jnp.zeros((tq, nc), F32)
    for h in range(heads):
        qh = (q_ref[:, h * hd:(h + 1) * hd] * scale).astype(BF16)
        s = lax.dot_general(qh, kc, (((1,), (1,)), ((), ())), preferred_element_type=F32) - slopes[h] * dist
        s = jnp.where(valid, s, NEG)
        e = jnp.exp(s - jnp.max(s, axis=-1, keepdims=True))
        p = jnp.where(valid, e / jnp.sum(e, axis=-1, keepdims=True), 0.0)
        oc_ref[:, h * hd:(h + 1) * hd] = jnp.dot(p.astype(BF16), vc, preferred_element_type=F32)
        psum = psum + p
    sel_ref[...] = _select_blocks(psum, pos0, qi * tq, tq, nc, lsel, n_sel, k_top)


def nsa_cmp_select(q, q_col, k_cmp, v_cmp, b, t, pos0, n_sel):
    tq = min(ATT_TILE, t)
    nq = t // tq
    qw = NSA_HEADS * NSA_HD
    nc = k_cmp.shape[1]
    lsel = -(-n_sel // LANE) * LANE
    assert t % tq == 0 and q_col % qw == 0
    kern = functools.partial(_nsa_cmp_kernel, tq=tq, heads=NSA_HEADS, hd=NSA_HD, scale=NSA_HD ** -0.5,
                             slopes=_alibi_slopes_np(NSA_HEADS), pos0=pos0, n_sel=n_sel, k_top=min(N_SEL, n_sel))
    return pl.pallas_call(
        kern,
        grid=(b, nq),
        in_specs=[
            pl.BlockSpec((tq, qw), lambda bi, qi: (bi * nq + qi, q_col // qw)),
            pl.BlockSpec((1, nc, NSA_HD), lambda bi, qi: (bi, 0, 0)),
            pl.BlockSpec((1, nc, NSA_HD), lambda bi, qi: (bi, 0, 0)),
        ],
        out_specs=[
            pl.BlockSpec((tq, qw), lambda bi, qi: (bi * nq + qi, 0)),
            pl.BlockSpec((tq, lsel), lambda bi, qi: (bi * nq + qi, 0)),
        ],
        out_shape=[jax.ShapeDtypeStruct((b * t, qw), F32), jax.ShapeDtypeStruct((b * t, lsel), F32)],
        compiler_params=pltpu.CompilerParams(
            dimension_semantics=("parallel", "parallel"), vmem_limit_bytes=VMEM_LIMIT),
        name="nsa_cmp_select",
    )(q, k_cmp, v_cmp)


def nsa_compress(ck, pe, wc1, wc2):
    b, n_chunk = ck.shape[:2]
    feat = CMP_STRIDE * NSA_HD
    x = jnp.moveaxis(ck, 3, 0).reshape(2, b * n_chunk, feat)
    w_halves = _cmp_weight_halves(wc1)
    pr = [rms_matmul(x[kind], w_halves[kind]).reshape(b, n_chunk, 2 * CMP_HID) for kind in range(2)]
    return nsa_compress_tail(pr, pe, wc1, wc2)


def _cmp_weight_halves(wc1):
    feat = CMP_STRIDE * NSA_HD
    return jnp.concatenate([wc1[:, :feat], wc1[:, feat:]], axis=2)


def nsa_compress_tail(pr, pe, wc1, wc2):
    b, n_chunk = pr[0].shape[:2]
    feat = CMP_STRIDE * NSA_HD
    pe_flat = jnp.pad(pe.reshape(2, 1, 2 * feat), ((0, 0), (0, 7), (0, 0)))
    out = []
    for kind in range(2):
        pe_term = rms_matmul(pe_flat[kind], wc1[kind])[0]
        nxt = jnp.concatenate([pr[kind][:, 1:, CMP_HID:], jnp.zeros((b, 1, CMP_HID), F32)], axis=1)
        pre = (pr[kind][:, :, :CMP_HID] + nxt + pe_term).reshape(b * n_chunk, CMP_HID)
        out.append(rms_matmul(pre, wc2[kind], pre_act="gelu").reshape(b, n_chunk, NSA_HD))
    return out[0], out[1]


PAGES_PER_STEP = 8


def _paged_attn_kernel(pt_ref, vis_ref, cnt_ref, *refs, mode, pg, t_new, kv_of, scale, kbase, q_base, page,
                       use_slopes, n_vis):
    refs = list(refs)
    q_ref = refs.pop(0)
    slope_ref = refs.pop(0)
    sel_ref = refs.pop(0) if mode == "select" else None
    page_refs = [refs.pop(0) for _ in range(pg)]
    newk_ref, newv_ref, o_ref, m_ref, l_ref, acc_ref = refs
    bi = pl.program_id(0)
    j = pl.program_id(1)
    n_steps = pl.num_programs(1)
    rows = q_ref.shape[1]
    cnt = cnt_ref[bi]

    @pl.when(j == 0)
    def _():
        m_ref[...] = jnp.full_like(m_ref, NEG)
        l_ref[...] = jnp.zeros_like(l_ref)
        acc_ref[...] = jnp.zeros_like(acc_ref)

    qb = q_ref[0].astype(BF16)
    q_pos = q_base + lax.broadcasted_iota(jnp.int32, (rows, page), 0) % t_new
    lane = lax.broadcasted_iota(jnp.int32, (rows, page), 1)

    def scores(k, kpos0, mask, k_transposed=True):
        if k_transposed:
            s = jnp.dot(qb, k.astype(BF16), preferred_element_type=F32) * scale
        else:
            s = lax.dot_general(qb, k.astype(BF16), (((1,), (1,)), ((), ())), preferred_element_type=F32) * scale
        dist_i = q_pos - (kpos0 + lane)
        if use_slopes:
            s = s - slope_ref[:, :1] * dist_i.astype(F32)
        if mode == "window":
            wmask = dist_i < WINDOW
            mask = wmask if mask is None else mask & wmask
        if mode == "select":
            selr = sel_ref[0]
            blk = lax.broadcasted_iota(jnp.int32, selr.shape, 1)
            blk0 = kpos0 // SEL_BLOCK
            c0 = jnp.sum(jnp.where(blk == blk0, selr, 0.0), axis=1, keepdims=True)
            c1 = jnp.sum(jnp.where(blk == blk0 + 1, selr, 0.0), axis=1, keepdims=True)
            cmask = jnp.where(lane < SEL_BLOCK, c0, c1) > 0.5
            mask = cmask if mask is None else mask & cmask
        if mask is not None:
            s = jnp.where(mask, s, NEG)
        return s

    def update(s_list, v_list, v_transposed=True):
        m_old = m_ref[:, :1]
        mx = s_list[0]
        for s in s_list[1:]:
            mx = jnp.maximum(mx, s)
        m_new = jnp.maximum(m_old, jnp.max(mx, axis=-1, keepdims=True))
        alpha = jnp.exp(m_old - m_new)
        p_list = [jnp.exp(s - m_new) for s in s_list]
        ps = p_list[0]
        for p in p_list[1:]:
            ps = ps + p
        acc = alpha * acc_ref[...]
        for p, v in zip(p_list, v_list):
            if v_transposed:
                acc = acc + lax.dot_general(p.astype(BF16), v.astype(BF16), (((1,), (1,)), ((), ())),
                                            preferred_element_type=F32)
            else:
                acc = acc + jnp.dot(p.astype(BF16), v.astype(BF16), preferred_element_type=F32)
        l_ref[...] = jnp.broadcast_to(alpha * l_ref[:, :1] + jnp.sum(ps, axis=-1, keepdims=True), l_ref.shape)
        m_ref[...] = jnp.broadcast_to(m_new, m_ref.shape)
        acc_ref[...] = acc

    @pl.when(j * pg < cnt)
    def _():
        s_list, v_list = [], []
        for i in range(pg):
            slot = j * pg + i
            k, v = kv_of(page_refs[i])
            s = scores(k, kbase + vis_ref[bi * n_vis + slot] * page, None)
            if mode == "select":
                s = jnp.where(slot < cnt, s, NEG)
            s_list.append(s)
            v_list.append(v)
        update(s_list, v_list)

    @pl.when(j == n_steps - 1)
    def _():
        dist_new = q_pos - (q_base + lane)
        s = scores(newk_ref[0], q_base, (lane < t_new) & (dist_new >= 0), k_transposed=False)
        update([s], [newv_ref[0]], v_transposed=False)
        o_ref[0] = acc_ref[...] / l_ref[:, :1]


def paged_attn(q, slopes_rows, pool, kv_of, dv, page_table, new_k, new_v, *, mode, t_new, scale, kbase, q_base,
               sel_rows=None, visit=None, use_slopes=True, pg=PAGES_PER_STEP, pool_block=None):
    b, rows, dk = q.shape
    n_pages = page_table.shape[1]
    page = pool.shape[-1]
    pg = min(pg, n_pages)
    assert n_pages % pg == 0
    n_steps = n_pages // pg
    if visit is None:
        visit = (jnp.tile(jnp.arange(n_pages, dtype=jnp.int32)[None], (b, 1)), jnp.full((b,), n_pages, jnp.int32))
    vis, cnt = visit
    if pool_block is None:
        pool_block = (pool.shape[1:], (0,) * (pool.ndim - 1))
    blk = (1,) + tuple(pool_block[0])
    zeros = tuple(pool_block[1])
    in_specs = [
        pl.BlockSpec((1, rows, dk), lambda bi, j, pt, vs, ct: (bi, 0, 0)),
        pl.BlockSpec((rows, 1), lambda bi, j, pt, vs, ct: (0, 0)),
    ]
    args = [q, slopes_rows]
    if mode == "select":
        in_specs.append(pl.BlockSpec((1, rows, sel_rows.shape[2]), lambda bi, j, pt, vs, ct: (bi, 0, 0)))
        args.append(sel_rows)
    for i in range(pg):
        in_specs.append(pl.BlockSpec(
            blk, lambda bi, j, pt, vs, ct, i=i: (pt[bi * n_pages + vs[bi * n_pages + j * pg + i]],) + zeros))
        args.append(pool)
    in_specs.append(pl.BlockSpec((1, page, dk), lambda bi, j, pt, vs, ct: (bi, 0, 0)))
    in_specs.append(pl.BlockSpec((1, page, dv), lambda bi, j, pt, vs, ct: (bi, 0, 0)))
    args += [new_k, new_v]
    kern = functools.partial(_paged_attn_kernel, mode=mode, pg=pg, t_new=t_new, kv_of=kv_of, scale=scale, kbase=kbase,
                             q_base=q_base, page=page, use_slopes=use_slopes, n_vis=n_pages)
    return pl.pallas_call(
        kern,
        grid_spec=pltpu.PrefetchScalarGridSpec(
            num_scalar_prefetch=3,
            grid=(b, n_steps),
            in_specs=in_specs,
            out_specs=pl.BlockSpec((1, rows, dv), lambda bi, j, pt, vs, ct: (bi, 0, 0)),
            scratch_shapes=[
                pltpu.VMEM((rows, LANE), F32),
                pltpu.VMEM((rows, LANE), F32),
                pltpu.VMEM((rows, dv), F32),
            ],
        ),
        out_shape=jax.ShapeDtypeStruct((b, rows, dv), F32),
        compiler_params=pltpu.CompilerParams(
            dimension_semantics=("parallel", "arbitrary"), vmem_limit_bytes=VMEM_LIMIT),
        name="paged_attn_" + mode,
    )(page_table.reshape(-1), vis.reshape(-1), cnt, *args)


def _sb_paged_kernel(pt_ref, *refs, pg, t_new, kw, scale, page):
    refs = list(refs)
    q_ref = refs.pop(0)
    page_refs = [refs.pop(0) for _ in range(pg)]
    newk_ref, newv_ref, o_ref, run_ref, acc_ref = refs
    j = pl.program_id(1)
    n_steps = pl.num_programs(1)
    rows = q_ref.shape[1]
    qb = (q_ref[0] * scale).astype(BF16)
    later = (lax.broadcasted_iota(jnp.int32, (page, page), 0) >
             lax.broadcasted_iota(jnp.int32, (page, page), 1)).astype(BF16)

    def logits(k, mask, k_transposed=True):
        if k_transposed:
            z = jnp.dot(qb, k.astype(BF16), preferred_element_type=F32)
        else:
            z = lax.dot_general(qb, k.astype(BF16), (((1,), (1,)), ((), ())), preferred_element_type=F32)
        ls = jnp.minimum(z, 0.0) - jnp.log(1.0 + jnp.exp(-jnp.abs(z)))
        l1mb = ls - z
        if mask is not None:
            l1mb = jnp.where(mask, l1mb, 0.0)
        hi = l1mb.astype(BF16)
        lo = (l1mb - hi.astype(F32)).astype(BF16)
        aft = jnp.dot(hi, later, preferred_element_type=F32) + jnp.dot(lo, later, preferred_element_type=F32)
        return ls, aft, aft[:, :1] + l1mb[:, :1]

    @pl.when(j == 0)
    def _():
        mask = (lax.broadcasted_iota(jnp.int32, (rows, page), 1) <
                lax.broadcasted_iota(jnp.int32, (rows, page), 0) % t_new)
        ls, aft, tot = logits(newk_ref[0], mask, k_transposed=False)
        att = jnp.where(mask, jnp.exp(ls + aft), 0.0)
        acc_ref[...] = jnp.dot(att.astype(BF16), newv_ref[0].astype(BF16), preferred_element_type=F32)
        run_ref[...] = jnp.broadcast_to(tot, run_ref.shape)

    run = run_ref[:, :1]
    acc = acc_ref[...]
    for i in range(pg):
        pr = page_refs[i]
        ls, aft, tot = logits(pr[0, 0].reshape(kw, page), None)
        att = jnp.exp(ls + aft + run)
        acc = acc + lax.dot_general(att.astype(BF16), pr[0, 1].reshape(kw, page).astype(BF16),
                                    (((1,), (1,)), ((), ())), preferred_element_type=F32)
        run = run + tot
    run_ref[...] = jnp.broadcast_to(run, run_ref.shape)
    acc_ref[...] = acc

    @pl.when(j == n_steps - 1)
    def _():
        o_ref[0] = acc_ref[...]


def sb_paged(q_rows, pool, page_table, new_k, new_v, t_new, pg=PAGES_PER_STEP):
    b, rows, kw = q_rows.shape
    n_pages = page_table.shape[1]
    page = pool.shape[-1]
    pg = min(pg, n_pages)
    assert n_pages % pg == 0
    n_steps = n_pages // pg
    in_specs = [pl.BlockSpec((1, rows, kw), lambda bi, j, pt: (bi, 0, 0))]
    args = [q_rows]
    blk = (1,) + pool.shape[1:]
    for i in range(pg):
        in_specs.append(pl.BlockSpec(
            blk, lambda bi, j, pt, i=i: (pt[bi * n_pages + n_pages - 1 - (j * pg + i)], 0, 0, 0, 0)))
        args.append(pool)
    in_specs.append(pl.BlockSpec((1, page, kw), lambda bi, j, pt: (bi, 0, 0)))
    in_specs.append(pl.BlockSpec((1, page, kw), lambda bi, j, pt: (bi, 0, 0)))
    args += [new_k, new_v]
    kern = functools.partial(_sb_paged_kernel, pg=pg, t_new=t_new, kw=kw, scale=SB_HD ** -0.5, page=page)
    return pl.pallas_call(
        kern,
        grid_spec=pltpu.PrefetchScalarGridSpec(
            num_scalar_prefetch=1,
            grid=(b, n_steps),
            in_specs=in_specs,
            out_specs=pl.BlockSpec((1, rows, kw), lambda bi, j, pt: (bi, 0, 0)),
            scratch_shapes=[pltpu.VMEM((rows, LANE), F32), pltpu.VMEM((rows, kw), F32)],
        ),
        out_shape=jax.ShapeDtypeStruct((b, rows, kw), F32),
        compiler_params=pltpu.CompilerParams(
            dimension_semantics=("parallel", "arbitrary"), vmem_limit_bytes=VMEM_LIMIT),
        name="sb_paged",
    )(page_table.reshape(-1), *args)


def _chunk_proj_kernel(pt_ref, *refs, pg, rows_per_chunk, kinds):
    page_refs = refs[:pg]
    w_ref, o_ref, rows_ref = refs[pg:]
    page = page_refs[0].shape[3]
    n_c = page // rows_per_chunk
    for kind in range(kinds):
        for i, pr in enumerate(page_refs):
            rows_ref[kind, i * page:(i + 1) * page, :] = pr[0, kind].T
        acc = jnp.zeros((pg * n_c, w_ref.shape[3]), F32)
        for r in range(rows_per_chunk):
            a = rows_ref[kind, pl.ds(r, pg * n_c, stride=rows_per_chunk), :]
            acc = acc + jnp.dot(a.astype(BF16), w_ref[kind, r], preferred_element_type=F32)
        o_ref[0, kind] = acc


def nsa_chunk_proj(pool, page_table, w_rows, pg=PAGES_PER_STEP):
    b, n_pages = page_table.shape
    hd, page = pool.shape[2:]
    n_c = page // CMP_STRIDE
    m = w_rows.shape[3]
    pg = min(pg, n_pages)
    assert n_pages % pg == 0
    blk = (1, 2, hd, page)
    in_specs = [pl.BlockSpec(blk, lambda bi, j, pt, i=i: (pt[bi * n_pages + j * pg + i], 0, 0, 0)) for i in range(pg)]
    in_specs.append(pl.BlockSpec(w_rows.shape, lambda bi, j, pt: (0, 0, 0, 0)))
    return pl.pallas_call(
        functools.partial(_chunk_proj_kernel, pg=pg, rows_per_chunk=CMP_STRIDE, kinds=2),
        grid_spec=pltpu.PrefetchScalarGridSpec(
            num_scalar_prefetch=1,
            grid=(b, n_pages // pg),
            in_specs=in_specs,
            out_specs=pl.BlockSpec((1, 2, pg * n_c, m), lambda bi, j, pt: (bi, 0, j, 0)),
            scratch_shapes=[pltpu.VMEM((2, pg * page, hd), F32)],
        ),
        out_shape=jax.ShapeDtypeStruct((b, 2, n_pages * n_c, m), F32),
        compiler_params=pltpu.CompilerParams(
            dimension_semantics=("parallel", "arbitrary"), vmem_limit_bytes=VMEM_LIMIT),
        name="nsa_chunk_proj",
    )(page_table.reshape(-1), *([pool] * pg), w_rows)


def _old_paged_mqa_kernel(pt_ref, *refs, mode, pg, t_new, dk, dv, k_off, v_off, scale, kbase, q_base, page, use_slopes):
    refs = list(refs)
    q_ref = refs.pop(0)
    slope_ref = refs.pop(0)
    sel_ref = refs.pop(0) if mode == "select" else None
    page_refs = [refs.pop(0) for _ in range(pg)]
    new_ref, o_ref, m_ref, l_ref, acc_ref = refs
    j = pl.program_id(1)
    n_steps = pl.num_programs(1)
    rows = q_ref.shape[1]

    @pl.when(j == 0)
    def _():
        m_ref[...] = jnp.full_like(m_ref, NEG)
        l_ref[...] = jnp.zeros_like(l_ref)
        acc_ref[...] = jnp.zeros_like(acc_ref)

    qb = q_ref[0].astype(BF16)
    q_pos = q_base + lax.broadcasted_iota(jnp.int32, (rows, page), 0) % t_new
    lane = lax.broadcasted_iota(jnp.int32, (rows, page), 1)

    def tile(kv, kpos0, is_new, carry):
        m_old, l_old, acc = carry
        kt = kv[:, k_off:k_off + dk].astype(BF16)
        vt = kv[:, v_off:v_off + dv].astype(BF16)
        s = lax.dot_general(qb, kt, (((1,), (1,)), ((), ())), preferred_element_type=F32) * scale
        dist_i = q_pos - (kpos0 + lane)
        if use_slopes:
            s = s - slope_ref[:, :1] * dist_i.astype(F32)
        mask = None
        if is_new:
            mask = (lane < t_new) & (dist_i >= 0)
        if mode == "window":
            wmask = dist_i < WINDOW
            mask = wmask if mask is None else mask & wmask
        if mode == "select":
            selr = sel_ref[0]
            blk = lax.broadcasted_iota(jnp.int32, selr.shape, 1)
            blk0 = kpos0 // SEL_BLOCK
            c0 = jnp.sum(jnp.where(blk == blk0, selr, 0.0), axis=1, keepdims=True)
            c1 = jnp.sum(jnp.where(blk == blk0 + 1, selr, 0.0), axis=1, keepdims=True)
            cmask = jnp.where(lane < SEL_BLOCK, c0, c1) > 0.5
            mask = cmask if mask is None else mask & cmask
        if mask is not None:
            s = jnp.where(mask, s, NEG)
        m_new = jnp.maximum(m_old, jnp.max(s, axis=-1, keepdims=True))
        alpha = jnp.exp(m_old - m_new)
        p = jnp.exp(s - m_new)
        l_new = alpha * l_old + jnp.sum(p, axis=-1, keepdims=True)
        acc = alpha * acc + jnp.dot(p.astype(BF16), vt, preferred_element_type=F32)
        return m_new, l_new, acc

    carry = (m_ref[:, :1], l_ref[:, :1], acc_ref[...])
    for i in range(pg):
        carry = tile(page_refs[i][0], kbase + (j * pg + i) * page, False, carry)
    m_ref[...] = jnp.broadcast_to(carry[0], m_ref.shape)
    l_ref[...] = jnp.broadcast_to(carry[1], l_ref.shape)
    acc_ref[...] = carry[2]

    @pl.when(j == n_steps - 1)
    def _():
        m_f, l_f, acc_f = tile(new_ref[0], q_base, True, (m_ref[:, :1], l_ref[:, :1], acc_ref[...]))
        o_ref[0] = acc_f / l_f


def paged_mqa(q, slopes_rows, pool, lane_blk, kv_w, page_table, new_tile, *, mode, t_new, dk, dv, k_off, v_off,
              scale, kbase, q_base, sel_rows=None, use_slopes=True):
    b, rows, _ = q.shape
    n_pages = page_table.shape[1]
    page = pool.shape[1]
    pg = min(PAGES_PER_STEP, n_pages)
    assert n_pages % pg == 0
    n_steps = n_pages // pg
    in_specs = [
        pl.BlockSpec((1, rows, dk), lambda bi, j, pt: (bi, 0, 0)),
        pl.BlockSpec((rows, 1), lambda bi, j, pt: (0, 0)),
    ]
    args = [q, slopes_rows]
    if mode == "select":
        in_specs.append(pl.BlockSpec((1, rows, sel_rows.shape[2]), lambda bi, j, pt: (bi, 0, 0)))
        args.append(sel_rows)
    for i in range(pg):
        in_specs.append(pl.BlockSpec(
            (1, page, kv_w), lambda bi, j, pt, i=i: (pt[bi * n_pages + j * pg + i], 0, lane_blk)))
        args.append(pool)
    in_specs.append(pl.BlockSpec((1, page, kv_w), lambda bi, j, pt: (bi, 0, 0)))
    args.append(new_tile)
    kern = functools.partial(_paged_mqa_kernel, mode=mode, pg=pg, t_new=t_new, dk=dk, dv=dv, k_off=k_off, v_off=v_off,
                             scale=scale, kbase=kbase, q_base=q_base, page=page, use_slopes=use_slopes)
    return pl.pallas_call(
        kern,
        grid_spec=pltpu.PrefetchScalarGridSpec(
            num_scalar_prefetch=1,
            grid=(b, n_steps),
            in_specs=in_specs,
            out_specs=pl.BlockSpec((1, rows, dv), lambda bi, j, pt: (bi, 0, 0)),
            scratch_shapes=[
                pltpu.VMEM((rows, LANE), F32),
                pltpu.VMEM((rows, LANE), F32),
                pltpu.VMEM((rows, dv), F32),
            ],
        ),
        out_shape=jax.ShapeDtypeStruct((b, rows, dv), F32),
        compiler_params=pltpu.CompilerParams(
            dimension_semantics=("parallel", "arbitrary"), vmem_limit_bytes=VMEM_LIMIT),
        name="paged_mqa_" + mode,
    )(page_table.reshape(-1), *args)


def _sb_decode_kernel(pt_ref, *refs, pg, t_new, kw, scale, page):
    refs = list(refs)
    q_ref = refs.pop(0)
    page_refs = [refs.pop(0) for _ in range(pg)]
    new_ref, o_ref, run_ref, acc_ref = refs
    j = pl.program_id(1)
    n_steps = pl.num_programs(1)
    rows = q_ref.shape[1]
    qb = (q_ref[0] * scale).astype(BF16)
    later = (lax.broadcasted_iota(jnp.int32, (page, page), 0) >
             lax.broadcasted_iota(jnp.int32, (page, page), 1)).astype(BF16)

    def tile(kv, is_new, carry):
        run, acc = carry
        kt = kv[:, :kw].astype(BF16)
        vt = kv[:, kw:2 * kw].astype(BF16)
        z = lax.dot_general(qb, kt, (((1,), (1,)), ((), ())), preferred_element_type=F32)
        ls = jnp.minimum(z, 0.0) - jnp.log(1.0 + jnp.exp(-jnp.abs(z)))
        l1mb = ls - z
        if is_new:
            mask = (lax.broadcasted_iota(jnp.int32, (rows, page), 1) <
                    lax.broadcasted_iota(jnp.int32, (rows, page), 0) % t_new)
            l1mb = jnp.where(mask, l1mb, 0.0)
        hi = l1mb.astype(BF16)
        lo = (l1mb - hi.astype(F32)).astype(BF16)
        aft = jnp.dot(hi, later, preferred_element_type=F32) + jnp.dot(lo, later, preferred_element_type=F32)
        att = jnp.exp(ls + aft + run)
        if is_new:
            att = jnp.where(mask, att, 0.0)
        acc = acc + jnp.dot(att.astype(BF16), vt, preferred_element_type=F32)
        return run + aft[:, :1] + l1mb[:, :1], acc

    @pl.when(j == 0)
    def _():
        run0, acc0 = tile(new_ref[0], True, (jnp.zeros((rows, 1), F32), jnp.zeros((rows, kw), F32)))
        run_ref[...] = jnp.broadcast_to(run0, run_ref.shape)
        acc_ref[...] = acc0

    carry = (run_ref[:, :1], acc_ref[...])
    for i in range(pg):
        carry = tile(page_refs[i][0], False, carry)
    run_ref[...] = jnp.broadcast_to(carry[0], run_ref.shape)
    acc_ref[...] = carry[1]

    @pl.when(j == n_steps - 1)
    def _():
        o_ref[0] = acc_ref[...]


def sb_decode(q_rows, pool, page_table, new_tile, t_new):
    b, rows, kw = q_rows.shape
    n_pages = page_table.shape[1]
    page = pool.shape[1]
    pg = min(PAGES_PER_STEP, n_pages)
    assert n_pages % pg == 0
    n_steps = n_pages // pg
    in_specs = [pl.BlockSpec((1, rows, kw), lambda bi, j, pt: (bi, 0, 0))]
    args = [q_rows]
    for i in range(pg):
        in_specs.append(pl.BlockSpec(
            (1, page, 2 * kw), lambda bi, j, pt, i=i: (pt[bi * n_pages + n_pages - 1 - (j * pg + i)], 0, 0)))
        args.append(pool)
    in_specs.append(pl.BlockSpec((1, page, 2 * kw), lambda bi, j, pt: (bi, 0, 0)))
    args.append(new_tile)
    kern = functools.partial(_sb_decode_kernel, pg=pg, t_new=t_new, kw=kw, scale=SB_HD ** -0.5, page=page)
    return pl.pallas_call(
        kern,
        grid_spec=pltpu.PrefetchScalarGridSpec(
            num_scalar_prefetch=1,
            grid=(b, n_steps),
            in_specs=in_specs,
            out_specs=pl.BlockSpec((1, rows, kw), lambda bi, j, pt: (bi, 0, 0)),
            scratch_shapes=[pltpu.VMEM((rows, LANE), F32), pltpu.VMEM((rows, kw), F32)],
        ),
        out_shape=jax.ShapeDtypeStruct((b, rows, kw), F32),
        compiler_params=pltpu.CompilerParams(
            dimension_semantics=("parallel", "arbitrary"), vmem_limit_bytes=VMEM_LIMIT),
        name="sb_decode",
    )(page_table.reshape(-1), *args)


def _nsa_cmp_decode_kernel(q_ref, slope_ref, kc_ref, vc_ref, oc_ref, sel_ref, *, t_new, heads, scale, q_base, n_sel, k_top):
    rows = q_ref.shape[1]
    nc = kc_ref.shape[1]
    lsel = sel_ref.shape[2]
    pos = q_base + lax.broadcasted_iota(jnp.int32, (rows, nc), 0) % t_new
    cmp_end = CMP_STRIDE * lax.broadcasted_iota(jnp.int32, (rows, nc), 1) + (CMP_LEN - 1)
    valid = cmp_end <= pos
    dist = (pos - cmp_end).astype(F32)
    qb = (q_ref[0] * scale).astype(BF16)
    kc = kc_ref[0].astype(BF16)
    vc = vc_ref[0].astype(BF16)
    s = lax.dot_general(qb, kc, (((1,), (1,)), ((), ())), preferred_element_type=F32) - slope_ref[:, :1] * dist
    s = jnp.where(valid, s, NEG)
    e = jnp.exp(s - jnp.max(s, axis=-1, keepdims=True))
    p = jnp.where(valid, e / jnp.sum(e, axis=-1, keepdims=True), 0.0)
    oc_ref[0] = jnp.dot(p.astype(BF16), vc, preferred_element_type=F32)
    psum = p[0:t_new]
    for h in range(1, heads):
        psum = psum + p[h * t_new:(h + 1) * t_new]
    sel_ref[0] = _select_blocks(psum, q_base, 0, t_new, nc, lsel, n_sel, k_top)


def _select_blocks(psum, pos0, row0, tq, nc, lsel, n_sel, k_top):
    c_start = CMP_STRIDE * lax.broadcasted_iota(jnp.int32, (nc, lsel), 0)
    s_start = SEL_BLOCK * lax.broadcasted_iota(jnp.int32, (nc, lsel), 1)
    overlap = ((c_start < s_start + SEL_BLOCK) & (c_start + CMP_LEN > s_start)).astype(F32)
    imp = jnp.dot(psum, overlap, preferred_element_type=F32, precision=lax.Precision.HIGHEST)
    tpos = pos0 + row0 + lax.broadcasted_iota(jnp.int32, (tq, lsel), 0)
    j = lax.broadcasted_iota(jnp.int32, (tq, lsel), 1)
    cur = tpos // SEL_BLOCK
    forced = (j == 0) | (j == cur) | (j == cur - 1)
    cand = (SEL_BLOCK * j <= tpos) & (j < n_sel)
    score = jnp.where(cand, imp + jnp.where(forced, FORCE_BONUS, 0.0), NEG)
    work = score
    chosen = jnp.zeros((tq, lsel), F32)
    for _ in range(k_top):
        mx = jnp.max(work, axis=-1, keepdims=True)
        first = jnp.min(jnp.where(work == mx, j, lsel), axis=-1, keepdims=True)
        hit = j == first
        chosen = jnp.where(hit, 1.0, chosen)
        work = jnp.where(hit, -3.0e38, work)
    return jnp.where(score > 0.5 * NEG, chosen, 0.0)


def nsa_cmp_decode(q_rows, slopes_rows, k_cmp, v_cmp, t_new, q_base, n_sel):
    b, rows, hd = q_rows.shape
    nc = k_cmp.shape[1]
    lsel = -(-n_sel // LANE) * LANE
    kern = functools.partial(_nsa_cmp_decode_kernel, t_new=t_new, heads=rows // t_new, scale=hd ** -0.5,
                             q_base=q_base, n_sel=n_sel, k_top=min(N_SEL, n_sel))
    return pl.pallas_call(
        kern,
        grid=(b,),
        in_specs=[
            pl.BlockSpec((1, rows, hd), lambda bi: (bi, 0, 0)),
            pl.BlockSpec((rows, 1), lambda bi: (0, 0)),
            pl.BlockSpec((1, nc, hd), lambda bi: (bi, 0, 0)),
            pl.BlockSpec((1, nc, hd), lambda bi: (bi, 0, 0)),
        ],
        out_specs=[
            pl.BlockSpec((1, rows, hd), lambda bi: (bi, 0, 0)),
            pl.BlockSpec((1, t_new, lsel), lambda bi: (bi, 0, 0)),
        ],
        out_shape=[jax.ShapeDtypeStruct((b, rows, hd), F32), jax.ShapeDtypeStruct((b, t_new, lsel), F32)],
        compiler_params=pltpu.CompilerParams(dimension_semantics=("parallel",), vmem_limit_bytes=VMEM_LIMIT),
        name="nsa_cmp_decode",
    )(q_rows, slopes_rows, k_cmp, v_cmp)


def _router_kernel(x_ref, g_ref, w_ref, b_ref, lg_ref, h_ref):
    x = x_ref[...]
    ms = jnp.mean(x * x, axis=-1, keepdims=True)
    h = (x * lax.rsqrt(ms + EPS)) * g_ref[...]
    h_ref[...] = h.astype(BF16)
    lg_ref[...] = jnp.dot(h, w_ref[...], preferred_element_type=F32,
                          precision=lax.Precision.HIGHEST) + b_ref[...]


def _experts_kernel(te_ref, tv_ref, xs_ref, sw_ref, w1_ref, w3_ref, w2_ref, o_ref, b1_ref, b3_ref, b2_ref):
    i = pl.program_id(0)

    @pl.when((i == 0) | (te_ref[i] != te_ref[jnp.maximum(i - 1, 0)]))
    def _():
        b1_ref[...] = w1_ref[0].astype(BF16)
        b3_ref[...] = w3_ref[0].astype(BF16)
        b2_ref[...] = w2_ref[0].astype(BF16)

    @pl.when(tv_ref[i] != 0)
    def _():
        x = xs_ref[...]
        a = jnp.dot(x, b1_ref[...], preferred_element_type=F32)
        c = jnp.dot(x, b3_ref[...], preferred_element_type=F32)
        hid = (jax.nn.silu(a) * c) * sw_ref[...]
        o_ref[...] = jnp.dot(hid.astype(BF16), b2_ref[...], preferred_element_type=F32)

    @pl.when(tv_ref[i] == 0)
    def _():
        o_ref[...] = jnp.zeros_like(o_ref)


def moe(x, g, w_rg, b_rg, w_re, b_re, w_e1, w_e3, w_e2):
    n, d = x.shape
    ne = N_GROUPS * N_EXP
    tm = _pick_tile(n, (ROW_TILE, 256, 128))
    wr = jnp.zeros((d, LANE), F32).at[:, :N_GROUPS].set(w_rg).at[:, N_GROUPS:N_GROUPS + ne].set(w_re)
    br = jnp.zeros((1, LANE), F32).at[0, :N_GROUPS].set(b_rg).at[0, N_GROUPS:N_GROUPS + ne].set(b_re)
    logits, h = pl.pallas_call(
        _router_kernel,
        grid=(n // tm,),
        in_specs=[
            pl.BlockSpec((tm, d), lambda i: (i, 0)),
            pl.BlockSpec((1, d), lambda i: (0, 0)),
            pl.BlockSpec((d, LANE), lambda i: (0, 0)),
            pl.BlockSpec((1, LANE), lambda i: (0, 0)),
        ],
        out_specs=[pl.BlockSpec((tm, LANE), lambda i: (i, 0)), pl.BlockSpec((tm, d), lambda i: (i, 0))],
        out_shape=[jax.ShapeDtypeStruct((n, LANE), F32), jax.ShapeDtypeStruct((n, d), BF16)],
        compiler_params=pltpu.CompilerParams(dimension_semantics=("parallel",), vmem_limit_bytes=VMEM_LIMIT),
        name="moe_router",
    )(x, g.astype(F32).reshape(1, d), wr, br)

    lg = logits[:, :N_GROUPS]
    le = logits[:, N_GROUPS:N_GROUPS + ne].reshape(n, N_GROUPS, N_EXP)
    pg = jax.nn.softmax(lg, axis=-1)
    gi = jnp.argmax(lg, axis=-1)
    gw = jnp.take_along_axis(pg, gi[:, None], axis=1)[:, 0]
    le_g = jnp.take_along_axis(le, gi[:, None, None], axis=1)[:, 0]
    top_v, top_i = lax.top_k(le_g, TOP_K)
    we = jax.nn.softmax(top_v, axis=-1)
    eid = (gi[:, None] * N_EXP + top_i).astype(jnp.int32)
    wt = we * gw[:, None]

    ts = EXPERT_TILE
    n_tiles = -(-(TOP_K * n) // ts) + ne
    member = (eid[:, :, None] == jnp.arange(ne)[None, None, :]).any(axis=1).astype(jnp.int32)
    cnt = member.sum(axis=0)
    rank = jnp.cumsum(member, axis=0) - member
    tiles_e = (cnt + ts - 1) // ts
    tile_end = jnp.cumsum(tiles_e)
    pad_off = (tile_end - tiles_e) * ts
    pos = pad_off[eid] + jnp.take_along_axis(rank, eid, axis=1)
    tok = jnp.broadcast_to(jnp.arange(n, dtype=jnp.int32)[:, None], (n, TOP_K))
    slot_tok = jnp.zeros((n_tiles * ts,), jnp.int32).at[pos.reshape(-1)].set(tok.reshape(-1))
    slot_w = jnp.zeros((n_tiles * ts,), F32).at[pos.reshape(-1)].set(wt.reshape(-1))
    tile_id = jnp.arange(n_tiles, dtype=jnp.int32)
    tile_e = jnp.minimum(jnp.searchsorted(tile_end, tile_id, side="right"), ne - 1).astype(jnp.int32)
    tile_v = (tile_id < tile_end[-1]).astype(jnp.int32)

    xs = jnp.take(h, slot_tok, axis=0)
    w1 = w_e1.reshape(ne, d, D_EXP)
    w3 = w_e3.reshape(ne, d, D_EXP)
    w2 = w_e2.reshape(ne, D_EXP, d)
    ys = pl.pallas_call(
        _experts_kernel,
        grid_spec=pltpu.PrefetchScalarGridSpec(
            num_scalar_prefetch=2,
            grid=(n_tiles,),
            in_specs=[
                pl.BlockSpec((ts, d), lambda i, te, tv: (i, 0)),
                pl.BlockSpec((ts, 1), lambda i, te, tv: (i, 0)),
                pl.BlockSpec((1, d, D_EXP), lambda i, te, tv: (te[i], 0, 0)),
                pl.BlockSpec((1, d, D_EXP), lambda i, te, tv: (te[i], 0, 0)),
                pl.BlockSpec((1, D_EXP, d), lambda i, te, tv: (te[i], 0, 0)),
            ],
            out_specs=pl.BlockSpec((ts, d), lambda i, te, tv: (i, 0)),
            scratch_shapes=[pltpu.VMEM((d, D_EXP), BF16), pltpu.VMEM((d, D_EXP), BF16), pltpu.VMEM((D_EXP, d), BF16)],
        ),
        out_shape=jax.ShapeDtypeStruct((n_tiles * ts, d), F32),
        compiler_params=pltpu.CompilerParams(dimension_semantics=("arbitrary",), vmem_limit_bytes=VMEM_LIMIT),
        name="moe_experts",
    )(tile_e, tile_v, xs, slot_w.reshape(-1, 1), w1, w3, w2)
    return jnp.take(ys, pos[:, 0], axis=0) + jnp.take(ys, pos[:, 1], axis=0)


def rmsnorm(x, g):
    xf = x.astype(F32)
    y = xf * lax.rsqrt(jnp.mean(xf * xf, axis=-1, keepdims=True) + EPS)
    return (y * g.astype(F32)).astype(x.dtype)


def split_cols(a, widths):
    return jnp.split(a, np.cumsum(widths)[:-1].tolist(), axis=-1)


def qblock(t):
    return min(QBLOCK, t)


def map_query_blocks(fn, arrays, block):
    t = arrays[0].shape[1]
    nb = -(-t // block)
    pad = nb * block - t

    def to_blocks(a):
        a = jnp.pad(a, [(0, 0), (0, pad)] + [(0, 0)] * (a.ndim - 2))
        a = a.reshape((a.shape[0], nb, block) + a.shape[2:])
        return jnp.moveaxis(a, 1, 0)

    xs = (jnp.arange(nb),) + tuple(to_blocks(a) for a in arrays)
    out = lax.map(lambda args: fn(*args), xs)
    out = jnp.moveaxis(out, 0, 1)
    out = out.reshape((out.shape[0], nb * block) + out.shape[3:])
    return out[:, :t]


def rope_angles(pos, dim):
    inv = ROPE_THETA ** (-jnp.arange(0, dim, 2, dtype=F32) / dim)
    ang = pos.astype(F32)[:, None] * inv[None, :]
    return jnp.cos(ang), jnp.sin(ang)


def apply_rope(x, cos, sin):
    half = x.shape[-1] // 2
    x1 = x[..., :half].astype(F32)
    x2 = x[..., half:].astype(F32)
    return jnp.concatenate([x1 * cos - x2 * sin, x1 * sin + x2 * cos], axis=-1).astype(x.dtype)


def alibi_slopes(n):
    return 2.0 ** (-8.0 * jnp.arange(1, n + 1, dtype=F32) / n)


def gather_pages(pool, page_table):
    g = pool[page_table]
    return g.reshape((page_table.shape[0], page_table.shape[1] * pool.shape[1]) + pool.shape[2:])


def mla_mixer(c_q, c_kv_raw, k_r_raw, pos, past, g_cq, w_uq, g_ckv, w_uk, w_uv):
    b, t, _ = c_q.shape
    cos, sin = rope_angles(pos, ROPE_DIM)
    q = jnp.einsum('btr,rhd->bthd', rmsnorm(c_q, g_cq), w_uq)
    q_nope = q[..., :NOPE_DIM]
    q_rope = apply_rope(q[..., NOPE_DIM:], cos[None, :, None, :], sin[None, :, None, :])
    q_lat = jnp.einsum('bthd,rhd->bthr', q_nope, w_uk)
    c_kv = rmsnorm(c_kv_raw, g_ckv)
    k_r = apply_rope(k_r_raw, cos[None], sin[None])
    new_rows = jnp.concatenate([c_kv, k_r], axis=-1)
    keys = jnp.concatenate([past, new_rows], axis=1)
    k_pos = jnp.arange(keys.shape[1])
    lat, kr = keys[..., :KV_RANK], keys[..., KV_RANK:]
    scale = (NOPE_DIM + ROPE_DIM) ** -0.5

    def blk(bi, ql, qr, qp):
        s = (jnp.einsum('bqhr,bsr->bhqs', ql, lat) + jnp.einsum('bqhd,bsd->bhqs', qr, kr)).astype(F32) * scale
        mask = k_pos[None, :] <= qp[0][:, None]
        p = jax.nn.softmax(jnp.where(mask, s, NEG), axis=-1).astype(lat.dtype)
        return jnp.einsum('bhqs,bsr->bqhr', p, lat)

    o_lat = map_query_blocks(blk, (q_lat, q_rope, pos[None]), qblock(t))
    out = jnp.einsum('bthr,rhd->bthd', o_lat, w_uv).reshape(b, t, MLA_HEADS * MLA_VDIM)
    return out, new_rows


def mlstm_mixer(q, k, v, ig, fg, og, c0, n0, m0, g_mh):
    b, t = q.shape[:2]
    dt = q.dtype
    q = q.reshape(b, t, ML_HEADS, ML_QK)
    k = k.reshape(b, t, ML_HEADS, ML_QK) * (ML_QK ** -0.5)
    v = v.reshape(b, t, ML_HEADS, ML_V)
    li = ig.astype(F32)
    lf = jax.nn.log_sigmoid(fg.astype(F32))
    L = min(ML_CHUNK, t)
    nc = -(-t // L)
    pad = nc * L - t

    def chunks(a, val=0.0):
        a = jnp.pad(a, [(0, 0), (0, pad)] + [(0, 0)] * (a.ndim - 2), constant_values=val)
        return jnp.moveaxis(a.reshape((b, nc, L) + a.shape[2:]), 1, 0)

    tri = jnp.tril(jnp.ones((L, L), dtype=bool))

    def step(carry, xs):
        c, n, m = carry
        qc, kc, vc, lic, lfc = xs
        cb = jnp.cumsum(lfc, axis=1)
        a = cb + m[:, None, :]
        d = cb[:, :, None, :] - cb[:, None, :, :] + lic[:, None, :, :]
        d = jnp.where(tri[None, :, :, None], d, NEG)
        mt = jnp.maximum(a, d.max(axis=2))
        w = jnp.exp(d - mt[:, :, None, :])
        inter = jnp.exp(a - mt)
        sc = w * jnp.einsum('bthd,bshd->btsh', qc, kc).astype(F32)
        num = jnp.einsum('btsh,bshv->bthv', sc, vc) + inter[..., None] * jnp.einsum('bhvd,bthd->bthv', c, qc)
        den = sc.sum(axis=2) + inter * jnp.einsum('bhd,bthd->bth', n, qc)
        h = num / jnp.maximum(jnp.abs(den), jnp.exp(-mt))[..., None]
        wl, il = w[:, -1], inter[:, -1]
        c = il[..., None, None] * c + jnp.einsum('bsh,bshv,bshd->bhvd', wl, vc, kc)
        n = il[..., None] * n + jnp.einsum('bsh,bshd->bhd', wl, kc)
        return (c, n, mt[:, -1]), h

    carry0 = (c0.astype(F32), n0.astype(F32), m0.astype(F32))
    (c, n, m), h = lax.scan(step, carry0, (chunks(q), chunks(k), chunks(v), chunks(li, NEG), chunks(lf)))
    h = jnp.moveaxis(h, 0, 1).reshape(b, nc * L, ML_HEADS, ML_V)[:, :t]
    h = h - h.mean(axis=-1, keepdims=True)
    h = h * lax.rsqrt(jnp.mean(h * h, axis=-1, keepdims=True) + EPS) * g_mh.astype(F32)
    out = jax.nn.sigmoid(og.astype(F32)).reshape(b, t, ML_HEADS, ML_V) * h
    return out.reshape(b, t, ML_HEADS * ML_V).astype(dt), c, n, m


def sb_mixer(q, k_new, v_new, pos, past_kv):
    b, t = q.shape[:2]
    q = q.reshape(b, t, SB_KV_HEADS, SB_HEADS // SB_KV_HEADS, SB_HD)
    new_rows = jnp.stack([k_new.reshape(b, t, SB_KV_HEADS, SB_HD), v_new.reshape(b, t, SB_KV_HEADS, SB_HD)], axis=2)
    kv = jnp.concatenate([past_kv, new_rows], axis=1)
    k, v = kv[:, :, 0], kv[:, :, 1]
    k_pos = jnp.arange(kv.shape[1])

    def blk(bi, qh, qp):
        z = jnp.einsum('bqgrd,bsgd->bgrqs', qh, k).astype(F32) * (SB_HD ** -0.5)
        mask = k_pos[None, :] < qp[0][:, None]
        l1mb = jnp.where(mask, jax.nn.log_sigmoid(-z), 0.0)
        after = lax.cumsum(l1mb, axis=z.ndim - 1, reverse=True) - l1mb
        att = jnp.where(mask, jnp.exp(jax.nn.log_sigmoid(z) + after), 0.0).astype(v.dtype)
        return jnp.einsum('bgrqs,bsgd->bqgrd', att, v)

    o = map_query_blocks(blk, (q, pos[None]), qblock(t))
    return o.reshape(b, t, SB_HEADS * SB_HD), new_rows


def nsa_mixer(q, kv_new, win_new, gate, pos, past_kv, win_prior, pe, wc1, wc2):
    b, t = q.shape[:2]
    past = past_kv.shape[1]
    s_len = past + t
    scale = NSA_HD ** -0.5
    slopes = alibi_slopes(NSA_HEADS)
    kv_all = jnp.concatenate([past_kv, kv_new], axis=1)
    n_chunk = max(-(-s_len // CMP_STRIDE), 2)
    ck = jnp.pad(kv_all[:, :, :2], ((0, 0), (0, n_chunk * CMP_STRIDE - s_len), (0, 0), (0, 0)))
    ck = ck.reshape(b, n_chunk, CMP_STRIDE, 2, NSA_HD)
    w1 = wc1.reshape(2, 2, CMP_STRIDE, NSA_HD, CMP_HID)
    proj = jnp.einsum('bcrkd,kzrdh->bczkh', ck, w1)
    pe_term = jnp.einsum('kpd,kpdh->kh', pe, wc1.reshape(2, CMP_LEN, NSA_HD, CMP_HID))
    hid = jax.nn.gelu(proj[:, :-1, 0] + proj[:, 1:, 1] + pe_term)
    cmp = jnp.einsum('bnkh,khd->bnkd', hid, wc2)
    k_cmp, v_cmp = cmp[:, :, 0], cmp[:, :, 1]
    cmp_start = CMP_STRIDE * jnp.arange(n_chunk - 1)
    cmp_end = cmp_start + CMP_LEN - 1
    n_sel = -(-s_len // SEL_BLOCK)
    sel = jnp.pad(kv_all[:, :, 2:], ((0, 0), (0, n_sel * SEL_BLOCK - s_len), (0, 0), (0, 0)))
    sel = sel.reshape(b, n_sel, SEL_BLOCK, 2, NSA_HD)
    sel_start = SEL_BLOCK * jnp.arange(n_sel)
    overlap = ((cmp_start[:, None] < sel_start[None, :] + SEL_BLOCK) & (cmp_start[:, None] + CMP_LEN > sel_start[None, :])).astype(F32)
    k_top = min(N_SEL, n_sel)
    bidx = jnp.arange(b)[:, None, None]
    pw = win_prior.shape[1]
    qb = qblock(t)
    nb = -(-t // qb)
    band = jnp.concatenate([jnp.zeros((b, WINDOW - pw, 2, NSA_HD), win_new.dtype), win_prior, win_new, jnp.zeros((b, nb * qb - t, 2, NSA_HD), win_new.dtype)], axis=1)
    band_pos = past - WINDOW + jnp.arange(WINDOW + nb * qb)

    def blk(bi, qh, gh, qp):
        tq = qp[0]
        valid_c = cmp_end[None, :] <= tq[:, None]
        dist_c = (tq[:, None] - cmp_end[None, :]).astype(F32)
        s_c = jnp.einsum('bqhd,bnd->bhqn', qh, k_cmp).astype(F32) * scale - slopes[:, None, None] * dist_c
        p_c = jnp.where(valid_c, jax.nn.softmax(jnp.where(valid_c, s_c, NEG), axis=-1), 0.0)
        o_c = jnp.einsum('bhqn,bnd->bqhd', p_c.astype(v_cmp.dtype), v_cmp)
        imp = jnp.einsum('bhqn,nj->bqj', p_c, overlap)
        cur = tq // SEL_BLOCK
        j = jnp.arange(n_sel)
        forced = (j[None, :] == 0) | (j[None, :] == cur[:, None]) | (j[None, :] == cur[:, None] - 1)
        cand = sel_start[None, :] <= tq[:, None]
        score = jnp.where(cand[None], imp + jnp.where(forced, FORCE_BONUS, 0.0)[None], NEG)
        top_s, idx = lax.top_k(score, k_top)
        kv_sel = sel[bidx, idx]
        kpos = idx[..., None] * SEL_BLOCK + jnp.arange(SEL_BLOCK)
        ok = (top_s > 0.5 * NEG)[..., None] & (kpos <= tq[None, :, None, None])
        dist_s = (tq[None, :, None, None] - kpos).astype(F32)
        s_s = jnp.einsum('bqhd,bqkld->bqhkl', qh, kv_sel[..., 0, :]).astype(F32) * scale - slopes[None, None, :, None, None] * dist_s[:, :, None]
        s_s = jnp.where(ok[:, :, None], s_s, NEG)
        p_s = jax.nn.softmax(s_s.reshape(s_s.shape[:3] + (-1,)), axis=-1).reshape(s_s.shape)
        o_s = jnp.einsum('bqhkl,bqkld->bqhd', p_s.astype(kv_sel.dtype), kv_sel[..., 1, :])
        bw = lax.dynamic_slice_in_dim(band, bi * qb, qb + WINDOW, axis=1)
        bp = lax.dynamic_slice_in_dim(band_pos, bi * qb, qb + WINDOW, axis=0)
        dist_w = tq[:, None] - bp[None, :]
        ok_w = (bp[None, :] >= 0) & (dist_w >= 0) & (dist_w < WINDOW)
        s_w = jnp.einsum('bqhd,bsd->bhqs', qh, bw[:, :, 0]).astype(F32) * scale - slopes[:, None, None] * dist_w.astype(F32)
        p_w = jax.nn.softmax(jnp.where(ok_w, s_w, NEG), axis=-1)
        o_w = jnp.einsum('bhqs,bsd->bqhd', p_w.astype(bw.dtype), bw[:, :, 1])
        g = jax.nn.sigmoid(gh.astype(F32))
        return (g[..., 0:1] * o_c + g[..., 1:2] * o_s + g[..., 2:3] * o_w).astype(qh.dtype)

    o = map_query_blocks(blk, (q, gate, pos[None]), qb)
    new_win = jnp.concatenate([win_prior, win_new], axis=1)[:, -min(WINDOW, pw + t):]
    return o.reshape(b, t, NSA_HEADS * NSA_HD), kv_new, new_win


def mem_kv_rows(mem, g, wk, wv):
    b, nm, d = mem.shape
    kv = rms_matmul(mem.reshape(b * nm, d), jnp.concatenate([wk, wv], axis=1), g=g)
    hd = X_HEADS * X_HD
    k = kv[:, :hd].reshape(b, nm, X_HEADS, X_HD)
    v = kv[:, hd:].reshape(b, nm, X_HEADS, X_HD)
    return jnp.stack([k, v], axis=2)


def mem_attend(x, g, mkv, wq, wo):
    b, t, d = x.shape
    q = rms_matmul(x.reshape(b * t, d), wq, g=g).reshape(b, t, X_HEADS, X_HD)
    s = jnp.einsum('bthd,bmhd->bhtm', q, mkv[:, :, 0]).astype(F32) * (X_HD ** -0.5)
    p = jax.nn.softmax(s, axis=-1).astype(mkv.dtype)
    o = jnp.einsum('bhtm,bmhd->bthd', p, mkv[:, :, 1]).reshape(b * t, X_HEADS * X_HD)
    return rms_matmul(o, wo, res=x.reshape(b * t, d)).reshape(b, t, d)


def odd_mixers_prompt(proj, b, t, pe, wc1, wc2):
    n = b * t
    sb_w = SB_HEADS * SB_HD
    kv_w = SB_KV_HEADS * SB_HD
    q_col = sb_w + 2 * kv_w
    kv_col = q_col + NSA_HEADS * NSA_HD
    win_col = kv_col + 4 * NSA_HD
    gate_col = win_col + 2 * NSA_HD
    o_sb = sb_attn_prompt(proj, b, t, 0, sb_w, sb_w + kv_w)
    sb_rows = proj[:, sb_w:sb_w + 2 * kv_w].reshape(b, t, 2, SB_KV_HEADS, SB_HD)
    nsa_rows = proj[:, kv_col:win_col].reshape(b, t, 4, NSA_HD)
    win_rows = proj[:, win_col:gate_col].reshape(b, t, 2, NSA_HD)
    ck = proj[:, kv_col:kv_col + 2 * NSA_HD].reshape(b, t // CMP_STRIDE, CMP_STRIDE, 2, NSA_HD)
    k_cmp, v_cmp = nsa_compress(ck, pe, wc1, wc2)
    o_c, sel = nsa_cmp_select(proj, q_col, k_cmp, v_cmp, b, t, 0, t // SEL_BLOCK)
    slopes = _alibi_slopes_np(NSA_HEADS)
    common = dict(heads=NSA_HEADS, dk=NSA_HD, dv=NSA_HD, k_off=0, v_off=NSA_HD, scale=NSA_HD ** -0.5, slopes=slopes)
    o_s = mqa_flash(proj, q_col, proj, kv_col + 2 * NSA_HD, 2 * NSA_HD, b, t, mode="select", sel=sel, **common)
    o_w = mqa_flash(proj, q_col, proj, win_col, 2 * NSA_HD, b, t, mode="window", **common)
    g = jax.nn.sigmoid(proj[:, gate_col:gate_col + 3 * NSA_HEADS]).reshape(n, NSA_HEADS, 3)
    sh = (n, NSA_HEADS, NSA_HD)
    o_nsa = (g[..., 0:1] * o_c.reshape(sh) + g[..., 1:2] * o_s.reshape(sh) + g[..., 2:3] * o_w.reshape(sh)).reshape(n, -1)
    return o_sb, sb_rows, o_nsa, nsa_rows, win_rows[:, -min(WINDOW, t):]


def _pad_rows(a, rows):
    return jnp.pad(a, ((0, 0), (0, rows - a.shape[1]), (0, 0)))


def _rows_head_major(a, b, t, heads):
    w = a.shape[1] // heads
    return jnp.transpose(a.reshape(b, t, heads, w), (0, 2, 1, 3)).reshape(b, heads * t, w)


def _rows_token_major(a, b, t, heads):
    w = a.shape[2]
    return jnp.transpose(a.reshape(b, heads, t, w), (0, 2, 1, 3)).reshape(b * t, heads * w)


def odd_mixers_sample(proj, b, t, past, page_table, pool_sb, pool_nsa, win_prior, pe, wc1, wc2):
    n = b * t
    page = pool_sb.shape[1]
    sb_w = SB_HEADS * SB_HD
    kv_w = SB_KV_HEADS * SB_HD
    rep = SB_HEADS // SB_KV_HEADS
    q_col = sb_w + 2 * kv_w
    kv_col = q_col + NSA_HEADS * NSA_HD
    win_col = kv_col + 4 * NSA_HD
    gate_col = win_col + 2 * NSA_HD
    assert past % page == 0 and past % SEL_BLOCK == 0 and t <= CMP_STRIDE and t <= page

    q = jnp.transpose(proj[:, :sb_w].reshape(b, t, SB_KV_HEADS, rep, SB_HD), (0, 2, 3, 1, 4))
    eye = jnp.eye(SB_KV_HEADS, dtype=F32)
    q_rows = (q[:, :, :, :, None, :] * eye[None, :, None, None, :, None]).reshape(b, SB_HEADS * t, kv_w)
    new_sk = _pad_rows(proj[:, sb_w:sb_w + kv_w].reshape(b, t, kv_w), page)
    new_sv = _pad_rows(proj[:, sb_w + kv_w:sb_w + 2 * kv_w].reshape(b, t, kv_w), page)
    o = sb_paged(q_rows, jnp.transpose(pool_sb, (0, 2, 3, 4, 1)), page_table, new_sk, new_sv, t)
    o = o.reshape(b, SB_KV_HEADS, rep, t, SB_KV_HEADS, SB_HD)
    o = jnp.stack([o[:, g, :, :, g] for g in range(SB_KV_HEADS)], axis=1)
    o_sb = jnp.transpose(o, (0, 3, 1, 2, 4)).reshape(n, sb_w)
    sb_rows = proj[:, sb_w:sb_w + 2 * kv_w].reshape(b, t, 2, SB_KV_HEADS, SB_HD)

    nsa_rows = proj[:, kv_col:win_col].reshape(b, t, 4, NSA_HD)
    win_new = proj[:, win_col:gate_col].reshape(b, t, 2, NSA_HD)
    n_pages = page_table.shape[1]
    w_rows = _cmp_weight_halves(wc1).reshape(2, CMP_STRIDE, NSA_HD, 2 * CMP_HID).astype(BF16)
    pool_t = jnp.transpose(pool_nsa, (0, 2, 3, 1))
    pr = nsa_chunk_proj(pool_t, page_table, w_rows)
    k_cmp, v_cmp = nsa_compress_tail([pr[:, 0], pr[:, 1]], pe, wc1, wc2)
    slopes_rows = jnp.repeat(jnp.asarray(_alibi_slopes_np(NSA_HEADS), F32), t).reshape(NSA_HEADS * t, 1)
    q_nsa = _rows_head_major(proj[:, q_col:kv_col], b, t, NSA_HEADS)
    n_sel = -(-(past + t) // SEL_BLOCK)
    o_c, sel = nsa_cmp_decode(q_nsa, slopes_rows, k_cmp, v_cmp, t, past, n_sel)
    common = dict(t_new=t, scale=NSA_HD ** -0.5, q_base=past)

    def new_cols(c0):
        return _pad_rows(proj[:, c0:c0 + NSA_HD].reshape(b, t, NSA_HD), page)

    per_page = page // SEL_BLOCK
    picked = sel[:, :, :n_pages * per_page].reshape(b, t, n_pages, per_page).sum(axis=(1, 3)) > 0.5
    cnt = picked.sum(axis=1).astype(jnp.int32)
    order = jnp.argsort(jnp.logical_not(picked), axis=1, stable=True).astype(jnp.int32)
    last = jnp.take_along_axis(order, jnp.maximum(cnt - 1, 0)[:, None], axis=1)
    vis = jnp.where(jnp.arange(n_pages)[None, :] < cnt[:, None], order, last)
    kv_pair = lambda r: (r[0, 0], r[0, 1])
    o_s = paged_attn(q_nsa, slopes_rows, pool_t, kv_pair, NSA_HD, page_table,
                     new_cols(kv_col + 2 * NSA_HD), new_cols(kv_col + 3 * NSA_HD), mode="select", kbase=0,
                     sel_rows=jnp.tile(sel, (1, NSA_HEADS, 1)), visit=(vis, cnt), pg=16,
                     pool_block=((2, NSA_HD, page), (1, 0, 0)), **common)
    pw = win_prior.shape[1]
    assert pw % page == 0 and pw <= past
    win_pool = jnp.transpose(win_prior.reshape(b, pw // page, page, 2, NSA_HD), (0, 1, 3, 4, 2))
    win_pool = win_pool.reshape(b * (pw // page), 2, NSA_HD, page)
    win_pt = jnp.arange(b * (pw // page), dtype=jnp.int32).reshape(b, pw // page)
    o_w = paged_attn(q_nsa, slopes_rows, win_pool, kv_pair, NSA_HD, win_pt,
                     new_cols(win_col), new_cols(win_col + NSA_HD), mode="window", kbase=past - pw, **common)
    g = jax.nn.sigmoid(proj[:, gate_col:gate_col + 3 * NSA_HEADS]).reshape(n, NSA_HEADS, 3)
    sh = (n, NSA_HEADS, NSA_HD)
    o_nsa = (g[..., 0:1] * _rows_token_major(o_c, b, t, NSA_HEADS).reshape(sh)
             + g[..., 1:2] * _rows_token_major(o_s, b, t, NSA_HEADS).reshape(sh)
             + g[..., 2:3] * _rows_token_major(o_w, b, t, NSA_HEADS).reshape(sh)).reshape(n, -1)
    win = jnp.concatenate([win_prior, win_new], axis=1)[:, -min(WINDOW, pw + t):]
    return o_sb, sb_rows, o_nsa, nsa_rows, win


def mla_attend(c_q, c_kv_raw, k_r_raw, b, t, start, g_cq, w_uq, g_ckv, w_uk, w_uv, paged=None):
    n = b * t
    pos = start + jnp.arange(t, dtype=jnp.int32)
    cos, sin = rope_angles(pos, ROPE_DIM)
    cos = jnp.tile(cos, (b, 1))
    sin = jnp.tile(sin, (b, 1))
    q = rms_matmul(c_q, w_uq.reshape(Q_RANK, -1), g=g_cq).reshape(n, MLA_HEADS, NOPE_DIM + ROPE_DIM)
    q_rope = apply_rope(q[..., NOPE_DIM:], cos[:, None, :], sin[:, None, :])
    q_lat = jnp.einsum('nhd,rhd->nhr', q[..., :NOPE_DIM], w_uk)
    new_rows = jnp.concatenate([rmsnorm(c_kv_raw, g_ckv), apply_rope(k_r_raw, cos, sin)], axis=-1)
    qf = jnp.concatenate([q_lat, q_rope], axis=-1).reshape(n, -1)
    dk = KV_RANK + ROPE_DIM
    scale = (NOPE_DIM + ROPE_DIM) ** -0.5
    if paged is None:
        o_lat = mqa_flash(qf, 0, new_rows, 0, dk, b, t, mode="causal", heads=MLA_HEADS, dk=dk, dv=KV_RANK,
                          k_off=0, v_off=0, scale=scale)
    else:
        pool, page_table = paged
        page = pool.shape[1]
        new_k = _pad_rows(new_rows.reshape(b, t, dk), page)
        o = paged_attn(_rows_head_major(qf, b, t, MLA_HEADS), jnp.zeros((MLA_HEADS * t, 1), F32),
                       jnp.transpose(pool, (0, 2, 1)),
                       lambda r: (r[0], r[0, :KV_RANK, :]), KV_RANK, page_table, new_k, new_k[:, :, :KV_RANK],
                       mode="causal", t_new=t, scale=scale, kbase=0, q_base=start, use_slopes=False, pg=16)
        o_lat = _rows_token_major(o, b, t, MLA_HEADS)
    out = jnp.einsum('nhr,rhd->nhd', o_lat.reshape(n, MLA_HEADS, KV_RANK), w_uv).reshape(n, MLA_HEADS * MLA_VDIM)
    return out, new_rows


def trunk(x, start, ml_state, mem_kv, p, caches=None):
    b, t, d = x.shape
    depth = p['g_mix'].shape[0]
    mla_rows, ml_c, ml_n, ml_m, sb_rows, nsa_rows, wins = [], [], [], [], [], [], []
    for l in range(depth):
        e = l // 2
        x2 = x.reshape(b * t, d)
        if l % 2 == 0:
            proj = rms_matmul(x2, p['w_in_even'][e], g=p['g_mix'][l])
            c_q, c_kv, k_r, mq, mk, mv, mi, mf, mo = split_cols(proj, EVEN_COLS)
            paged = None if caches is None else (caches['mla'][e], caches['page_table'])
            o_a, rows = mla_attend(c_q, c_kv, k_r, b, t, start, p['g_cq'][e], p['w_uq'][e], p['g_ckv'][e], p['w_uk'][e], p['w_uv'][e], paged=paged)
            rows = rows.reshape(b, t, -1)
            mq, mk, mv, mi, mf, mo = (a.reshape(b, t, -1) for a in (mq, mk, mv, mi, mf, mo))
            c0, n0, m0 = ml_state[e]
            o_b, c, n, m = mlstm_mixer(mq, mk, mv, mi + p['b_ml_i'][e], mf + p['b_ml_f'][e], mo, c0, n0, m0, p['g_mh'][e])
            mla_rows.append(rows)
            ml_c.append(c)
            ml_n.append(n)
            ml_m.append(m)
            mix = jnp.concatenate([o_a, o_b.reshape(b * t, -1)], axis=-1)
            x = rms_matmul(mix, p['w_out_even'][e], res=x2).reshape(b, t, d)
        else:
            proj = rms_matmul(x2, p['w_in_odd'][e], g=p['g_mix'][l], keep_pad=True)
            if caches is None:
                o_c, srows, o_d, nrows, win = odd_mixers_prompt(proj, b, t, p['nsa_pe'][e], p['nsa_wc1'][e], p['nsa_wc2'][e])
            else:
                o_c, srows, o_d, nrows, win = odd_mixers_sample(
                    proj, b, t, start, caches['page_table'], caches['sb'][e], caches['nsa'][e], caches['win'][e],
                    p['nsa_pe'][e], p['nsa_wc1'][e], p['nsa_wc2'][e])
            sb_rows.append(srows)
            nsa_rows.append(nrows)
            wins.append(win)
            mix = jnp.concatenate([o_c, o_d], axis=-1)
            x = rms_matmul(mix, p['w_out_odd'][e], res=x2).reshape(b, t, d)
        x = mem_attend(x, p['g_xattn'][l], mem_kv[l], p['w_xq'][l], p['w_xo'][l])
        x2 = x.reshape(b * t, d)
        x = (x2 + moe(x2, p['g_ffn'][l], p['w_rg'][l], p['b_rg'][l], p['w_re'][l], p['b_re'][l], p['w_e1'][l], p['w_e3'][l], p['w_e2'][l])).reshape(b, t, d)
    y = rmsnorm(x, p['g_final'])
    return y, mla_rows, ml_c, ml_n, ml_m, sb_rows, nsa_rows, wins


def kernel(x_prompt, x_sample, mem_prompt, cache_mla, state_mlstm_c, state_mlstm_n, state_mlstm_m, cache_sb_kv, cache_nsa_kv, state_nsa_win, cache_mem_kv, page_table, g_mix, w_in_even, b_ml_i, b_ml_f, g_cq, w_uq, g_ckv, w_uk, w_uv, g_mh, w_out_even, w_in_odd, nsa_pe, nsa_wc1, nsa_wc2, w_out_odd, g_xattn, g_memnorm, w_xq, w_xk, w_xv, w_xo, g_ffn, w_rg, b_rg, w_re, b_re, w_e1, w_e3, w_e2, g_final):
    p = dict(g_mix=g_mix, w_in_even=w_in_even, b_ml_i=b_ml_i, b_ml_f=b_ml_f, g_cq=g_cq, w_uq=w_uq, g_ckv=g_ckv, w_uk=w_uk, w_uv=w_uv, g_mh=g_mh, w_out_even=w_out_even, w_in_odd=w_in_odd, nsa_pe=nsa_pe, nsa_wc1=nsa_wc1, nsa_wc2=nsa_wc2, w_out_odd=w_out_odd, g_xattn=g_xattn, w_xq=w_xq, w_xo=w_xo, g_ffn=g_ffn, w_rg=w_rg, b_rg=b_rg, w_re=w_re, b_re=b_re, w_e1=w_e1, w_e3=w_e3, w_e2=w_e2, g_final=g_final)
    dt = x_prompt.dtype
    bp = x_prompt.shape[0]
    depth = g_mix.shape[0]
    n_even = (depth + 1) // 2
    n_odd = depth // 2
    mem_kv_list_p = [mem_kv_rows(mem_prompt, g_memnorm[l], w_xk[l], w_xv[l]) for l in range(depth)]
    y_prompt, mla_p, c_p, n_p, m_p, sb_p, nsa_p, win_pl = trunk(
        x_prompt, 0,
        [(jnp.zeros((bp, ML_HEADS, ML_V, ML_QK), F32), jnp.zeros((bp, ML_HEADS, ML_QK), F32), jnp.zeros((bp, ML_HEADS), F32)) for _ in range(n_even)],
        mem_kv_list_p, p)
    past_len = page_table.shape[1] * cache_mla.shape[2]
    caches = dict(page_table=page_table, mla=[cache_mla[e] for e in range(n_even)],
                  sb=[cache_sb_kv[e] for e in range(n_odd)], nsa=[cache_nsa_kv[e] for e in range(n_odd)],
                  win=[state_nsa_win[e] for e in range(n_odd)])
    y_sample, mla_s, c_s, n_s, m_s, sb_s, nsa_s, win_sl = trunk(
        x_sample, past_len,
        [(state_mlstm_c[e], state_mlstm_n[e], state_mlstm_m[e]) for e in range(n_even)],
        [cache_mem_kv[l] for l in range(depth)], p, caches=caches)
    return (y_prompt, y_sample, jnp.stack(mla_p), jnp.stack(mla_s), jnp.stack(c_p), jnp.stack(c_s),
            jnp.stack(n_p), jnp.stack(n_s), jnp.stack(m_p), jnp.stack(m_s), jnp.stack(sb_p), jnp.stack(sb_s),
            jnp.stack(nsa_p), jnp.stack(nsa_s), jnp.stack(win_pl), jnp.stack(win_sl), jnp.stack(mem_kv_list_p))
```

```python
import functools

import jax
import jax.numpy as jnp
import numpy as np
from jax import lax
from jax.experimental import pallas as pl
from jax.experimental.pallas import tpu as pltpu

F32 = jnp.float32
BF16 = jnp.bfloat16

D_MODEL = 2048
QBLOCK = 128
NEG = -1e30
EPS = 1e-6
MLA_HEADS = 8
Q_RANK = 512
KV_RANK = 256
NOPE_DIM = 128
ROPE_DIM = 64
MLA_VDIM = 128
ROPE_THETA = 10000.0
ML_HEADS = 4
ML_QK = 128
ML_V = 256
ML_CHUNK = 64
SB_HEADS = 8
SB_KV_HEADS = 4
SB_HD = 64
NSA_HEADS = 16
NSA_HD = 64
CMP_STRIDE = 16
CMP_LEN = 2 * CMP_STRIDE
CMP_HID = 128
SEL_BLOCK = 64
N_SEL = 16
WINDOW = 512
FORCE_BONUS = 1000.0
X_HEADS = 4
X_HD = 128
N_GROUPS = 4
N_EXP = 8
TOP_K = 2
D_EXP = 512

EVEN_COLS = (Q_RANK, KV_RANK, ROPE_DIM, ML_HEADS * ML_QK, ML_HEADS * ML_QK, ML_HEADS * ML_V, ML_HEADS, ML_HEADS, ML_HEADS * ML_V)
ODD_COLS = (SB_HEADS * SB_HD, SB_KV_HEADS * SB_HD, SB_KV_HEADS * SB_HD, NSA_HEADS * NSA_HD, 4 * NSA_HD, 2 * NSA_HD, 3 * NSA_HEADS)

LANE = 128
VMEM_LIMIT = 56 * 1024 * 1024
ROW_TILE = 512
EXPERT_TILE = 256


def _pick_tile(n, candidates):
    for c in candidates:
        if n % c == 0:
            return c
    raise ValueError(f"no tile in {candidates} divides {n}")


def _rms_matmul_kernel(*refs, do_norm, has_bias, has_res, act, pre_act):
    x_ref, g_ref, w_ref = refs[:3]
    rest = list(refs[3:])
    b_ref = rest.pop(0) if has_bias else None
    r_ref = rest.pop(0) if has_res else None
    o_ref, xn_ref = rest

    @pl.when(pl.program_id(1) == 0)
    def _():
        x = x_ref[...].astype(F32)
        if do_norm:
            ms = jnp.mean(x * x, axis=-1, keepdims=True)
            x = (x * lax.rsqrt(ms + EPS)) * g_ref[...]
        if pre_act == "gelu":
            x = jax.nn.gelu(x)
        xn_ref[...] = x.astype(BF16)

    acc = jnp.dot(xn_ref[...], w_ref[...], preferred_element_type=F32)
    if has_bias:
        acc = acc + b_ref[...]
    if act == "gelu":
        acc = jax.nn.gelu(acc)
    if has_res:
        acc = acc + r_ref[...]
    o_ref[...] = acc


def rms_matmul(x, w, g=None, res=None, bias=None, act=None, pre_act=None, keep_pad=False):
    n, k = x.shape
    m = w.shape[1]
    mp = -(-m // LANE) * LANE
    wb = w.astype(BF16)
    if mp != m:
        wb = jnp.pad(wb, ((0, 0), (0, mp - m)))
        if res is not None:
            res = jnp.pad(res, ((0, 0), (0, mp - m)))
        if bias is not None:
            bias = jnp.pad(bias, ((0, mp - m),))
    tm = _pick_tile(n, (ROW_TILE, 256, 128, 64, 32, 16, 8))
    tn = _pick_tile(mp, (512, 384, 256, 128))
    do_norm = g is not None
    gg = (g if do_norm else jnp.ones((k,), F32)).astype(F32).reshape(1, k)
    in_specs = [
        pl.BlockSpec((tm, k), lambda i, j: (i, 0)),
        pl.BlockSpec((1, k), lambda i, j: (0, 0)),
        pl.BlockSpec((k, tn), lambda i, j: (0, j)),
    ]
    args = [x, gg, wb]
    if bias is not None:
        in_specs.append(pl.BlockSpec((1, tn), lambda i, j: (0, j)))
        args.append(bias.astype(F32).reshape(1, mp))
    if res is not None:
        in_specs.append(pl.BlockSpec((tm, tn), lambda i, j: (i, j)))
        args.append(res)
    out = pl.pallas_call(
        functools.partial(_rms_matmul_kernel, do_norm=do_norm, has_bias=bias is not None,
                          has_res=res is not None, act=act, pre_act=pre_act),
        grid=(n // tm, mp // tn),
        in_specs=in_specs,
        out_specs=pl.BlockSpec((tm, tn), lambda i, j: (i, j)),
        out_shape=jax.ShapeDtypeStruct((n, mp), F32),
        scratch_shapes=[pltpu.VMEM((tm, k), BF16)],
        compiler_params=pltpu.CompilerParams(
            dimension_semantics=("parallel", "arbitrary"), vmem_limit_bytes=VMEM_LIMIT),
        name="rms_matmul",
    )(*args)
    return out if (keep_pad or mp == m) else out[:, :m]


ATT_TILE = 256


def _sb_prompt_kernel(q_ref, k_ref, v_ref, o_ref, qs_ref, acc_ref, run_ref, *, tq, groups, rep, hd, scale):
    qi = pl.program_id(1)
    kk = pl.program_id(2)
    nk = pl.num_programs(2)

    @pl.when(kk == 0)
    def _():
        for gi in range(groups):
            for ri in range(rep):
                c0 = (gi * rep + ri) * hd
                qs_ref[gi, ri * tq:(ri + 1) * tq, :] = (q_ref[:, c0:c0 + hd] * scale).astype(BF16)
        acc_ref[...] = jnp.zeros_like(acc_ref)
        run_ref[...] = jnp.zeros_like(run_ref)

    @pl.when(kk <= qi)
    def _():
        rows = rep * tq
        row_t = lax.broadcasted_iota(jnp.int32, (rows, tq), 0) % tq
        col = lax.broadcasted_iota(jnp.int32, (rows, tq), 1)
        mask = (col < row_t) | (kk > 0)
        later = (lax.broadcasted_iota(jnp.int32, (tq, tq), 0) >
                 lax.broadcasted_iota(jnp.int32, (tq, tq), 1)).astype(BF16)
        for gi in range(groups):
            kt = k_ref[:, gi * hd:(gi + 1) * hd].astype(BF16)
            vt = v_ref[:, gi * hd:(gi + 1) * hd].astype(BF16)
            z = lax.dot_general(qs_ref[gi], kt, (((1,), (1,)), ((), ())), preferred_element_type=F32)
            ls = jnp.minimum(z, 0.0) - jnp.log(1.0 + jnp.exp(-jnp.abs(z)))
            l1mb = jnp.where(mask, ls - z, 0.0)
            hi = l1mb.astype(BF16)
            lo = (l1mb - hi.astype(F32)).astype(BF16)
            aft = jnp.dot(hi, later, preferred_element_type=F32) + jnp.dot(lo, later, preferred_element_type=F32)
            run = run_ref[gi][:, :1]
            att = jnp.where(mask, jnp.exp(ls + aft + run), 0.0)
            acc_ref[gi] += jnp.dot(att.astype(BF16), vt, preferred_element_type=F32)
            run_ref[gi] = jnp.broadcast_to(run + aft[:, :1] + l1mb[:, :1], (rows, LANE))

    @pl.when(kk == nk - 1)
    def _():
        for gi in range(groups):
            for ri in range(rep):
                c0 = (gi * rep + ri) * hd
                o_ref[:, c0:c0 + hd] = acc_ref[gi, ri * tq:(ri + 1) * tq, :]


def sb_attn_prompt(proj, b, t, q_col, k_col, v_col):
    tq = ATT_TILE
    nq = t // tq
    qw = SB_HEADS * SB_HD
    kw = SB_KV_HEADS * SB_HD
    rep = SB_HEADS // SB_KV_HEADS
    assert t % tq == 0 and q_col % qw == 0 and k_col % kw == 0 and v_col % kw == 0
    kern = functools.partial(_sb_prompt_kernel, tq=tq, groups=SB_KV_HEADS, rep=rep, hd=SB_HD, scale=SB_HD ** -0.5)
    return pl.pallas_call(
        kern,
        grid=(b, nq, nq),
        in_specs=[
            pl.BlockSpec((tq, qw), lambda bi, qi, kk: (bi * nq + qi, q_col // qw)),
            pl.BlockSpec((tq, kw), lambda bi, qi, kk: (bi * nq + jnp.maximum(qi - kk, 0), k_col // kw)),
            pl.BlockSpec((tq, kw), lambda bi, qi, kk: (bi * nq + jnp.maximum(qi - kk, 0), v_col // kw)),
        ],
        out_specs=pl.BlockSpec((tq, qw), lambda bi, qi, kk: (bi * nq + qi, 0)),
        out_shape=jax.ShapeDtypeStruct((b * t, qw), F32),
        scratch_shapes=[
            pltpu.VMEM((SB_KV_HEADS, rep * tq, SB_HD), BF16),
            pltpu.VMEM((SB_KV_HEADS, rep * tq, SB_HD), F32),
            pltpu.VMEM((SB_KV_HEADS, rep * tq, LANE), F32),
        ],
        compiler_params=pltpu.CompilerParams(
            dimension_semantics=("parallel", "parallel", "arbitrary"), vmem_limit_bytes=VMEM_LIMIT),
        name="sb_attn_prompt",
    )(proj, proj, proj)


def _alibi_slopes_np(n):
    return [float(v) for v in (np.float32(2.0) ** (np.float32(-8.0) * np.arange(1, n + 1, dtype=np.float32) / np.float32(n)))]


def _mqa_flash_kernel(*refs, mode, heads, dk, dv, k_off, v_off, tq, scale, slopes, window, sel_block, nks):
    if mode == "select":
        q_ref, kv_ref, sel_ref, o_ref, qs_ref, m_ref, l_ref, acc_ref = refs
    else:
        q_ref, kv_ref, o_ref, qs_ref, m_ref, l_ref, acc_ref = refs
    qi = pl.program_id(1)
    kk = pl.program_id(2)
    kj = qi - (nks - 1) + kk if mode == "window" else kk
    prescale = float(np.log2(scale)).is_integer()

    @pl.when(kk == 0)
    def _():
        for h in range(heads):
            qh = q_ref[:, h * dk:(h + 1) * dk]
            qs_ref[h] = (qh * scale if prescale else qh).astype(BF16)
        m_ref[...] = jnp.full_like(m_ref, NEG)
        l_ref[...] = jnp.zeros_like(l_ref)
        acc_ref[...] = jnp.zeros_like(acc_ref)

    active = (kj >= 0) if mode == "window" else (kj <= qi)

    @pl.when(active)
    def _():
        q_pos = qi * tq + lax.broadcasted_iota(jnp.int32, (tq, tq), 0)
        k_pos = kj * tq + lax.broadcasted_iota(jnp.int32, (tq, tq), 1)
        dist_i = q_pos - k_pos
        mask = dist_i >= 0
        if mode == "window":
            mask = mask & (dist_i < window)
        if mode == "select":
            nblk = sel_ref.shape[1]
            blk_of_key = (kj * tq + lax.broadcasted_iota(jnp.int32, (nblk, tq), 1)) // sel_block
            expand = (lax.broadcasted_iota(jnp.int32, (nblk, tq), 0) == blk_of_key).astype(BF16)
            chosen = jnp.dot(sel_ref[...].astype(BF16), expand, preferred_element_type=F32)
            mask = mask & (chosen > 0.5)
        dist = dist_i.astype(F32)
        kt = kv_ref[:, k_off:k_off + dk].astype(BF16)
        vt = kv_ref[:, v_off:v_off + dv].astype(BF16)
        for h in range(heads):
            s = lax.dot_general(qs_ref[h], kt, (((1,), (1,)), ((), ())), preferred_element_type=F32)
            if not prescale:
                s = s * scale
            if slopes is not None:
                s = s - slopes[h] * dist
            s = jnp.where(mask, s, NEG)
            m_old = m_ref[h][:, :1]
            m_new = jnp.maximum(m_old, jnp.max(s, axis=-1, keepdims=True))
            alpha = jnp.exp(m_old - m_new)
            p = jnp.exp(s - m_new)
            l_ref[h] = jnp.broadcast_to(alpha * l_ref[h][:, :1] + jnp.sum(p, axis=-1, keepdims=True), (tq, LANE))
            acc_ref[h] = alpha * acc_ref[h] + jnp.dot(p.astype(BF16), vt, preferred_element_type=F32)
            m_ref[h] = jnp.broadcast_to(m_new, (tq, LANE))

    @pl.when(kk == nks - 1)
    def _():
        for h in range(heads):
            o_ref[:, h * dv:(h + 1) * dv] = acc_ref[h] / l_ref[h][:, :1]


def mqa_flash(q, q_col, kv, kv_col, kv_w, b, t, *, mode, heads, dk, dv, k_off, v_off, scale, slopes=None, sel=None):
    tq = ATT_TILE
    nq = t // tq
    qw = heads * dk
    assert t % tq == 0 and q_col % qw == 0 and kv_col % kv_w == 0
    nks = (WINDOW // tq + 1) if mode == "window" else nq
    if mode == "window":
        kv_idx = lambda bi, qi, kk: (bi * nq + jnp.maximum(qi - (nks - 1) + kk, 0), kv_col // kv_w)
    else:
        kv_idx = lambda bi, qi, kk: (bi * nq + jnp.minimum(kk, qi), kv_col // kv_w)
    in_specs = [
        pl.BlockSpec((tq, qw), lambda bi, qi, kk: (bi * nq + qi, q_col // qw)),
        pl.BlockSpec((tq, kv_w), kv_idx),
    ]
    args = [q, kv]
    if mode == "select":
        in_specs.append(pl.BlockSpec((tq, sel.shape[1]), lambda bi, qi, kk: (bi * nq + qi, 0)))
        args.append(sel)
    kern = functools.partial(_mqa_flash_kernel, mode=mode, heads=heads, dk=dk, dv=dv, k_off=k_off, v_off=v_off,
                             tq=tq, scale=scale, slopes=slopes, window=WINDOW, sel_block=SEL_BLOCK, nks=nks)
    return pl.pallas_call(
        kern,
        grid=(b, nq, nks),
        in_specs=in_specs,
        out_specs=pl.BlockSpec((tq, heads * dv), lambda bi, qi, kk: (bi * nq + qi, 0)),
        out_shape=jax.ShapeDtypeStruct((b * t, heads * dv), F32),
        scratch_shapes=[
            pltpu.VMEM((heads, tq, dk), BF16),
            pltpu.VMEM((heads, tq, LANE), F32),
            pltpu.VMEM((heads, tq, LANE), F32),
            pltpu.VMEM((heads, tq, dv), F32),
        ],
        compiler_params=pltpu.CompilerParams(
            dimension_semantics=("parallel", "parallel", "arbitrary"), vmem_limit_bytes=VMEM_LIMIT),
        name="mqa_flash_" + mode,
    )(*args)


def _nsa_cmp_kernel(q_ref, kc_ref, vc_ref, oc_ref, sel_ref, *, tq, heads, hd, scale, slopes, pos0, n_sel, k_top):
    qi = pl.program_id(1)
    nc = kc_ref.shape[1]
    lsel = sel_ref.shape[1]
    pos = pos0 + qi * tq + lax.broadcasted_iota(jnp.int32, (tq, nc), 0)
    cmp_end = CMP_STRIDE * lax.broadcasted_iota(jnp.int32, (tq, nc), 1) + (CMP_LEN - 1)
    valid = cmp_end <= pos
    dist = (pos - cmp_end).astype(F32)
    kc = kc_ref[0].astype(BF16)
    vc = vc_ref[0].astype(BF16)
    psum = jnp.zeros((tq, nc), F32)
    for h in range(heads):
        qh = (q_ref[:, h * hd:(h + 1) * hd] * scale).astype(BF16)
        s = lax.dot_general(qh, kc, (((1,), (1,)), ((), ())), preferred_element_type=F32) - slopes[h] * dist
        s = jnp.where(valid, s, NEG)
        e = jnp.exp(s - jnp.max(s, axis=-1, keepdims=True))
        p = jnp.where(valid, e / jnp.sum(e, axis=-1, keepdims=True), 0.0)
        oc_ref[:, h * hd:(h + 1) * hd] = jnp.dot(p.astype(BF16), vc, preferred_element_type=F32)
        psum = psum + p
    sel_ref[...] = _select_blocks(psum, pos0, qi * tq, tq, nc, lsel, n_sel, k_top)


def nsa_cmp_select(q, q_col, k_cmp, v_cmp, b, t, pos0, n_sel):
    tq = min(ATT_TILE, t)
    nq = t // tq
    qw = NSA_HEADS * NSA_HD
    nc = k_cmp.shape[1]
    lsel = -(-n_sel // LANE) * LANE
    assert t % tq == 0 and q_col % qw == 0
    kern = functools.partial(_nsa_cmp_kernel, tq=tq, heads=NSA_HEADS, hd=NSA_HD, scale=NSA_HD ** -0.5,
                             slopes=_alibi_slopes_np(NSA_HEADS), pos0=pos0, n_sel=n_sel, k_top=min(N_SEL, n_sel))
    return pl.pallas_call(
        kern,
        grid=(b, nq),
        in_specs=[
            pl.BlockSpec((tq, qw), lambda bi, qi: (bi * nq + qi, q_col // qw)),
            pl.BlockSpec((1, nc, NSA_HD), lambda bi, qi: (bi, 0, 0)),
            pl.BlockSpec((1, nc, NSA_HD), lambda bi, qi: (bi, 0, 0)),
        ],
        out_specs=[
            pl.BlockSpec((tq, qw), lambda bi, qi: (bi * nq + qi, 0)),
            pl.BlockSpec((tq, lsel), lambda bi, qi: (bi * nq + qi, 0)),
        ],
        out_shape=[jax.ShapeDtypeStruct((b * t, qw), F32), jax.ShapeDtypeStruct((b * t, lsel), F32)],
        compiler_params=pltpu.CompilerParams(
            dimension_semantics=("parallel", "parallel"), vmem_limit_bytes=VMEM_LIMIT),
        name="nsa_cmp_select",
    )(q, k_cmp, v_cmp)


def nsa_compress(ck, pe, wc1, wc2):
    b, n_chunk = ck.shape[:2]
    feat = CMP_STRIDE * NSA_HD
    x = jnp.moveaxis(ck, 3, 0).reshape(2, b * n_chunk, feat)
    w_halves = _cmp_weight_halves(wc1)
    pr = [rms_matmul(x[kind], w_halves[kind]).reshape(b, n_chunk, 2 * CMP_HID) for kind in range(2)]
    return nsa_compress_tail(pr, pe, wc1, wc2)


def _cmp_weight_halves(wc1):
    feat = CMP_STRIDE * NSA_HD
    return jnp.concatenate([wc1[:, :feat], wc1[:, feat:]], axis=2)


def nsa_compress_tail(pr, pe, wc1, wc2):
    b, n_chunk = pr[0].shape[:2]
    feat = CMP_STRIDE * NSA_HD
    pe_flat = jnp.pad(pe.reshape(2, 1, 2 * feat), ((0, 0), (0, 7), (0, 0)))
    out = []
    for kind in range(2):
        pe_term = rms_matmul(pe_flat[kind], wc1[kind])[0]
        nxt = jnp.concatenate([pr[kind][:, 1:, CMP_HID:], jnp.zeros((b, 1, CMP_HID), F32)], axis=1)
        pre = (pr[kind][:, :, :CMP_HID] + nxt + pe_term).reshape(b * n_chunk, CMP_HID)
        out.append(rms_matmul(pre, wc2[kind], pre_act="gelu").reshape(b, n_chunk, NSA_HD))
    return out[0], out[1]


PAGES_PER_STEP = 8


def _paged_attn_kernel(pt_ref, vis_ref, cnt_ref, *refs, mode, pg, t_new, kv_of, scale, kbase, q_base, page,
                       use_slopes, n_vis):
    refs = list(refs)
    q_ref = refs.pop(0)
    slope_ref = refs.pop(0)
    sel_ref = refs.pop(0) if mode == "select" else None
    page_refs = [refs.pop(0) for _ in range(pg)]
    newk_ref, newv_ref, o_ref, m_ref, l_ref, acc_ref = refs
    bi = pl.program_id(0)
    j = pl.program_id(1)
    n_steps = pl.num_programs(1)
    rows = q_ref.shape[1]
    cnt = cnt_ref[bi]

    @pl.when(j == 0)
    def _():
        m_ref[...] = jnp.full_like(m_ref, NEG)
        l_ref[...] = jnp.zeros_like(l_ref)
        acc_ref[...] = jnp.zeros_like(acc_ref)

    qb = q_ref[0].astype(BF16)
    q_pos = q_base + lax.broadcasted_iota(jnp.int32, (rows, page), 0) % t_new
    lane = lax.broadcasted_iota(jnp.int32, (rows, page), 1)

    def scores(s, kpos0, mask):
        dist_i = q_pos - (kpos0 + lane)
        if use_slopes:
            s = s - slope_ref[:, :1] * dist_i.astype(F32)
        if mode == "window":
            wmask = dist_i < WINDOW
            mask = wmask if mask is None else mask & wmask
        if mode == "select":
            selr = sel_ref[0]
            blk = lax.broadcasted_iota(jnp.int32, selr.shape, 1)
            blk0 = kpos0 // SEL_BLOCK
            c0 = jnp.sum(jnp.where(blk == blk0, selr, 0.0), axis=1, keepdims=True)
            c1 = jnp.sum(jnp.where(blk == blk0 + 1, selr, 0.0), axis=1, keepdims=True)
            cmask = jnp.where(lane < SEL_BLOCK, c0, c1) > 0.5
            mask = cmask if mask is None else mask & cmask
        if mask is not None:
            s = jnp.where(mask, s, NEG)
        return s

    def update(s_list, v, v_transposed):
        m_old = m_ref[:, :1]
        mx = s_list[0]
        for s in s_list[1:]:
            mx = jnp.maximum(mx, s)
        m_new = jnp.maximum(m_old, jnp.max(mx, axis=-1, keepdims=True))
        alpha = jnp.exp(m_old - m_new)
        p_list = [jnp.exp(s - m_new) for s in s_list]
        ps = p_list[0]
        for p in p_list[1:]:
            ps = ps + p
        p_all = (p_list[0] if len(p_list) == 1 else jnp.concatenate(p_list, axis=1)).astype(BF16)
        if v_transposed:
            pv = lax.dot_general(p_all, v, (((1,), (1,)), ((), ())), preferred_element_type=F32)
        else:
            pv = jnp.dot(p_all, v, preferred_element_type=F32)
        l_ref[...] = jnp.broadcast_to(alpha * l_ref[:, :1] + jnp.sum(ps, axis=-1, keepdims=True), l_ref.shape)
        m_ref[...] = jnp.broadcast_to(m_new, m_ref.shape)
        acc_ref[...] = alpha * acc_ref[...] + pv

    @pl.when(j * pg < cnt)
    def _():
        kvs = [kv_of(pr) for pr in page_refs]
        k_all = jnp.concatenate([k.astype(BF16) for k, _ in kvs], axis=1)
        v_all = jnp.concatenate([v.astype(BF16) for _, v in kvs], axis=1)
        s_all = jnp.dot(qb, k_all, preferred_element_type=F32) * scale
        s_list = []
        for i in range(pg):
            slot = j * pg + i
            s = scores(s_all[:, i * page:(i + 1) * page], kbase + vis_ref[bi * n_vis + slot] * page, None)
            if mode == "select":
                s = jnp.where(slot < cnt, s, NEG)
            s_list.append(s)
        update(s_list, v_all, True)

    @pl.when(j == n_steps - 1)
    def _():
        dist_new = q_pos - (q_base + lane)
        s = lax.dot_general(qb, newk_ref[0].astype(BF16), (((1,), (1,)), ((), ())),
                            preferred_element_type=F32) * scale
        s = scores(s, q_base, (lane < t_new) & (dist_new >= 0))
        update([s], newv_ref[0].astype(BF16), False)
        o_ref[0] = acc_ref[...] / l_ref[:, :1]


def paged_attn(q, slopes_rows, pool, kv_of, dv, page_table, new_k, new_v, *, mode, t_new, scale, kbase, q_base,
               sel_rows=None, visit=None, use_slopes=True, pg=PAGES_PER_STEP, pool_block=None):
    b, rows, dk = q.shape
    n_pages = page_table.shape[1]
    page = pool.shape[-1]
    pg = min(pg, n_pages)
    assert n_pages % pg == 0
    n_steps = n_pages // pg
    if visit is None:
        visit = (jnp.tile(jnp.arange(n_pages, dtype=jnp.int32)[None], (b, 1)), jnp.full((b,), n_pages, jnp.int32))
    vis, cnt = visit
    if pool_block is None:
        pool_block = (pool.shape[1:], (0,) * (pool.ndim - 1))
    blk = (1,) + tuple(pool_block[0])
    zeros = tuple(pool_block[1])
    in_specs = [
        pl.BlockSpec((1, rows, dk), lambda bi, j, pt, vs, ct: (bi, 0, 0)),
        pl.BlockSpec((rows, 1), lambda bi, j, pt, vs, ct: (0, 0)),
    ]
    args = [q, slopes_rows]
    if mode == "select":
        in_specs.append(pl.BlockSpec((1, rows, sel_rows.shape[2]), lambda bi, j, pt, vs, ct: (bi, 0, 0)))
        args.append(sel_rows)
    for i in range(pg):
        in_specs.append(pl.BlockSpec(
            blk, lambda bi, j, pt, vs, ct, i=i: (pt[bi * n_pages + vs[bi * n_pages + j * pg + i]],) + zeros))
        args.append(pool)
    in_specs.append(pl.BlockSpec((1, page, dk), lambda bi, j, pt, vs, ct: (bi, 0, 0)))
    in_specs.append(pl.BlockSpec((1, page, dv), lambda bi, j, pt, vs, ct: (bi, 0, 0)))
    args += [new_k, new_v]
    kern = functools.partial(_paged_attn_kernel, mode=mode, pg=pg, t_new=t_new, kv_of=kv_of, scale=scale, kbase=kbase,
                             q_base=q_base, page=page, use_slopes=use_slopes, n_vis=n_pages)
    return pl.pallas_call(
        kern,
        grid_spec=pltpu.PrefetchScalarGridSpec(
            num_scalar_prefetch=3,
            grid=(b, n_steps),
            in_specs=in_specs,
            out_specs=pl.BlockSpec((1, rows, dv), lambda bi, j, pt, vs, ct: (bi, 0, 0)),
            scratch_shapes=[
                pltpu.VMEM((rows, LANE), F32),
                pltpu.VMEM((rows, LANE), F32),
                pltpu.VMEM((rows, dv), F32),
            ],
        ),
        out_shape=jax.ShapeDtypeStruct((b, rows, dv), F32),
        compiler_params=pltpu.CompilerParams(
            dimension_semantics=("parallel", "arbitrary"), vmem_limit_bytes=VMEM_LIMIT),
        name="paged_attn_" + mode,
    )(page_table.reshape(-1), vis.reshape(-1), cnt, *args)


def _sb_paged_kernel(pt_ref, *refs, pg, t_new, kw, scale, page):
    refs = list(refs)
    q_ref = refs.pop(0)
    page_refs = [refs.pop(0) for _ in range(pg)]
    newk_ref, newv_ref, o_ref, run_ref, acc_ref = refs
    j = pl.program_id(1)
    n_steps = pl.num_programs(1)
    rows = q_ref.shape[1]
    qb = (q_ref[0] * scale).astype(BF16)
    later = (lax.broadcasted_iota(jnp.int32, (page, page), 0) >
             lax.broadcasted_iota(jnp.int32, (page, page), 1)).astype(BF16)

    def log_terms(z, mask):
        ls = jnp.minimum(z, 0.0) - jnp.log(1.0 + jnp.exp(-jnp.abs(z)))
        l1mb = ls - z
        if mask is not None:
            l1mb = jnp.where(mask, l1mb, 0.0)
        hi = l1mb.astype(BF16)
        lo = (l1mb - hi.astype(F32)).astype(BF16)
        return ls, l1mb, hi, lo

    def suffix_sums(hi, lo):
        return jnp.dot(hi, later, preferred_element_type=F32) + jnp.dot(lo, later, preferred_element_type=F32)

    @pl.when(j == 0)
    def _():
        mask = (lax.broadcasted_iota(jnp.int32, (rows, page), 1) <
                lax.broadcasted_iota(jnp.int32, (rows, page), 0) % t_new)
        z = lax.dot_general(qb, newk_ref[0].astype(BF16), (((1,), (1,)), ((), ())), preferred_element_type=F32)
        ls, l1mb, hi, lo = log_terms(z, mask)
        aft = suffix_sums(hi, lo)
        att = jnp.where(mask, jnp.exp(ls + aft), 0.0)
        acc_ref[...] = jnp.dot(att.astype(BF16), newv_ref[0].astype(BF16), preferred_element_type=F32)
        run_ref[...] = jnp.broadcast_to(aft[:, :1] + l1mb[:, :1], run_ref.shape)

    k_all = jnp.concatenate([pr[0, 0].reshape(kw, page).astype(BF16) for pr in page_refs], axis=1)
    v_all = jnp.concatenate([pr[0, 1].reshape(kw, page).astype(BF16) for pr in page_refs], axis=1)
    ls, l1mb, hi, lo = log_terms(jnp.dot(qb, k_all, preferred_element_type=F32), None)
    hi_rows = jnp.concatenate([hi[:, i * page:(i + 1) * page] for i in range(pg)], axis=0)
    lo_rows = jnp.concatenate([lo[:, i * page:(i + 1) * page] for i in range(pg)], axis=0)
    aft_rows = suffix_sums(hi_rows, lo_rows)
    run = run_ref[:, :1]
    att = []
    for i in range(pg):
        aft = aft_rows[i * rows:(i + 1) * rows]
        att.append(jnp.exp(ls[:, i * page:(i + 1) * page] + aft + run).astype(BF16))
        run = run + aft[:, :1] + l1mb[:, i * page:i * page + 1]
    run_ref[...] = jnp.broadcast_to(run, run_ref.shape)
    acc_ref[...] += lax.dot_general(jnp.concatenate(att, axis=1), v_all, (((1,), (1,)), ((), ())),
                                    preferred_element_type=F32)

    @pl.when(j == n_steps - 1)
    def _():
        o_ref[0] = acc_ref[...]


def sb_paged(q_rows, pool, page_table, new_k, new_v, t_new, pg=PAGES_PER_STEP):
    b, rows, kw = q_rows.shape
    n_pages = page_table.shape[1]
    page = pool.shape[-1]
    pg = min(pg, n_pages)
    assert n_pages % pg == 0
    n_steps = n_pages // pg
    in_specs = [pl.BlockSpec((1, rows, kw), lambda bi, j, pt: (bi, 0, 0))]
    args = [q_rows]
    blk = (1,) + pool.shape[1:]
    for i in range(pg):
        in_specs.append(pl.BlockSpec(
            blk, lambda bi, j, pt, i=i: (pt[bi * n_pages + n_pages - 1 - (j * pg + i)], 0, 0, 0, 0)))
        args.append(pool)
    in_specs.append(pl.BlockSpec((1, page, kw), lambda bi, j, pt: (bi, 0, 0)))
    in_specs.append(pl.BlockSpec((1, page, kw), lambda bi, j, pt: (bi, 0, 0)))
    args += [new_k, new_v]
    kern = functools.partial(_sb_paged_kernel, pg=pg, t_new=t_new, kw=kw, scale=SB_HD ** -0.5, page=page)
    return pl.pallas_call(
        kern,
        grid_spec=pltpu.PrefetchScalarGridSpec(
            num_scalar_prefetch=1,
            grid=(b, n_steps),
            in_specs=in_specs,
            out_specs=pl.BlockSpec((1, rows, kw), lambda bi, j, pt: (bi, 0, 0)),
            scratch_shapes=[pltpu.VMEM((rows, LANE), F32), pltpu.VMEM((rows, kw), F32)],
        ),
        out_shape=jax.ShapeDtypeStruct((b, rows, kw), F32),
        compiler_params=pltpu.CompilerParams(
            dimension_semantics=("parallel", "arbitrary"), vmem_limit_bytes=VMEM_LIMIT),
        name="sb_paged",
    )(page_table.reshape(-1), *args)


def _chunk_proj_kernel(pt_ref, *refs, pg, rows_per_chunk, kinds):
    page_refs = refs[:pg]
    w_ref, o_ref, rows_ref = refs[pg:]
    page = page_refs[0].shape[3]
    n_c = page // rows_per_chunk
    for kind in range(kinds):
        for i, pr in enumerate(page_refs):
            rows_ref[kind, i * page:(i + 1) * page, :] = pr[0, kind].T
        acc = jnp.zeros((pg * n_c, w_ref.shape[3]), F32)
        for r in range(rows_per_chunk):
            a = rows_ref[kind, pl.ds(r, pg * n_c, stride=rows_per_chunk), :]
            acc = acc + jnp.dot(a.astype(BF16), w_ref[kind, r], preferred_element_type=F32)
        o_ref[0, kind] = acc


def nsa_chunk_proj(pool, page_table, w_rows, pg=PAGES_PER_STEP):
    b, n_pages = page_table.shape
    hd, page = pool.shape[2:]
    n_c = page // CMP_STRIDE
    m = w_rows.shape[3]
    pg = min(pg, n_pages)
    assert n_pages % pg == 0
    blk = (1, 2, hd, page)
    in_specs = [pl.BlockSpec(blk, lambda bi, j, pt, i=i: (pt[bi * n_pages + j * pg + i], 0, 0, 0)) for i in range(pg)]
    in_specs.append(pl.BlockSpec(w_rows.shape, lambda bi, j, pt: (0, 0, 0, 0)))
    return pl.pallas_call(
        functools.partial(_chunk_proj_kernel, pg=pg, rows_per_chunk=CMP_STRIDE, kinds=2),
        grid_spec=pltpu.PrefetchScalarGridSpec(
            num_scalar_prefetch=1,
            grid=(b, n_pages // pg),
            in_specs=in_specs,
            out_specs=pl.BlockSpec((1, 2, pg * n_c, m), lambda bi, j, pt: (bi, 0, j, 0)),
            scratch_shapes=[pltpu.VMEM((2, pg * page, hd), F32)],
        ),
        out_shape=jax.ShapeDtypeStruct((b, 2, n_pages * n_c, m), F32),
        compiler_params=pltpu.CompilerParams(
            dimension_semantics=("parallel", "arbitrary"), vmem_limit_bytes=VMEM_LIMIT),
        name="nsa_chunk_proj",
    )(page_table.reshape(-1), *([pool] * pg), w_rows)


def _old_paged_mqa_kernel(pt_ref, *refs, mode, pg, t_new, dk, dv, k_off, v_off, scale, kbase, q_base, page, use_slopes):
    refs = list(refs)
    q_ref = refs.pop(0)
    slope_ref = refs.pop(0)
    sel_ref = refs.pop(0) if mode == "select" else None
    page_refs = [refs.pop(0) for _ in range(pg)]
    new_ref, o_ref, m_ref, l_ref, acc_ref = refs
    j = pl.program_id(1)
    n_steps = pl.num_programs(1)
    rows = q_ref.shape[1]

    @pl.when(j == 0)
    def _():
        m_ref[...] = jnp.full_like(m_ref, NEG)
        l_ref[...] = jnp.zeros_like(l_ref)
        acc_ref[...] = jnp.zeros_like(acc_ref)

    qb = q_ref[0].astype(BF16)
    q_pos = q_base + lax.broadcasted_iota(jnp.int32, (rows, page), 0) % t_new
    lane = lax.broadcasted_iota(jnp.int32, (rows, page), 1)

    def tile(kv, kpos0, is_new, carry):
        m_old, l_old, acc = carry
        kt = kv[:, k_off:k_off + dk].astype(BF16)
        vt = kv[:, v_off:v_off + dv].astype(BF16)
        s = lax.dot_general(qb, kt, (((1,), (1,)), ((), ())), preferred_element_type=F32) * scale
        dist_i = q_pos - (kpos0 + lane)
        if use_slopes:
            s = s - slope_ref[:, :1] * dist_i.astype(F32)
        mask = None
        if is_new:
            mask = (lane < t_new) & (dist_i >= 0)
        if mode == "window":
            wmask = dist_i < WINDOW
            mask = wmask if mask is None else mask & wmask
        if mode == "select":
            selr = sel_ref[0]
            blk = lax.broadcasted_iota(jnp.int32, selr.shape, 1)
            blk0 = kpos0 // SEL_BLOCK
            c0 = jnp.sum(jnp.where(blk == blk0, selr, 0.0), axis=1, keepdims=True)
            c1 = jnp.sum(jnp.where(blk == blk0 + 1, selr, 0.0), axis=1, keepdims=True)
            cmask = jnp.where(lane < SEL_BLOCK, c0, c1) > 0.5
            mask = cmask if mask is None else mask & cmask
        if mask is not None:
            s = jnp.where(mask, s, NEG)
        m_new = jnp.maximum(m_old, jnp.max(s, axis=-1, keepdims=True))
        alpha = jnp.exp(m_old - m_new)
        p = jnp.exp(s - m_new)
        l_new = alpha * l_old + jnp.sum(p, axis=-1, keepdims=True)
        acc = alpha * acc + jnp.dot(p.astype(BF16), vt, preferred_element_type=F32)
        return m_new, l_new, acc

    carry = (m_ref[:, :1], l_ref[:, :1], acc_ref[...])
    for i in range(pg):
        carry = tile(page_refs[i][0], kbase + (j * pg + i) * page, False, carry)
    m_ref[...] = jnp.broadcast_to(carry[0], m_ref.shape)
    l_ref[...] = jnp.broadcast_to(carry[1], l_ref.shape)
    acc_ref[...] = carry[2]

    @pl.when(j == n_steps - 1)
    def _():
        m_f, l_f, acc_f = tile(new_ref[0], q_base, True, (m_ref[:, :1], l_ref[:, :1], acc_ref[...]))
        o_ref[0] = acc_f / l_f


def paged_mqa(q, slopes_rows, pool, lane_blk, kv_w, page_table, new_tile, *, mode, t_new, dk, dv, k_off, v_off,
              scale, kbase, q_base, sel_rows=None, use_slopes=True):
    b, rows, _ = q.shape
    n_pages = page_table.shape[1]
    page = pool.shape[1]
    pg = min(PAGES_PER_STEP, n_pages)
    assert n_pages % pg == 0
    n_steps = n_pages // pg
    in_specs = [
        pl.BlockSpec((1, rows, dk), lambda bi, j, pt: (bi, 0, 0)),
        pl.BlockSpec((rows, 1), lambda bi, j, pt: (0, 0)),
    ]
    args = [q, slopes_rows]
    if mode == "select":
        in_specs.append(pl.BlockSpec((1, rows, sel_rows.shape[2]), lambda bi, j, pt: (bi, 0, 0)))
        args.append(sel_rows)
    for i in range(pg):
        in_specs.append(pl.BlockSpec(
            (1, page, kv_w), lambda bi, j, pt, i=i: (pt[bi * n_pages + j * pg + i], 0, lane_blk)))
        args.append(pool)
    in_specs.append(pl.BlockSpec((1, page, kv_w), lambda bi, j, pt: (bi, 0, 0)))
    args.append(new_tile)
    kern = functools.partial(_paged_mqa_kernel, mode=mode, pg=pg, t_new=t_new, dk=dk, dv=dv, k_off=k_off, v_off=v_off,
                             scale=scale, kbase=kbase, q_base=q_base, page=page, use_slopes=use_slopes)
    return pl.pallas_call(
        kern,
        grid_spec=pltpu.PrefetchScalarGridSpec(
            num_scalar_prefetch=1,
            grid=(b, n_steps),
            in_specs=in_specs,
            out_specs=pl.BlockSpec((1, rows, dv), lambda bi, j, pt: (bi, 0, 0)),
            scratch_shapes=[
                pltpu.VMEM((rows, LANE), F32),
                pltpu.VMEM((rows, LANE), F32),
                pltpu.VMEM((rows, dv), F32),
            ],
        ),
        out_shape=jax.ShapeDtypeStruct((b, rows, dv), F32),
        compiler_params=pltpu.CompilerParams(
            dimension_semantics=("parallel", "arbitrary"), vmem_limit_bytes=VMEM_LIMIT),
        name="paged_mqa_" + mode,
    )(page_table.reshape(-1), *args)


def _sb_decode_kernel(pt_ref, *refs, pg, t_new, kw, scale, page):
    refs = list(refs)
    q_ref = refs.pop(0)
    page_refs = [refs.pop(0) for _ in range(pg)]
    new_ref, o_ref, run_ref, acc_ref = refs
    j = pl.program_id(1)
    n_steps = pl.num_programs(1)
    rows = q_ref.shape[1]
    qb = (q_ref[0] * scale).astype(BF16)
    later = (lax.broadcasted_iota(jnp.int32, (page, page), 0) >
             lax.broadcasted_iota(jnp.int32, (page, page), 1)).astype(BF16)

    def tile(kv, is_new, carry):
        run, acc = carry
        kt = kv[:, :kw].astype(BF16)
        vt = kv[:, kw:2 * kw].astype(BF16)
        z = lax.dot_general(qb, kt, (((1,), (1,)), ((), ())), preferred_element_type=F32)
        ls = jnp.minimum(z, 0.0) - jnp.log(1.0 + jnp.exp(-jnp.abs(z)))
        l1mb = ls - z
        if is_new:
            mask = (lax.broadcasted_iota(jnp.int32, (rows, page), 1) <
                    lax.broadcasted_iota(jnp.int32, (rows, page), 0) % t_new)
            l1mb = jnp.where(mask, l1mb, 0.0)
        hi = l1mb.astype(BF16)
        lo = (l1mb - hi.astype(F32)).astype(BF16)
        aft = jnp.dot(hi, later, preferred_element_type=F32) + jnp.dot(lo, later, preferred_element_type=F32)
        att = jnp.exp(ls + aft + run)
        if is_new:
            att = jnp.where(mask, att, 0.0)
        acc = acc + jnp.dot(att.astype(BF16), vt, preferred_element_type=F32)
        return run + aft[:, :1] + l1mb[:, :1], acc

    @pl.when(j == 0)
    def _():
        run0, acc0 = tile(new_ref[0], True, (jnp.zeros((rows, 1), F32), jnp.zeros((rows, kw), F32)))
        run_ref[...] = jnp.broadcast_to(run0, run_ref.shape)
        acc_ref[...] = acc0

    carry = (run_ref[:, :1], acc_ref[...])
    for i in range(pg):
        carry = tile(page_refs[i][0], False, carry)
    run_ref[...] = jnp.broadcast_to(carry[0], run_ref.shape)
    acc_ref[...] = carry[1]

    @pl.when(j == n_steps - 1)
    def _():
        o_ref[0] = acc_ref[...]


def sb_decode(q_rows, pool, page_table, new_tile, t_new):
    b, rows, kw = q_rows.shape
    n_pages = page_table.shape[1]
    page = pool.shape[1]
    pg = min(PAGES_PER_STEP, n_pages)
    assert n_pages % pg == 0
    n_steps = n_pages // pg
    in_specs = [pl.BlockSpec((1, rows, kw), lambda bi, j, pt: (bi, 0, 0))]
    args = [q_rows]
    for i in range(pg):
        in_specs.append(pl.BlockSpec(
            (1, page, 2 * kw), lambda bi, j, pt, i=i: (pt[bi * n_pages + n_pages - 1 - (j * pg + i)], 0, 0)))
        args.append(pool)
    in_specs.append(pl.BlockSpec((1, page, 2 * kw), lambda bi, j, pt: (bi, 0, 0)))
    args.append(new_tile)
    kern = functools.partial(_sb_decode_kernel, pg=pg, t_new=t_new, kw=kw, scale=SB_HD ** -0.5, page=page)
    return pl.pallas_call(
        kern,
        grid_spec=pltpu.PrefetchScalarGridSpec(
            num_scalar_prefetch=1,
            grid=(b, n_steps),
            in_specs=in_specs,
            out_specs=pl.BlockSpec((1, rows, kw), lambda bi, j, pt: (bi, 0, 0)),
            scratch_shapes=[pltpu.VMEM((rows, LANE), F32), pltpu.VMEM((rows, kw), F32)],
        ),
        out_shape=jax.ShapeDtypeStruct((b, rows, kw), F32),
        compiler_params=pltpu.CompilerParams(
            dimension_semantics=("parallel", "arbitrary"), vmem_limit_bytes=VMEM_LIMIT),
        name="sb_decode",
    )(page_table.reshape(-1), *args)


def _nsa_cmp_decode_kernel(q_ref, slope_ref, kc_ref, vc_ref, oc_ref, sel_ref, *, t_new, heads, scale, q_base, n_sel, k_top):
    rows = q_ref.shape[1]
    nc = kc_ref.shape[1]
    lsel = sel_ref.shape[2]
    pos = q_base + lax.broadcasted_iota(jnp.int32, (rows, nc), 0) % t_new
    cmp_end = CMP_STRIDE * lax.broadcasted_iota(jnp.int32, (rows, nc), 1) + (CMP_LEN - 1)
    valid = cmp_end <= pos
    dist = (pos - cmp_end).astype(F32)
    qb = (q_ref[0] * scale).astype(BF16)
    kc = kc_ref[0].astype(BF16)
    vc = vc_ref[0].astype(BF16)
    s = lax.dot_general(qb, kc, (((1,), (1,)), ((), ())), preferred_element_type=F32) - slope_ref[:, :1] * dist
    s = jnp.where(valid, s, NEG)
    e = jnp.exp(s - jnp.max(s, axis=-1, keepdims=True))
    p = jnp.where(valid, e / jnp.sum(e, axis=-1, keepdims=True), 0.0)
    oc_ref[0] = jnp.dot(p.astype(BF16), vc, preferred_element_type=F32)
    psum = p[0:t_new]
    for h in range(1, heads):
        psum = psum + p[h * t_new:(h + 1) * t_new]
    sel_ref[0] = _select_blocks(psum, q_base, 0, t_new, nc, lsel, n_sel, k_top)


def _select_blocks(psum, pos0, row0, tq, nc, lsel, n_sel, k_top):
    c_start = CMP_STRIDE * lax.broadcasted_iota(jnp.int32, (nc, lsel), 0)
    s_start = SEL_BLOCK * lax.broadcasted_iota(jnp.int32, (nc, lsel), 1)
    overlap = ((c_start < s_start + SEL_BLOCK) & (c_start + CMP_LEN > s_start)).astype(F32)
    imp = jnp.dot(psum, overlap, preferred_element_type=F32, precision=lax.Precision.HIGHEST)
    tpos = pos0 + row0 + lax.broadcasted_iota(jnp.int32, (tq, lsel), 0)
    j = lax.broadcasted_iota(jnp.int32, (tq, lsel), 1)
    cur = tpos // SEL_BLOCK
    forced = (j == 0) | (j == cur) | (j == cur - 1)
    cand = (SEL_BLOCK * j <= tpos) & (j < n_sel)
    score = jnp.where(cand, imp + jnp.where(forced, FORCE_BONUS, 0.0), NEG)
    work = score
    chosen = jnp.zeros((tq, lsel), F32)
    for _ in range(k_top):
        mx = jnp.max(work, axis=-1, keepdims=True)
        first = jnp.min(jnp.where(work == mx, j, lsel), axis=-1, keepdims=True)
        hit = j == first
        chosen = jnp.where(hit, 1.0, chosen)
        work = jnp.where(hit, -3.0e38, work)
    return jnp.where(score > 0.5 * NEG, chosen, 0.0)


def nsa_cmp_decode(q_rows, slopes_rows, k_cmp, v_cmp, t_new, q_base, n_sel):
    b, rows, hd = q_rows.shape
    nc = k_cmp.shape[1]
    lsel = -(-n_sel // LANE) * LANE
    kern = functools.partial(_nsa_cmp_decode_kernel, t_new=t_new, heads=rows // t_new, scale=hd ** -0.5,
                             q_base=q_base, n_sel=n_sel, k_top=min(N_SEL, n_sel))
    return pl.pallas_call(
        kern,
        grid=(b,),
        in_specs=[
            pl.BlockSpec((1, rows, hd), lambda bi: (bi, 0, 0)),
            pl.BlockSpec((rows, 1), lambda bi: (0, 0)),
            pl.BlockSpec((1, nc, hd), lambda bi: (bi, 0, 0)),
            pl.BlockSpec((1, nc, hd), lambda bi: (bi, 0, 0)),
        ],
        out_specs=[
            pl.BlockSpec((1, rows, hd), lambda bi: (bi, 0, 0)),
            pl.BlockSpec((1, t_new, lsel), lambda bi: (bi, 0, 0)),
        ],
        out_shape=[jax.ShapeDtypeStruct((b, rows, hd), F32), jax.ShapeDtypeStruct((b, t_new, lsel), F32)],
        compiler_params=pltpu.CompilerParams(dimension_semantics=("parallel",), vmem_limit_bytes=VMEM_LIMIT),
        name="nsa_cmp_decode",
    )(q_rows, slopes_rows, k_cmp, v_cmp)


def _router_kernel(x_ref, g_ref, w_ref, b_ref, lg_ref, h_ref):
    x = x_ref[...]
    ms = jnp.mean(x * x, axis=-1, keepdims=True)
    h = (x * lax.rsqrt(ms + EPS)) * g_ref[...]
    h_ref[...] = h.astype(BF16)
    lg_ref[...] = jnp.dot(h, w_ref[...], preferred_element_type=F32,
                          precision=lax.Precision.HIGHEST) + b_ref[...]


def _experts_kernel(te_ref, tv_ref, xs_ref, sw_ref, w1_ref, w3_ref, w2_ref, o_ref, b1_ref, b3_ref, b2_ref):
    i = pl.program_id(0)

    @pl.when((i == 0) | (te_ref[i] != te_ref[jnp.maximum(i - 1, 0)]))
    def _():
        b1_ref[...] = w1_ref[0].astype(BF16)
        b3_ref[...] = w3_ref[0].astype(BF16)
        b2_ref[...] = w2_ref[0].astype(BF16)

    @pl.when(tv_ref[i] != 0)
    def _():
        x = xs_ref[...]
        a = jnp.dot(x, b1_ref[...], preferred_element_type=F32)
        c = jnp.dot(x, b3_ref[...], preferred_element_type=F32)
        hid = (jax.nn.silu(a) * c) * sw_ref[...]
        o_ref[...] = jnp.dot(hid.astype(BF16), b2_ref[...], preferred_element_type=F32)

    @pl.when(tv_ref[i] == 0)
    def _():
        o_ref[...] = jnp.zeros_like(o_ref)


def moe(x, g, w_rg, b_rg, w_re, b_re, w_e1, w_e3, w_e2):
    n, d = x.shape
    ne = N_GROUPS * N_EXP
    tm = _pick_tile(n, (ROW_TILE, 256, 128))
    wr = jnp.zeros((d, LANE), F32).at[:, :N_GROUPS].set(w_rg).at[:, N_GROUPS:N_GROUPS + ne].set(w_re)
    br = jnp.zeros((1, LANE), F32).at[0, :N_GROUPS].set(b_rg).at[0, N_GROUPS:N_GROUPS + ne].set(b_re)
    logits, h = pl.pallas_call(
        _router_kernel,
        grid=(n // tm,),
        in_specs=[
            pl.BlockSpec((tm, d), lambda i: (i, 0)),
            pl.BlockSpec((1, d), lambda i: (0, 0)),
            pl.BlockSpec((d, LANE), lambda i: (0, 0)),
            pl.BlockSpec((1, LANE), lambda i: (0, 0)),
        ],
        out_specs=[pl.BlockSpec((tm, LANE), lambda i: (i, 0)), pl.BlockSpec((tm, d), lambda i: (i, 0))],
        out_shape=[jax.ShapeDtypeStruct((n, LANE), F32), jax.ShapeDtypeStruct((n, d), BF16)],
        compiler_params=pltpu.CompilerParams(dimension_semantics=("parallel",), vmem_limit_bytes=VMEM_LIMIT),
        name="moe_router",
    )(x, g.astype(F32).reshape(1, d), wr, br)

    lg = logits[:, :N_GROUPS]
    le = logits[:, N_GROUPS:N_GROUPS + ne].reshape(n, N_GROUPS, N_EXP)
    pg = jax.nn.softmax(lg, axis=-1)
    gi = jnp.argmax(lg, axis=-1)
    gw = jnp.take_along_axis(pg, gi[:, None], axis=1)[:, 0]
    le_g = jnp.take_along_axis(le, gi[:, None, None], axis=1)[:, 0]
    top_v, top_i = lax.top_k(le_g, TOP_K)
    we = jax.nn.softmax(top_v, axis=-1)
    eid = (gi[:, None] * N_EXP + top_i).astype(jnp.int32)
    wt = we * gw[:, None]

    ts = EXPERT_TILE
    n_tiles = -(-(TOP_K * n) // ts) + ne
    member = (eid[:, :, None] == jnp.arange(ne)[None, None, :]).any(axis=1).astype(jnp.int32)
    cnt = member.sum(axis=0)
    rank = jnp.cumsum(member, axis=0) - member
    tiles_e = (cnt + ts - 1) // ts
    tile_end = jnp.cumsum(tiles_e)
    pad_off = (tile_end - tiles_e) * ts
    pos = pad_off[eid] + jnp.take_along_axis(rank, eid, axis=1)
    tok = jnp.broadcast_to(jnp.arange(n, dtype=jnp.int32)[:, None], (n, TOP_K))
    slot_tok = jnp.zeros((n_tiles * ts,), jnp.int32).at[pos.reshape(-1)].set(tok.reshape(-1))
    slot_w = jnp.zeros((n_tiles * ts,), F32).at[pos.reshape(-1)].set(wt.reshape(-1))
    tile_id = jnp.arange(n_tiles, dtype=jnp.int32)
    tile_e = jnp.minimum(jnp.searchsorted(tile_end, tile_id, side="right"), ne - 1).astype(jnp.int32)
    tile_v = (tile_id < tile_end[-1]).astype(jnp.int32)

    xs = jnp.take(h, slot_tok, axis=0)
    w1 = w_e1.reshape(ne, d, D_EXP)
    w3 = w_e3.reshape(ne, d, D_EXP)
    w2 = w_e2.reshape(ne, D_EXP, d)
    ys = pl.pallas_call(
        _experts_kernel,
        grid_spec=pltpu.PrefetchScalarGridSpec(
            num_scalar_prefetch=2,
            grid=(n_tiles,),
            in_specs=[
                pl.BlockSpec((ts, d), lambda i, te, tv: (i, 0)),
                pl.BlockSpec((ts, 1), lambda i, te, tv: (i, 0)),
                pl.BlockSpec((1, d, D_EXP), lambda i, te, tv: (te[i], 0, 0)),
                pl.BlockSpec((1, d, D_EXP), lambda i, te, tv: (te[i], 0, 0)),
                pl.BlockSpec((1, D_EXP, d), lambda i, te, tv: (te[i], 0, 0)),
            ],
            out_specs=pl.BlockSpec((ts, d), lambda i, te, tv: (i, 0)),
            scratch_shapes=[pltpu.VMEM((d, D_EXP), BF16), pltpu.VMEM((d, D_EXP), BF16), pltpu.VMEM((D_EXP, d), BF16)],
        ),
        out_shape=jax.ShapeDtypeStruct((n_tiles * ts, d), F32),
        compiler_params=pltpu.CompilerParams(dimension_semantics=("arbitrary",), vmem_limit_bytes=VMEM_LIMIT),
        name="moe_experts",
    )(tile_e, tile_v, xs, slot_w.reshape(-1, 1), w1, w3, w2)
    return jnp.take(ys, pos[:, 0], axis=0) + jnp.take(ys, pos[:, 1], axis=0)


def rmsnorm(x, g):
    xf = x.astype(F32)
    y = xf * lax.rsqrt(jnp.mean(xf * xf, axis=-1, keepdims=True) + EPS)
    return (y * g.astype(F32)).astype(x.dtype)


def split_cols(a, widths):
    return jnp.split(a, np.cumsum(widths)[:-1].tolist(), axis=-1)


def qblock(t):
    return min(QBLOCK, t)


def map_query_blocks(fn, arrays, block):
    t = arrays[0].shape[1]
    nb = -(-t // block)
    pad = nb * block - t

    def to_blocks(a):
        a = jnp.pad(a, [(0, 0), (0, pad)] + [(0, 0)] * (a.ndim - 2))
        a = a.reshape((a.shape[0], nb, block) + a.shape[2:])
        return jnp.moveaxis(a, 1, 0)

    xs = (jnp.arange(nb),) + tuple(to_blocks(a) for a in arrays)
    out = lax.map(lambda args: fn(*args), xs)
    out = jnp.moveaxis(out, 0, 1)
    out = out.reshape((out.shape[0], nb * block) + out.shape[3:])
    return out[:, :t]


def rope_angles(pos, dim):
    inv = ROPE_THETA ** (-jnp.arange(0, dim, 2, dtype=F32) / dim)
    ang = pos.astype(F32)[:, None] * inv[None, :]
    return jnp.cos(ang), jnp.sin(ang)


def apply_rope(x, cos, sin):
    half = x.shape[-1] // 2
    x1 = x[..., :half].astype(F32)
    x2 = x[..., half:].astype(F32)
    return jnp.concatenate([x1 * cos - x2 * sin, x1 * sin + x2 * cos], axis=-1).astype(x.dtype)


def alibi_slopes(n):
    return 2.0 ** (-8.0 * jnp.arange(1, n + 1, dtype=F32) / n)


def gather_pages(pool, page_table):
    g = pool[page_table]
    return g.reshape((page_table.shape[0], page_table.shape[1] * pool.shape[1]) + pool.shape[2:])


def mla_mixer(c_q, c_kv_raw, k_r_raw, pos, past, g_cq, w_uq, g_ckv, w_uk, w_uv):
    b, t, _ = c_q.shape
    cos, sin = rope_angles(pos, ROPE_DIM)
    q = jnp.einsum('btr,rhd->bthd', rmsnorm(c_q, g_cq), w_uq)
    q_nope = q[..., :NOPE_DIM]
    q_rope = apply_rope(q[..., NOPE_DIM:], cos[None, :, None, :], sin[None, :, None, :])
    q_lat = jnp.einsum('bthd,rhd->bthr', q_nope, w_uk)
    c_kv = rmsnorm(c_kv_raw, g_ckv)
    k_r = apply_rope(k_r_raw, cos[None], sin[None])
    new_rows = jnp.concatenate([c_kv, k_r], axis=-1)
    keys = jnp.concatenate([past, new_rows], axis=1)
    k_pos = jnp.arange(keys.shape[1])
    lat, kr = keys[..., :KV_RANK], keys[..., KV_RANK:]
    scale = (NOPE_DIM + ROPE_DIM) ** -0.5

    def blk(bi, ql, qr, qp):
        s = (jnp.einsum('bqhr,bsr->bhqs', ql, lat) + jnp.einsum('bqhd,bsd->bhqs', qr, kr)).astype(F32) * scale
        mask = k_pos[None, :] <= qp[0][:, None]
        p = jax.nn.softmax(jnp.where(mask, s, NEG), axis=-1).astype(lat.dtype)
        return jnp.einsum('bhqs,bsr->bqhr', p, lat)

    o_lat = map_query_blocks(blk, (q_lat, q_rope, pos[None]), qblock(t))
    out = jnp.einsum('bthr,rhd->bthd', o_lat, w_uv).reshape(b, t, MLA_HEADS * MLA_VDIM)
    return out, new_rows


def mlstm_mixer(q, k, v, ig, fg, og, c0, n0, m0, g_mh):
    b, t = q.shape[:2]
    dt = q.dtype
    q = q.reshape(b, t, ML_HEADS, ML_QK)
    k = k.reshape(b, t, ML_HEADS, ML_QK) * (ML_QK ** -0.5)
    v = v.reshape(b, t, ML_HEADS, ML_V)
    li = ig.astype(F32)
    lf = jax.nn.log_sigmoid(fg.astype(F32))
    L = min(ML_CHUNK, t)
    nc = -(-t // L)
    pad = nc * L - t

    def chunks(a, val=0.0):
        a = jnp.pad(a, [(0, 0), (0, pad)] + [(0, 0)] * (a.ndim - 2), constant_values=val)
        return jnp.moveaxis(a.reshape((b, nc, L) + a.shape[2:]), 1, 0)

    tri = jnp.tril(jnp.ones((L, L), dtype=bool))

    def step(carry, xs):
        c, n, m = carry
        qc, kc, vc, lic, lfc = xs
        cb = jnp.cumsum(lfc, axis=1)
        a = cb + m[:, None, :]
        d = cb[:, :, None, :] - cb[:, None, :, :] + lic[:, None, :, :]
        d = jnp.where(tri[None, :, :, None], d, NEG)
        mt = jnp.maximum(a, d.max(axis=2))
        w = jnp.exp(d - mt[:, :, None, :])
        inter = jnp.exp(a - mt)
        sc = w * jnp.einsum('bthd,bshd->btsh', qc, kc).astype(F32)
        num = jnp.einsum('btsh,bshv->bthv', sc, vc) + inter[..., None] * jnp.einsum('bhvd,bthd->bthv', c, qc)
        den = sc.sum(axis=2) + inter * jnp.einsum('bhd,bthd->bth', n, qc)
        h = num / jnp.maximum(jnp.abs(den), jnp.exp(-mt))[..., None]
        wl, il = w[:, -1], inter[:, -1]
        c = il[..., None, None] * c + jnp.einsum('bsh,bshv,bshd->bhvd', wl, vc, kc)
        n = il[..., None] * n + jnp.einsum('bsh,bshd->bhd', wl, kc)
        return (c, n, mt[:, -1]), h

    carry0 = (c0.astype(F32), n0.astype(F32), m0.astype(F32))
    (c, n, m), h = lax.scan(step, carry0, (chunks(q), chunks(k), chunks(v), chunks(li, NEG), chunks(lf)))
    h = jnp.moveaxis(h, 0, 1).reshape(b, nc * L, ML_HEADS, ML_V)[:, :t]
    h = h - h.mean(axis=-1, keepdims=True)
    h = h * lax.rsqrt(jnp.mean(h * h, axis=-1, keepdims=True) + EPS) * g_mh.astype(F32)
    out = jax.nn.sigmoid(og.astype(F32)).reshape(b, t, ML_HEADS, ML_V) * h
    return out.reshape(b, t, ML_HEADS * ML_V).astype(dt), c, n, m


def sb_mixer(q, k_new, v_new, pos, past_kv):
    b, t = q.shape[:2]
    q = q.reshape(b, t, SB_KV_HEADS, SB_HEADS // SB_KV_HEADS, SB_HD)
    new_rows = jnp.stack([k_new.reshape(b, t, SB_KV_HEADS, SB_HD), v_new.reshape(b, t, SB_KV_HEADS, SB_HD)], axis=2)
    kv = jnp.concatenate([past_kv, new_rows], axis=1)
    k, v = kv[:, :, 0], kv[:, :, 1]
    k_pos = jnp.arange(kv.shape[1])

    def blk(bi, qh, qp):
        z = jnp.einsum('bqgrd,bsgd->bgrqs', qh, k).astype(F32) * (SB_HD ** -0.5)
        mask = k_pos[None, :] < qp[0][:, None]
        l1mb = jnp.where(mask, jax.nn.log_sigmoid(-z), 0.0)
        after = lax.cumsum(l1mb, axis=z.ndim - 1, reverse=True) - l1mb
        att = jnp.where(mask, jnp.exp(jax.nn.log_sigmoid(z) + after), 0.0).astype(v.dtype)
        return jnp.einsum('bgrqs,bsgd->bqgrd', att, v)

    o = map_query_blocks(blk, (q, pos[None]), qblock(t))
    return o.reshape(b, t, SB_HEADS * SB_HD), new_rows


def nsa_mixer(q, kv_new, win_new, gate, pos, past_kv, win_prior, pe, wc1, wc2):
    b, t = q.shape[:2]
    past = past_kv.shape[1]
    s_len = past + t
    scale = NSA_HD ** -0.5
    slopes = alibi_slopes(NSA_HEADS)
    kv_all = jnp.concatenate([past_kv, kv_new], axis=1)
    n_chunk = max(-(-s_len // CMP_STRIDE), 2)
    ck = jnp.pad(kv_all[:, :, :2], ((0, 0), (0, n_chunk * CMP_STRIDE - s_len), (0, 0), (0, 0)))
    ck = ck.reshape(b, n_chunk, CMP_STRIDE, 2, NSA_HD)
    w1 = wc1.reshape(2, 2, CMP_STRIDE, NSA_HD, CMP_HID)
    proj = jnp.einsum('bcrkd,kzrdh->bczkh', ck, w1)
    pe_term = jnp.einsum('kpd,kpdh->kh', pe, wc1.reshape(2, CMP_LEN, NSA_HD, CMP_HID))
    hid = jax.nn.gelu(proj[:, :-1, 0] + proj[:, 1:, 1] + pe_term)
    cmp = jnp.einsum('bnkh,khd->bnkd', hid, wc2)
    k_cmp, v_cmp = cmp[:, :, 0], cmp[:, :, 1]
    cmp_start = CMP_STRIDE * jnp.arange(n_chunk - 1)
    cmp_end = cmp_start + CMP_LEN - 1
    n_sel = -(-s_len // SEL_BLOCK)
    sel = jnp.pad(kv_all[:, :, 2:], ((0, 0), (0, n_sel * SEL_BLOCK - s_len), (0, 0), (0, 0)))
    sel = sel.reshape(b, n_sel, SEL_BLOCK, 2, NSA_HD)
    sel_start = SEL_BLOCK * jnp.arange(n_sel)
    overlap = ((cmp_start[:, None] < sel_start[None, :] + SEL_BLOCK) & (cmp_start[:, None] + CMP_LEN > sel_start[None, :])).astype(F32)
    k_top = min(N_SEL, n_sel)
    bidx = jnp.arange(b)[:, None, None]
    pw = win_prior.shape[1]
    qb = qblock(t)
    nb = -(-t // qb)
    band = jnp.concatenate([jnp.zeros((b, WINDOW - pw, 2, NSA_HD), win_new.dtype), win_prior, win_new, jnp.zeros((b, nb * qb - t, 2, NSA_HD), win_new.dtype)], axis=1)
    band_pos = past - WINDOW + jnp.arange(WINDOW + nb * qb)

    def blk(bi, qh, gh, qp):
        tq = qp[0]
        valid_c = cmp_end[None, :] <= tq[:, None]
        dist_c = (tq[:, None] - cmp_end[None, :]).astype(F32)
        s_c = jnp.einsum('bqhd,bnd->bhqn', qh, k_cmp).astype(F32) * scale - slopes[:, None, None] * dist_c
        p_c = jnp.where(valid_c, jax.nn.softmax(jnp.where(valid_c, s_c, NEG), axis=-1), 0.0)
        o_c = jnp.einsum('bhqn,bnd->bqhd', p_c.astype(v_cmp.dtype), v_cmp)
        imp = jnp.einsum('bhqn,nj->bqj', p_c, overlap)
        cur = tq // SEL_BLOCK
        j = jnp.arange(n_sel)
        forced = (j[None, :] == 0) | (j[None, :] == cur[:, None]) | (j[None, :] == cur[:, None] - 1)
        cand = sel_start[None, :] <= tq[:, None]
        score = jnp.where(cand[None], imp + jnp.where(forced, FORCE_BONUS, 0.0)[None], NEG)
        top_s, idx = lax.top_k(score, k_top)
        kv_sel = sel[bidx, idx]
        kpos = idx[..., None] * SEL_BLOCK + jnp.arange(SEL_BLOCK)
        ok = (top_s > 0.5 * NEG)[..., None] & (kpos <= tq[None, :, None, None])
        dist_s = (tq[None, :, None, None] - kpos).astype(F32)
        s_s = jnp.einsum('bqhd,bqkld->bqhkl', qh, kv_sel[..., 0, :]).astype(F32) * scale - slopes[None, None, :, None, None] * dist_s[:, :, None]
        s_s = jnp.where(ok[:, :, None], s_s, NEG)
        p_s = jax.nn.softmax(s_s.reshape(s_s.shape[:3] + (-1,)), axis=-1).reshape(s_s.shape)
        o_s = jnp.einsum('bqhkl,bqkld->bqhd', p_s.astype(kv_sel.dtype), kv_sel[..., 1, :])
        bw = lax.dynamic_slice_in_dim(band, bi * qb, qb + WINDOW, axis=1)
        bp = lax.dynamic_slice_in_dim(band_pos, bi * qb, qb + WINDOW, axis=0)
        dist_w = tq[:, None] - bp[None, :]
        ok_w = (bp[None, :] >= 0) & (dist_w >= 0) & (dist_w < WINDOW)
        s_w = jnp.einsum('bqhd,bsd->bhqs', qh, bw[:, :, 0]).astype(F32) * scale - slopes[:, None, None] * dist_w.astype(F32)
        p_w = jax.nn.softmax(jnp.where(ok_w, s_w, NEG), axis=-1)
        o_w = jnp.einsum('bhqs,bsd->bqhd', p_w.astype(bw.dtype), bw[:, :, 1])
        g = jax.nn.sigmoid(gh.astype(F32))
        return (g[..., 0:1] * o_c + g[..., 1:2] * o_s + g[..., 2:3] * o_w).astype(qh.dtype)

    o = map_query_blocks(blk, (q, gate, pos[None]), qb)
    new_win = jnp.concatenate([win_prior, win_new], axis=1)[:, -min(WINDOW, pw + t):]
    return o.reshape(b, t, NSA_HEADS * NSA_HD), kv_new, new_win


def mem_kv_rows(mem, g, wk, wv):
    b, nm, d = mem.shape
    kv = rms_matmul(mem.reshape(b * nm, d), jnp.concatenate([wk, wv], axis=1), g=g)
    hd = X_HEADS * X_HD
    k = kv[:, :hd].reshape(b, nm, X_HEADS, X_HD)
    v = kv[:, hd:].reshape(b, nm, X_HEADS, X_HD)
    return jnp.stack([k, v], axis=2)


def mem_attend(x, g, mkv, wq, wo):
    b, t, d = x.shape
    q = rms_matmul(x.reshape(b * t, d), wq, g=g).reshape(b, t, X_HEADS, X_HD)
    s = jnp.einsum('bthd,bmhd->bhtm', q, mkv[:, :, 0]).astype(F32) * (X_HD ** -0.5)
    p = jax.nn.softmax(s, axis=-1).astype(mkv.dtype)
    o = jnp.einsum('bhtm,bmhd->bthd', p, mkv[:, :, 1]).reshape(b * t, X_HEADS * X_HD)
    return rms_matmul(o, wo, res=x.reshape(b * t, d)).reshape(b, t, d)


def odd_mixers_prompt(proj, b, t, pe, wc1, wc2):
    n = b * t
    sb_w = SB_HEADS * SB_HD
    kv_w = SB_KV_HEADS * SB_HD
    q_col = sb_w + 2 * kv_w
    kv_col = q_col + NSA_HEADS * NSA_HD
    win_col = kv_col + 4 * NSA_HD
    gate_col = win_col + 2 * NSA_HD
    o_sb = sb_attn_prompt(proj, b, t, 0, sb_w, sb_w + kv_w)
    sb_rows = proj[:, sb_w:sb_w + 2 * kv_w].reshape(b, t, 2, SB_KV_HEADS, SB_HD)
    nsa_rows = proj[:, kv_col:win_col].reshape(b, t, 4, NSA_HD)
    win_rows = proj[:, win_col:gate_col].reshape(b, t, 2, NSA_HD)
    ck = proj[:, kv_col:kv_col + 2 * NSA_HD].reshape(b, t // CMP_STRIDE, CMP_STRIDE, 2, NSA_HD)
    k_cmp, v_cmp = nsa_compress(ck, pe, wc1, wc2)
    o_c, sel = nsa_cmp_select(proj, q_col, k_cmp, v_cmp, b, t, 0, t // SEL_BLOCK)
    slopes = _alibi_slopes_np(NSA_HEADS)
    common = dict(heads=NSA_HEADS, dk=NSA_HD, dv=NSA_HD, k_off=0, v_off=NSA_HD, scale=NSA_HD ** -0.5, slopes=slopes)
    o_s = mqa_flash(proj, q_col, proj, kv_col + 2 * NSA_HD, 2 * NSA_HD, b, t, mode="select", sel=sel, **common)
    o_w = mqa_flash(proj, q_col, proj, win_col, 2 * NSA_HD, b, t, mode="window", **common)
    g = jax.nn.sigmoid(proj[:, gate_col:gate_col + 3 * NSA_HEADS]).reshape(n, NSA_HEADS, 3)
    sh = (n, NSA_HEADS, NSA_HD)
    o_nsa = (g[..., 0:1] * o_c.reshape(sh) + g[..., 1:2] * o_s.reshape(sh) + g[..., 2:3] * o_w.reshape(sh)).reshape(n, -1)
    return o_sb, sb_rows, o_nsa, nsa_rows, win_rows[:, -min(WINDOW, t):]


def _pad_rows(a, rows):
    return jnp.pad(a, ((0, 0), (0, rows - a.shape[1]), (0, 0)))


def _rows_head_major(a, b, t, heads):
    w = a.shape[1] // heads
    return jnp.transpose(a.reshape(b, t, heads, w), (0, 2, 1, 3)).reshape(b, heads * t, w)


def _rows_token_major(a, b, t, heads):
    w = a.shape[2]
    return jnp.transpose(a.reshape(b, heads, t, w), (0, 2, 1, 3)).reshape(b * t, heads * w)


def odd_mixers_sample(proj, b, t, past, page_table, pool_sb, pool_nsa, win_prior, pe, wc1, wc2):
    n = b * t
    page = pool_sb.shape[1]
    sb_w = SB_HEADS * SB_HD
    kv_w = SB_KV_HEADS * SB_HD
    rep = SB_HEADS // SB_KV_HEADS
    q_col = sb_w + 2 * kv_w
    kv_col = q_col + NSA_HEADS * NSA_HD
    win_col = kv_col + 4 * NSA_HD
    gate_col = win_col + 2 * NSA_HD
    assert past % page == 0 and past % SEL_BLOCK == 0 and t <= CMP_STRIDE and t <= page

    q = jnp.transpose(proj[:, :sb_w].reshape(b, t, SB_KV_HEADS, rep, SB_HD), (0, 2, 3, 1, 4))
    eye = jnp.eye(SB_KV_HEADS, dtype=F32)
    q_rows = (q[:, :, :, :, None, :] * eye[None, :, None, None, :, None]).reshape(b, SB_HEADS * t, kv_w)
    new_sk = _pad_rows(proj[:, sb_w:sb_w + kv_w].reshape(b, t, kv_w), page)
    new_sv = _pad_rows(proj[:, sb_w + kv_w:sb_w + 2 * kv_w].reshape(b, t, kv_w), page)
    o = sb_paged(q_rows, jnp.transpose(pool_sb, (0, 2, 3, 4, 1)), page_table, new_sk, new_sv, t, pg=16)
    o = o.reshape(b, SB_KV_HEADS, rep, t, SB_KV_HEADS, SB_HD)
    o = jnp.stack([o[:, g, :, :, g] for g in range(SB_KV_HEADS)], axis=1)
    o_sb = jnp.transpose(o, (0, 3, 1, 2, 4)).reshape(n, sb_w)
    sb_rows = proj[:, sb_w:sb_w + 2 * kv_w].reshape(b, t, 2, SB_KV_HEADS, SB_HD)

    nsa_rows = proj[:, kv_col:win_col].reshape(b, t, 4, NSA_HD)
    win_new = proj[:, win_col:gate_col].reshape(b, t, 2, NSA_HD)
    n_pages = page_table.shape[1]
    w_rows = _cmp_weight_halves(wc1).reshape(2, CMP_STRIDE, NSA_HD, 2 * CMP_HID).astype(BF16)
    pool_t = jnp.transpose(pool_nsa, (0, 2, 3, 1))
    pr = nsa_chunk_proj(pool_t, page_table, w_rows, pg=16)
    k_cmp, v_cmp = nsa_compress_tail([pr[:, 0], pr[:, 1]], pe, wc1, wc2)
    slopes_rows = jnp.repeat(jnp.asarray(_alibi_slopes_np(NSA_HEADS), F32), t).reshape(NSA_HEADS * t, 1)
    q_nsa = _rows_head_major(proj[:, q_col:kv_col], b, t, NSA_HEADS)
    n_sel = -(-(past + t) // SEL_BLOCK)
    o_c, sel = nsa_cmp_decode(q_nsa, slopes_rows, k_cmp, v_cmp, t, past, n_sel)
    common = dict(t_new=t, scale=NSA_HD ** -0.5, q_base=past)

    def new_cols(c0):
        return _pad_rows(proj[:, c0:c0 + NSA_HD].reshape(b, t, NSA_HD), page)

    per_page = page // SEL_BLOCK
    picked = sel[:, :, :n_pages * per_page].reshape(b, t, n_pages, per_page).sum(axis=(1, 3)) > 0.5
    cnt = picked.sum(axis=1).astype(jnp.int32)
    order = jnp.argsort(jnp.logical_not(picked), axis=1, stable=True).astype(jnp.int32)
    last = jnp.take_along_axis(order, jnp.maximum(cnt - 1, 0)[:, None], axis=1)
    vis = jnp.where(jnp.arange(n_pages)[None, :] < cnt[:, None], order, last)
    kv_pair = lambda r: (r[0, 0], r[0, 1])
    o_s = paged_attn(q_nsa, slopes_rows, pool_t, kv_pair, NSA_HD, page_table,
                     new_cols(kv_col + 2 * NSA_HD), new_cols(kv_col + 3 * NSA_HD), mode="select", kbase=0,
                     sel_rows=jnp.tile(sel, (1, NSA_HEADS, 1)), visit=(vis, cnt), pg=16,
                     pool_block=((2, NSA_HD, page), (1, 0, 0)), **common)
    pw = win_prior.shape[1]
    assert pw % page == 0 and pw <= past
    win_pool = jnp.transpose(win_prior.reshape(b, pw // page, page, 2, NSA_HD), (0, 1, 3, 4, 2))
    win_pool = win_pool.reshape(b * (pw // page), 2, NSA_HD, page)
    win_pt = jnp.arange(b * (pw // page), dtype=jnp.int32).reshape(b, pw // page)
    o_w = paged_attn(q_nsa, slopes_rows, win_pool, kv_pair, NSA_HD, win_pt,
                     new_cols(win_col), new_cols(win_col + NSA_HD), mode="window", kbase=past - pw, **common)
    g = jax.nn.sigmoid(proj[:, gate_col:gate_col + 3 * NSA_HEADS]).reshape(n, NSA_HEADS, 3)
    sh = (n, NSA_HEADS, NSA_HD)
    o_nsa = (g[..., 0:1] * _rows_token_major(o_c, b, t, NSA_HEADS).reshape(sh)
             + g[..., 1:2] * _rows_token_major(o_s, b, t, NSA_HEADS).reshape(sh)
             + g[..., 2:3] * _rows_token_major(o_w, b, t, NSA_HEADS).reshape(sh)).reshape(n, -1)
    win = jnp.concatenate([win_prior, win_new], axis=1)[:, -min(WINDOW, pw + t):]
    return o_sb, sb_rows, o_nsa, nsa_rows, win


def mla_attend(c_q, c_kv_raw, k_r_raw, b, t, start, g_cq, w_uq, g_ckv, w_uk, w_uv, paged=None):
    n = b * t
    pos = start + jnp.arange(t, dtype=jnp.int32)
    cos, sin = rope_angles(pos, ROPE_DIM)
    cos = jnp.tile(cos, (b, 1))
    sin = jnp.tile(sin, (b, 1))
    q = rms_matmul(c_q, w_uq.reshape(Q_RANK, -1), g=g_cq).reshape(n, MLA_HEADS, NOPE_DIM + ROPE_DIM)
    q_rope = apply_rope(q[..., NOPE_DIM:], cos[:, None, :], sin[:, None, :])
    q_lat = jnp.einsum('nhd,rhd->nhr', q[..., :NOPE_DIM], w_uk)
    new_rows = jnp.concatenate([rmsnorm(c_kv_raw, g_ckv), apply_rope(k_r_raw, cos, sin)], axis=-1)
    qf = jnp.concatenate([q_lat, q_rope], axis=-1).reshape(n, -1)
    dk = KV_RANK + ROPE_DIM
    scale = (NOPE_DIM + ROPE_DIM) ** -0.5
    if paged is None:
        o_lat = mqa_flash(qf, 0, new_rows, 0, dk, b, t, mode="causal", heads=MLA_HEADS, dk=dk, dv=KV_RANK,
                          k_off=0, v_off=0, scale=scale)
    else:
        pool, page_table = paged
        page = pool.shape[1]
        new_k = _pad_rows(new_rows.reshape(b, t, dk), page)
        o = paged_attn(_rows_head_major(qf, b, t, MLA_HEADS), jnp.zeros((MLA_HEADS * t, 1), F32),
                       jnp.transpose(pool, (0, 2, 1)),
                       lambda r: (r[0], r[0, :KV_RANK, :]), KV_RANK, page_table, new_k, new_k[:, :, :KV_RANK],
                       mode="causal", t_new=t, scale=scale, kbase=0, q_base=start, use_slopes=False, pg=32)
        o_lat = _rows_token_major(o, b, t, MLA_HEADS)
    out = jnp.einsum('nhr,rhd->nhd', o_lat.reshape(n, MLA_HEADS, KV_RANK), w_uv).reshape(n, MLA_HEADS * MLA_VDIM)
    return out, new_rows


def trunk(x, start, ml_state, mem_kv, p, caches=None):
    b, t, d = x.shape
    depth = p['g_mix'].shape[0]
    mla_rows, ml_c, ml_n, ml_m, sb_rows, nsa_rows, wins = [], [], [], [], [], [], []
    for l in range(depth):
        e = l // 2
        x2 = x.reshape(b * t, d)
        if l % 2 == 0:
            proj = rms_matmul(x2, p['w_in_even'][e], g=p['g_mix'][l])
            c_q, c_kv, k_r, mq, mk, mv, mi, mf, mo = split_cols(proj, EVEN_COLS)
            paged = None if caches is None else (caches['mla'][e], caches['page_table'])
            o_a, rows = mla_attend(c_q, c_kv, k_r, b, t, start, p['g_cq'][e], p['w_uq'][e], p['g_ckv'][e], p['w_uk'][e], p['w_uv'][e], paged=paged)
            rows = rows.reshape(b, t, -1)
            mq, mk, mv, mi, mf, mo = (a.reshape(b, t, -1) for a in (mq, mk, mv, mi, mf, mo))
            c0, n0, m0 = ml_state[e]
            o_b, c, n, m = mlstm_mixer(mq, mk, mv, mi + p['b_ml_i'][e], mf + p['b_ml_f'][e], mo, c0, n0, m0, p['g_mh'][e])
            mla_rows.append(rows)
            ml_c.append(c)
            ml_n.append(n)
            ml_m.append(m)
            mix = jnp.concatenate([o_a, o_b.reshape(b * t, -1)], axis=-1)
            x = rms_matmul(mix, p['w_out_even'][e], res=x2).reshape(b, t, d)
        else:
            proj = rms_matmul(x2, p['w_in_odd'][e], g=p['g_mix'][l], keep_pad=True)
            if caches is None:
                o_c, srows, o_d, nrows, win = odd_mixers_prompt(proj, b, t, p['nsa_pe'][e], p['nsa_wc1'][e], p['nsa_wc2'][e])
            else:
                o_c, srows, o_d, nrows, win = odd_mixers_sample(
                    proj, b, t, start, caches['page_table'], caches['sb'][e], caches['nsa'][e], caches['win'][e],
                    p['nsa_pe'][e], p['nsa_wc1'][e], p['nsa_wc2'][e])
            sb_rows.append(srows)
            nsa_rows.append(nrows)
            wins.append(win)
            mix = jnp.concatenate([o_c, o_d], axis=-1)
            x = rms_matmul(mix, p['w_out_odd'][e], res=x2).reshape(b, t, d)
        x = mem_attend(x, p['g_xattn'][l], mem_kv[l], p['w_xq'][l], p['w_xo'][l])
        x2 = x.reshape(b * t, d)
        x = (x2 + moe(x2, p['g_ffn'][l], p['w_rg'][l], p['b_rg'][l], p['w_re'][l], p['b_re'][l], p['w_e1'][l], p['w_e3'][l], p['w_e2'][l])).reshape(b, t, d)
    y = rmsnorm(x, p['g_final'])
    return y, mla_rows, ml_c, ml_n, ml_m, sb_rows, nsa_rows, wins


def kernel(x_prompt, x_sample, mem_prompt, cache_mla, state_mlstm_c, state_mlstm_n, state_mlstm_m, cache_sb_kv, cache_nsa_kv, state_nsa_win, cache_mem_kv, page_table, g_mix, w_in_even, b_ml_i, b_ml_f, g_cq, w_uq, g_ckv, w_uk, w_uv, g_mh, w_out_even, w_in_odd, nsa_pe, nsa_wc1, nsa_wc2, w_out_odd, g_xattn, g_memnorm, w_xq, w_xk, w_xv, w_xo, g_ffn, w_rg, b_rg, w_re, b_re, w_e1, w_e3, w_e2, g_final):
    p = dict(g_mix=g_mix, w_in_even=w_in_even, b_ml_i=b_ml_i, b_ml_f=b_ml_f, g_cq=g_cq, w_uq=w_uq, g_ckv=g_ckv, w_uk=w_uk, w_uv=w_uv, g_mh=g_mh, w_out_even=w_out_even, w_in_odd=w_in_odd, nsa_pe=nsa_pe, nsa_wc1=nsa_wc1, nsa_wc2=nsa_wc2, w_out_odd=w_out_odd, g_xattn=g_xattn, w_xq=w_xq, w_xo=w_xo, g_ffn=g_ffn, w_rg=w_rg, b_rg=b_rg, w_re=w_re, b_re=b_re, w_e1=w_e1, w_e3=w_e3, w_e2=w_e2, g_final=g_final)
    dt = x_prompt.dtype
    bp = x_prompt.shape[0]
    depth = g_mix.shape[0]
    n_even = (depth + 1) // 2
    n_odd = depth // 2
    mem_kv_list_p = [mem_kv_rows(mem_prompt, g_memnorm[l], w_xk[l], w_xv[l]) for l in range(depth)]
    y_prompt, mla_p, c_p, n_p, m_p, sb_p, nsa_p, win_pl = trunk(
        x_prompt, 0,
        [(jnp.zeros((bp, ML_HEADS, ML_V, ML_QK), F32), jnp.zeros((bp, ML_HEADS, ML_QK), F32), jnp.zeros((bp, ML_HEADS), F32)) for _ in range(n_even)],
        mem_kv_list_p, p)
    past_len = page_table.shape[1] * cache_mla.shape[2]
    caches = dict(page_table=page_table, mla=[cache_mla[e] for e in range(n_even)],
                  sb=[cache_sb_kv[e] for e in range(n_odd)], nsa=[cache_nsa_kv[e] for e in range(n_odd)],
                  win=[state_nsa_win[e] for e in range(n_odd)])
    y_sample, mla_s, c_s, n_s, m_s, sb_s, nsa_s, win_sl = trunk(
        x_sample, past_len,
        [(state_mlstm_c[e], state_mlstm_n[e], state_mlstm_m[e]) for e in range(n_even)],
        [cache_mem_kv[l] for l in range(depth)], p, caches=caches)
    return (y_prompt, y_sample, jnp.stack(mla_p), jnp.stack(mla_s), jnp.stack(c_p), jnp.stack(c_s),
            jnp.stack(n_p), jnp.stack(n_s), jnp.stack(m_p), jnp.stack(m_s), jnp.stack(sb_p), jnp.stack(sb_s),
            jnp.stack(nsa_p), jnp.stack(nsa_s), jnp.stack(win_pl), jnp.stack(win_sl), jnp.stack(mem_kv_list_p))
```

```python
import functools

import jax
import jax.numpy as jnp
import numpy as np
from jax import lax
from jax.experimental import pallas as pl
from jax.experimental.pallas import tpu as pltpu

F32 = jnp.float32
BF16 = jnp.bfloat16

D_MODEL = 2048
QBLOCK = 128
NEG = -1e30
EPS = 1e-6
MLA_HEADS = 8
Q_RANK = 512
KV_RANK = 256
NOPE_DIM = 128
ROPE_DIM = 64
MLA_VDIM = 128
ROPE_THETA = 10000.0
ML_HEADS = 4
ML_QK = 128
ML_V = 256
ML_CHUNK = 64
SB_HEADS = 8
SB_KV_HEADS = 4
SB_HD = 64
NSA_HEADS = 16
NSA_HD = 64
CMP_STRIDE = 16
CMP_LEN = 2 * CMP_STRIDE
CMP_HID = 128
SEL_BLOCK = 64
N_SEL = 16
WINDOW = 512
FORCE_BONUS = 1000.0
X_HEADS = 4
X_HD = 128
N_GROUPS = 4
N_EXP = 8
TOP_K = 2
D_EXP = 512

EVEN_COLS = (Q_RANK, KV_RANK, ROPE_DIM, ML_HEADS * ML_QK, ML_HEADS * ML_QK, ML_HEADS * ML_V, ML_HEADS, ML_HEADS, ML_HEADS * ML_V)
ODD_COLS = (SB_HEADS * SB_HD, SB_KV_HEADS * SB_HD, SB_KV_HEADS * SB_HD, NSA_HEADS * NSA_HD, 4 * NSA_HD, 2 * NSA_HD, 3 * NSA_HEADS)

LANE = 128
VMEM_LIMIT = 56 * 1024 * 1024
ROW_TILE = 512
EXPERT_TILE = 256


def _pick_tile(n, candidates):
    for c in candidates:
        if n % c == 0:
            return c
    raise ValueError(f"no tile in {candidates} divides {n}")


def _rms_matmul_kernel(*refs, do_norm, has_bias, has_res, act, pre_act):
    x_ref, g_ref, w_ref = refs[:3]
    rest = list(refs[3:])
    b_ref = rest.pop(0) if has_bias else None
    r_ref = rest.pop(0) if has_res else None
    o_ref, xn_ref = rest

    @pl.when(pl.program_id(1) == 0)
    def _():
        x = x_ref[...].astype(F32)
        if do_norm:
            ms = jnp.mean(x * x, axis=-1, keepdims=True)
            x = (x * lax.rsqrt(ms + EPS)) * g_ref[...]
        if pre_act == "gelu":
            x = jax.nn.gelu(x)
        xn_ref[...] = x.astype(BF16)

    acc = jnp.dot(xn_ref[...], w_ref[...], preferred_element_type=F32)
    if has_bias:
        acc = acc + b_ref[...]
    if act == "gelu":
        acc = jax.nn.gelu(acc)
    if has_res:
        acc = acc + r_ref[...]
    o_ref[...] = acc


def rms_matmul(x, w, g=None, res=None, bias=None, act=None, pre_act=None, keep_pad=False):
    n, k = x.shape
    m = w.shape[1]
    mp = -(-m // LANE) * LANE
    wb = w.astype(BF16)
    if mp != m:
        wb = jnp.pad(wb, ((0, 0), (0, mp - m)))
        if res is not None:
            res = jnp.pad(res, ((0, 0), (0, mp - m)))
        if bias is not None:
            bias = jnp.pad(bias, ((0, mp - m),))
    tm = _pick_tile(n, (ROW_TILE, 256, 128, 64, 32, 16, 8))
    tn = _pick_tile(mp, (512, 384, 256, 128))
    do_norm = g is not None
    gg = (g if do_norm else jnp.ones((k,), F32)).astype(F32).reshape(1, k)
    in_specs = [
        pl.BlockSpec((tm, k), lambda i, j: (i, 0)),
        pl.BlockSpec((1, k), lambda i, j: (0, 0)),
        pl.BlockSpec((k, tn), lambda i, j: (0, j)),
    ]
    args = [x, gg, wb]
    if bias is not None:
        in_specs.append(pl.BlockSpec((1, tn), lambda i, j: (0, j)))
        args.append(bias.astype(F32).reshape(1, mp))
    if res is not None:
        in_specs.append(pl.BlockSpec((tm, tn), lambda i, j: (i, j)))
        args.append(res)
    out = pl.pallas_call(
        functools.partial(_rms_matmul_kernel, do_norm=do_norm, has_bias=bias is not None,
                          has_res=res is not None, act=act, pre_act=pre_act),
        grid=(n // tm, mp // tn),
        in_specs=in_specs,
        out_specs=pl.BlockSpec((tm, tn), lambda i, j: (i, j)),
        out_shape=jax.ShapeDtypeStruct((n, mp), F32),
        scratch_shapes=[pltpu.VMEM((tm, k), BF16)],
        compiler_params=pltpu.CompilerParams(
            dimension_semantics=("parallel", "arbitrary"), vmem_limit_bytes=VMEM_LIMIT),
        name="rms_matmul",
    )(*args)
    return out if (keep_pad or mp == m) else out[:, :m]


ATT_TILE = 256


def _sb_prompt_kernel(q_ref, k_ref, v_ref, o_ref, qs_ref, acc_ref, run_ref, *, tq, groups, rep, hd, scale):
    qi = pl.program_id(1)
    kk = pl.program_id(2)
    nk = pl.num_programs(2)

    @pl.when(kk == 0)
    def _():
        for gi in range(groups):
            for ri in range(rep):
                c0 = (gi * rep + ri) * hd
                qs_ref[gi, ri * tq:(ri + 1) * tq, :] = (q_ref[:, c0:c0 + hd] * scale).astype(BF16)
        acc_ref[...] = jnp.zeros_like(acc_ref)
        run_ref[...] = jnp.zeros_like(run_ref)

    @pl.when(kk <= qi)
    def _():
        rows = rep * tq
        row_t = lax.broadcasted_iota(jnp.int32, (rows, tq), 0) % tq
        col = lax.broadcasted_iota(jnp.int32, (rows, tq), 1)
        mask = (col < row_t) | (kk > 0)
        later = (lax.broadcasted_iota(jnp.int32, (tq, tq), 0) >
                 lax.broadcasted_iota(jnp.int32, (tq, tq), 1)).astype(BF16)
        for gi in range(groups):
            kt = k_ref[:, gi * hd:(gi + 1) * hd].astype(BF16)
            vt = v_ref[:, gi * hd:(gi + 1) * hd].astype(BF16)
            z = lax.dot_general(qs_ref[gi], kt, (((1,), (1,)), ((), ())), preferred_element_type=F32)
            ls = jnp.minimum(z, 0.0) - jnp.log(1.0 + jnp.exp(-jnp.abs(z)))
            l1mb = jnp.where(mask, ls - z, 0.0)
            hi = l1mb.astype(BF16)
            lo = (l1mb - hi.astype(F32)).astype(BF16)
            aft = jnp.dot(hi, later, preferred_element_type=F32) + jnp.dot(lo, later, preferred_element_type=F32)
            run = run_ref[gi][:, :1]
            att = jnp.where(mask, jnp.exp(ls + aft + run), 0.0)
            acc_ref[gi] += jnp.dot(att.astype(BF16), vt, preferred_element_type=F32)
            run_ref[gi] = jnp.broadcast_to(run + aft[:, :1] + l1mb[:, :1], (rows, LANE))

    @pl.when(kk == nk - 1)
    def _():
        for gi in range(groups):
            for ri in range(rep):
                c0 = (gi * rep + ri) * hd
                o_ref[:, c0:c0 + hd] = acc_ref[gi, ri * tq:(ri + 1) * tq, :]


def sb_attn_prompt(proj, b, t, q_col, k_col, v_col):
    tq = ATT_TILE
    nq = t // tq
    qw = SB_HEADS * SB_HD
    kw = SB_KV_HEADS * SB_HD
    rep = SB_HEADS // SB_KV_HEADS
    assert t % tq == 0 and q_col % qw == 0 and k_col % kw == 0 and v_col % kw == 0
    kern = functools.partial(_sb_prompt_kernel, tq=tq, groups=SB_KV_HEADS, rep=rep, hd=SB_HD, scale=SB_HD ** -0.5)
    return pl.pallas_call(
        kern,
        grid=(b, nq, nq),
        in_specs=[
            pl.BlockSpec((tq, qw), lambda bi, qi, kk: (bi * nq + qi, q_col // qw)),
            pl.BlockSpec((tq, kw), lambda bi, qi, kk: (bi * nq + jnp.maximum(qi - kk, 0), k_col // kw)),
            pl.BlockSpec((tq, kw), lambda bi, qi, kk: (bi * nq + jnp.maximum(qi - kk, 0), v_col // kw)),
        ],
        out_specs=pl.BlockSpec((tq, qw), lambda bi, qi, kk: (bi * nq + qi, 0)),
        out_shape=jax.ShapeDtypeStruct((b * t, qw), F32),
        scratch_shapes=[
            pltpu.VMEM((SB_KV_HEADS, rep * tq, SB_HD), BF16),
            pltpu.VMEM((SB_KV_HEADS, rep * tq, SB_HD), F32),
            pltpu.VMEM((SB_KV_HEADS, rep * tq, LANE), F32),
        ],
        compiler_params=pltpu.CompilerParams(
            dimension_semantics=("parallel", "parallel", "arbitrary"), vmem_limit_bytes=VMEM_LIMIT),
        name="sb_attn_prompt",
    )(proj, proj, proj)


def _alibi_slopes_np(n):
    return [float(v) for v in (np.float32(2.0) ** (np.float32(-8.0) * np.arange(1, n + 1, dtype=np.float32) / np.float32(n)))]


def _mqa_flash_kernel(*refs, mode, heads, dk, dv, k_off, v_off, tq, scale, slopes, window, sel_block, nks):
    if mode == "select":
        q_ref, kv_ref, sel_ref, o_ref, qs_ref, m_ref, l_ref, acc_ref = refs
    else:
        q_ref, kv_ref, o_ref, qs_ref, m_ref, l_ref, acc_ref = refs
    qi = pl.program_id(1)
    kk = pl.program_id(2)
    kj = qi - (nks - 1) + kk if mode == "window" else kk
    prescale = float(np.log2(scale)).is_integer()

    @pl.when(kk == 0)
    def _():
        for h in range(heads):
            qh = q_ref[:, h * dk:(h + 1) * dk]
            qs_ref[h] = (qh * scale if prescale else qh).astype(BF16)
        m_ref[...] = jnp.full_like(m_ref, NEG)
        l_ref[...] = jnp.zeros_like(l_ref)
        acc_ref[...] = jnp.zeros_like(acc_ref)

    active = (kj >= 0) if mode == "window" else (kj <= qi)

    @pl.when(active)
    def _():
        q_pos = qi * tq + lax.broadcasted_iota(jnp.int32, (tq, tq), 0)
        k_pos = kj * tq + lax.broadcasted_iota(jnp.int32, (tq, tq), 1)
        dist_i = q_pos - k_pos
        mask = dist_i >= 0
        if mode == "window":
            mask = mask & (dist_i < window)
        if mode == "select":
            nblk = sel_ref.shape[1]
            blk_of_key = (kj * tq + lax.broadcasted_iota(jnp.int32, (nblk, tq), 1)) // sel_block
            expand = (lax.broadcasted_iota(jnp.int32, (nblk, tq), 0) == blk_of_key).astype(BF16)
            chosen = jnp.dot(sel_ref[...].astype(BF16), expand, preferred_element_type=F32)
            mask = mask & (chosen > 0.5)
        dist = dist_i.astype(F32)
        kt = kv_ref[:, k_off:k_off + dk].astype(BF16)
        vt = kv_ref[:, v_off:v_off + dv].astype(BF16)
        for h in range(heads):
            s = lax.dot_general(qs_ref[h], kt, (((1,), (1,)), ((), ())), preferred_element_type=F32)
            if not prescale:
                s = s * scale
            if slopes is not None:
                s = s - slopes[h] * dist
            s = jnp.where(mask, s, NEG)
            m_old = m_ref[h][:, :1]
            m_new = jnp.maximum(m_old, jnp.max(s, axis=-1, keepdims=True))
            alpha = jnp.exp(m_old - m_new)
            p = jnp.exp(s - m_new)
            l_ref[h] = jnp.broadcast_to(alpha * l_ref[h][:, :1] + jnp.sum(p, axis=-1, keepdims=True), (tq, LANE))
            acc_ref[h] = alpha * acc_ref[h] + jnp.dot(p.astype(BF16), vt, preferred_element_type=F32)
            m_ref[h] = jnp.broadcast_to(m_new, (tq, LANE))

    @pl.when(kk == nks - 1)
    def _():
        for h in range(heads):
            o_ref[:, h * dv:(h + 1) * dv] = acc_ref[h] / l_ref[h][:, :1]


def mqa_flash(q, q_col, kv, kv_col, kv_w, b, t, *, mode, heads, dk, dv, k_off, v_off, scale, slopes=None, sel=None):
    tq = ATT_TILE
    nq = t // tq
    qw = heads * dk
    assert t % tq == 0 and q_col % qw == 0 and kv_col % kv_w == 0
    nks = (WINDOW // tq + 1) if mode == "window" else nq
    if mode == "window":
        kv_idx = lambda bi, qi, kk: (bi * nq + jnp.maximum(qi - (nks - 1) + kk, 0), kv_col // kv_w)
    else:
        kv_idx = lambda bi, qi, kk: (bi * nq + jnp.minimum(kk, qi), kv_col // kv_w)
    in_specs = [
        pl.BlockSpec((tq, qw), lambda bi, qi, kk: (bi * nq + qi, q_col // qw)),
        pl.BlockSpec((tq, kv_w), kv_idx),
    ]
    args = [q, kv]
    if mode == "select":
        in_specs.append(pl.BlockSpec((tq, sel.shape[1]), lambda bi, qi, kk: (bi * nq + qi, 0)))
        args.append(sel)
    kern = functools.partial(_mqa_flash_kernel, mode=mode, heads=heads, dk=dk, dv=dv, k_off=k_off, v_off=v_off,
                             tq=tq, scale=scale, slopes=slopes, window=WINDOW, sel_block=SEL_BLOCK, nks=nks)
    return pl.pallas_call(
        kern,
        grid=(b, nq, nks),
        in_specs=in_specs,
        out_specs=pl.BlockSpec((tq, heads * dv), lambda bi, qi, kk: (bi * nq + qi, 0)),
        out_shape=jax.ShapeDtypeStruct((b * t, heads * dv), F32),
        scratch_shapes=[
            pltpu.VMEM((heads, tq, dk), BF16),
            pltpu.VMEM((heads, tq, LANE), F32),
            pltpu.VMEM((heads, tq, LANE), F32),
            pltpu.VMEM((heads, tq, dv), F32),
        ],
        compiler_params=pltpu.CompilerParams(
            dimension_semantics=("parallel", "parallel", "arbitrary"), vmem_limit_bytes=VMEM_LIMIT),
        name="mqa_flash_" + mode,
    )(*args)


def _nsa_cmp_kernel(q_ref, kc_ref, vc_ref, oc_ref, sel_ref, *, tq, heads, hd, scale, slopes, pos0, n_sel, k_top):
    qi = pl.program_id(1)
    nc = kc_ref.shape[1]
    lsel = sel_ref.shape[1]
    pos = pos0 + qi * tq + lax.broadcasted_iota(jnp.int32, (tq, nc), 0)
    cmp_end = CMP_STRIDE * lax.broadcasted_iota(jnp.int32, (tq, nc), 1) + (CMP_LEN - 1)
    valid = cmp_end <= pos
    dist = (pos - cmp_end).astype(F32)
    kc = kc_ref[0].astype(BF16)
    vc = vc_ref[0].astype(BF16)
    psum = jnp.zeros((tq, nc), F32)
    for h in range(heads):
        qh = (q_ref[:, h * hd:(h + 1) * hd] * scale).astype(BF16)
        s = lax.dot_general(qh, kc, (((1,), (1,)), ((), ())), preferred_element_type=F32) - slopes[h] * dist
        s = jnp.where(valid, s, NEG)
        e = jnp.exp(s - jnp.max(s, axis=-1, keepdims=True))
        p = jnp.where(valid, e / jnp.sum(e, axis=-1, keepdims=True), 0.0)
        oc_ref[:, h * hd:(h + 1) * hd] = jnp.dot(p.astype(BF16), vc, preferred_element_type=F32)
        psum = psum + p
    sel_ref[...] = _select_blocks(psum, pos0, qi * tq, tq, nc, lsel, n_sel, k_top)


def nsa_cmp_select(q, q_col, k_cmp, v_cmp, b, t, pos0, n_sel):
    tq = min(ATT_TILE, t)
    nq = t // tq
    qw = NSA_HEADS * NSA_HD
    nc = k_cmp.shape[1]
    lsel = -(-n_sel // LANE) * LANE
    assert t % tq == 0 and q_col % qw == 0
    kern = functools.partial(_nsa_cmp_kernel, tq=tq, heads=NSA_HEADS, hd=NSA_HD, scale=NSA_HD ** -0.5,
                             slopes=_alibi_slopes_np(NSA_HEADS), pos0=pos0, n_sel=n_sel, k_top=min(N_SEL, n_sel))
    return pl.pallas_call(
        kern,
        grid=(b, nq),
        in_specs=[
            pl.BlockSpec((tq, qw), lambda bi, qi: (bi * nq + qi, q_col // qw)),
            pl.BlockSpec((1, nc, NSA_HD), lambda bi, qi: (bi, 0, 0)),
            pl.BlockSpec((1, nc, NSA_HD), lambda bi, qi: (bi, 0, 0)),
        ],
        out_specs=[
            pl.BlockSpec((tq, qw), lambda bi, qi: (bi * nq + qi, 0)),
            pl.BlockSpec((tq, lsel), lambda bi, qi: (bi * nq + qi, 0)),
        ],
        out_shape=[jax.ShapeDtypeStruct((b * t, qw), F32), jax.ShapeDtypeStruct((b * t, lsel), F32)],
        compiler_params=pltpu.CompilerParams(
            dimension_semantics=("parallel", "parallel"), vmem_limit_bytes=VMEM_LIMIT),
        name="nsa_cmp_select",
    )(q, k_cmp, v_cmp)


def nsa_compress(ck, pe, wc1, wc2):
    b, n_chunk = ck.shape[:2]
    feat = CMP_STRIDE * NSA_HD
    x = jnp.moveaxis(ck, 3, 0).reshape(2, b * n_chunk, feat)
    w_halves = _cmp_weight_halves(wc1)
    pr = [rms_matmul(x[kind], w_halves[kind]).reshape(b, n_chunk, 2 * CMP_HID) for kind in range(2)]
    return nsa_compress_tail(pr, pe, wc1, wc2)


def _cmp_weight_halves(wc1):
    feat = CMP_STRIDE * NSA_HD
    return jnp.concatenate([wc1[:, :feat], wc1[:, feat:]], axis=2)


def nsa_compress_tail(pr, pe, wc1, wc2):
    b, n_chunk = pr[0].shape[:2]
    feat = CMP_STRIDE * NSA_HD
    pe_flat = jnp.pad(pe.reshape(2, 1, 2 * feat), ((0, 0), (0, 7), (0, 0)))
    out = []
    for kind in range(2):
        pe_term = rms_matmul(pe_flat[kind], wc1[kind])[0]
        nxt = jnp.concatenate([pr[kind][:, 1:, CMP_HID:], jnp.zeros((b, 1, CMP_HID), F32)], axis=1)
        pre = (pr[kind][:, :, :CMP_HID] + nxt + pe_term).reshape(b * n_chunk, CMP_HID)
        out.append(rms_matmul(pre, wc2[kind], pre_act="gelu").reshape(b, n_chunk, NSA_HD))
    return out[0], out[1]


PAGES_PER_STEP = 8


def _paged_attn_kernel(pt_ref, vis_ref, cnt_ref, *refs, mode, pg, t_new, kv_of, scale, kbase, q_base, page,
                       use_slopes, n_vis):
    refs = list(refs)
    q_ref = refs.pop(0)
    slope_ref = refs.pop(0)
    sel_ref = refs.pop(0) if mode == "select" else None
    page_refs = [refs.pop(0) for _ in range(pg)]
    newk_ref, newv_ref, o_ref, m_ref, l_ref, acc_ref = refs
    bi = pl.program_id(0)
    j = pl.program_id(1)
    n_steps = pl.num_programs(1)
    rows = q_ref.shape[1]
    cnt = cnt_ref[bi]

    @pl.when(j == 0)
    def _():
        m_ref[...] = jnp.full_like(m_ref, NEG)
        l_ref[...] = jnp.zeros_like(l_ref)
        acc_ref[...] = jnp.zeros_like(acc_ref)

    qb = q_ref[0].astype(BF16)
    q_pos = q_base + lax.broadcasted_iota(jnp.int32, (rows, page), 0) % t_new
    lane = lax.broadcasted_iota(jnp.int32, (rows, page), 1)

    def scores(s, kpos0, mask):
        dist_i = q_pos - (kpos0 + lane)
        if use_slopes:
            s = s - slope_ref[:, :1] * dist_i.astype(F32)
        if mode == "window":
            wmask = dist_i < WINDOW
            mask = wmask if mask is None else mask & wmask
        if mode == "select":
            selr = sel_ref[0]
            blk = lax.broadcasted_iota(jnp.int32, selr.shape, 1)
            blk0 = kpos0 // SEL_BLOCK
            c0 = jnp.sum(jnp.where(blk == blk0, selr, 0.0), axis=1, keepdims=True)
            c1 = jnp.sum(jnp.where(blk == blk0 + 1, selr, 0.0), axis=1, keepdims=True)
            cmask = jnp.where(lane < SEL_BLOCK, c0, c1) > 0.5
            mask = cmask if mask is None else mask & cmask
        if mask is not None:
            s = jnp.where(mask, s, NEG)
        return s

    def update(s_list, v, v_transposed):
        m_old = m_ref[:, :1]
        mx = s_list[0]
        for s in s_list[1:]:
            mx = jnp.maximum(mx, s)
        m_new = jnp.maximum(m_old, jnp.max(mx, axis=-1, keepdims=True))
        alpha = jnp.exp(m_old - m_new)
        p_list = [jnp.exp(s - m_new) for s in s_list]
        ps = p_list[0]
        for p in p_list[1:]:
            ps = ps + p
        p_all = (p_list[0] if len(p_list) == 1 else jnp.concatenate(p_list, axis=1)).astype(BF16)
        if v_transposed:
            pv = lax.dot_general(p_all, v, (((1,), (1,)), ((), ())), preferred_element_type=F32)
        else:
            pv = jnp.dot(p_all, v, preferred_element_type=F32)
        l_ref[...] = jnp.broadcast_to(alpha * l_ref[:, :1] + jnp.sum(ps, axis=-1, keepdims=True), l_ref.shape)
        m_ref[...] = jnp.broadcast_to(m_new, m_ref.shape)
        acc_ref[...] = alpha * acc_ref[...] + pv

    @pl.when(j * pg < cnt)
    def _():
        kvs = [kv_of(pr) for pr in page_refs]
        k_all = jnp.concatenate([k.astype(BF16) for k, _ in kvs], axis=1)
        v_all = jnp.concatenate([v.astype(BF16) for _, v in kvs], axis=1)
        s_all = jnp.dot(qb, k_all, preferred_element_type=F32) * scale
        s_list = []
        for i in range(pg):
            slot = j * pg + i
            s = scores(s_all[:, i * page:(i + 1) * page], kbase + vis_ref[bi * n_vis + slot] * page, None)
            if mode == "select":
                s = jnp.where(slot < cnt, s, NEG)
            s_list.append(s)
        update(s_list, v_all, True)

    @pl.when(j == n_steps - 1)
    def _():
        dist_new = q_pos - (q_base + lane)
        s = lax.dot_general(qb, newk_ref[0].astype(BF16), (((1,), (1,)), ((), ())),
                            preferred_element_type=F32) * scale
        s = scores(s, q_base, (lane < t_new) & (dist_new >= 0))
        update([s], newv_ref[0].astype(BF16), False)
        o_ref[0] = acc_ref[...] / l_ref[:, :1]


def paged_attn(q, slopes_rows, pool, kv_of, dv, page_table, new_k, new_v, *, mode, t_new, scale, kbase, q_base,
               sel_rows=None, visit=None, use_slopes=True, pg=PAGES_PER_STEP, pool_block=None):
    b, rows, dk = q.shape
    n_pages = page_table.shape[1]
    page = pool.shape[-1]
    pg = min(pg, n_pages)
    assert n_pages % pg == 0
    n_steps = n_pages // pg
    if visit is None:
        visit = (jnp.tile(jnp.arange(n_pages, dtype=jnp.int32)[None], (b, 1)), jnp.full((b,), n_pages, jnp.int32))
    vis, cnt = visit
    if pool_block is None:
        pool_block = (pool.shape[1:], (0,) * (pool.ndim - 1))
    blk = (1,) + tuple(pool_block[0])
    zeros = tuple(pool_block[1])
    in_specs = [
        pl.BlockSpec((1, rows, dk), lambda bi, j, pt, vs, ct: (bi, 0, 0)),
        pl.BlockSpec((rows, 1), lambda bi, j, pt, vs, ct: (0, 0)),
    ]
    args = [q, slopes_rows]
    if mode == "select":
        in_specs.append(pl.BlockSpec((1, rows, sel_rows.shape[2]), lambda bi, j, pt, vs, ct: (bi, 0, 0)))
        args.append(sel_rows)
    for i in range(pg):
        in_specs.append(pl.BlockSpec(
            blk, lambda bi, j, pt, vs, ct, i=i: (pt[bi * n_pages + vs[bi * n_pages + j * pg + i]],) + zeros))
        args.append(pool)
    in_specs.append(pl.BlockSpec((1, page, dk), lambda bi, j, pt, vs, ct: (bi, 0, 0)))
    in_specs.append(pl.BlockSpec((1, page, dv), lambda bi, j, pt, vs, ct: (bi, 0, 0)))
    args += [new_k, new_v]
    kern = functools.partial(_paged_attn_kernel, mode=mode, pg=pg, t_new=t_new, kv_of=kv_of, scale=scale, kbase=kbase,
                             q_base=q_base, page=page, use_slopes=use_slopes, n_vis=n_pages)
    return pl.pallas_call(
        kern,
        grid_spec=pltpu.PrefetchScalarGridSpec(
            num_scalar_prefetch=3,
            grid=(b, n_steps),
            in_specs=in_specs,
            out_specs=pl.BlockSpec((1, rows, dv), lambda bi, j, pt, vs, ct: (bi, 0, 0)),
            scratch_shapes=[
                pltpu.VMEM((rows, LANE), F32),
                pltpu.VMEM((rows, LANE), F32),
                pltpu.VMEM((rows, dv), F32),
            ],
        ),
        out_shape=jax.ShapeDtypeStruct((b, rows, dv), F32),
        compiler_params=pltpu.CompilerParams(
            dimension_semantics=("parallel", "arbitrary"), vmem_limit_bytes=VMEM_LIMIT),
        name="paged_attn_" + mode,
    )(page_table.reshape(-1), vis.reshape(-1), cnt, *args)


def _sb_paged_kernel(pt_ref, *refs, pg, t_new, kw, scale, page):
    refs = list(refs)
    q_ref = refs.pop(0)
    page_refs = [refs.pop(0) for _ in range(pg)]
    newk_ref, newv_ref, o_ref, run_ref, acc_ref = refs
    j = pl.program_id(1)
    n_steps = pl.num_programs(1)
    rows = q_ref.shape[1]
    qb = (q_ref[0] * scale).astype(BF16)
    later = (lax.broadcasted_iota(jnp.int32, (page, page), 0) >
             lax.broadcasted_iota(jnp.int32, (page, page), 1)).astype(BF16)

    def log_terms(z, mask):
        ls = jnp.minimum(z, 0.0) - jnp.log(1.0 + jnp.exp(-jnp.abs(z)))
        l1mb = ls - z
        if mask is not None:
            l1mb = jnp.where(mask, l1mb, 0.0)
        hi = l1mb.astype(BF16)
        lo = (l1mb - hi.astype(F32)).astype(BF16)
        return ls, l1mb, hi, lo

    def suffix_sums(hi, lo):
        return jnp.dot(hi, later, preferred_element_type=F32) + jnp.dot(lo, later, preferred_element_type=F32)

    @pl.when(j == 0)
    def _():
        mask = (lax.broadcasted_iota(jnp.int32, (rows, page), 1) <
                lax.broadcasted_iota(jnp.int32, (rows, page), 0) % t_new)
        z = lax.dot_general(qb, newk_ref[0].astype(BF16), (((1,), (1,)), ((), ())), preferred_element_type=F32)
        ls, l1mb, hi, lo = log_terms(z, mask)
        aft = suffix_sums(hi, lo)
        att = jnp.where(mask, jnp.exp(ls + aft), 0.0)
        acc_ref[...] = jnp.dot(att.astype(BF16), newv_ref[0].astype(BF16), preferred_element_type=F32)
        run_ref[...] = jnp.broadcast_to(aft[:, :1] + l1mb[:, :1], run_ref.shape)

    k_all = jnp.concatenate([pr[0, 0].reshape(kw, page).astype(BF16) for pr in page_refs], axis=1)
    v_all = jnp.concatenate([pr[0, 1].reshape(kw, page).astype(BF16) for pr in page_refs], axis=1)
    ls, l1mb, hi, lo = log_terms(jnp.dot(qb, k_all, preferred_element_type=F32), None)
    hi_rows = jnp.concatenate([hi[:, i * page:(i + 1) * page] for i in range(pg)], axis=0)
    lo_rows = jnp.concatenate([lo[:, i * page:(i + 1) * page] for i in range(pg)], axis=0)
    aft_rows = suffix_sums(hi_rows, lo_rows)
    run = run_ref[:, :1]
    att = []
    for i in range(pg):
        aft = aft_rows[i * rows:(i + 1) * rows]
        att.append(jnp.exp(ls[:, i * page:(i + 1) * page] + aft + run).astype(BF16))
        run = run + aft[:, :1] + l1mb[:, i * page:i * page + 1]
    run_ref[...] = jnp.broadcast_to(run, run_ref.shape)
    acc_ref[...] += lax.dot_general(jnp.concatenate(att, axis=1), v_all, (((1,), (1,)), ((), ())),
                                    preferred_element_type=F32)

    @pl.when(j == n_steps - 1)
    def _():
        o_ref[0] = acc_ref[...]


def sb_paged(q_rows, pool, page_table, new_k, new_v, t_new, pg=PAGES_PER_STEP):
    b, rows, kw = q_rows.shape
    n_pages = page_table.shape[1]
    page = pool.shape[-1]
    pg = min(pg, n_pages)
    assert n_pages % pg == 0
    n_steps = n_pages // pg
    in_specs = [pl.BlockSpec((1, rows, kw), lambda bi, j, pt: (bi, 0, 0))]
    args = [q_rows]
    blk = (1,) + pool.shape[1:]
    for i in range(pg):
        in_specs.append(pl.BlockSpec(
            blk, lambda bi, j, pt, i=i: (pt[bi * n_pages + n_pages - 1 - (j * pg + i)], 0, 0, 0, 0)))
        args.append(pool)
    in_specs.append(pl.BlockSpec((1, page, kw), lambda bi, j, pt: (bi, 0, 0)))
    in_specs.append(pl.BlockSpec((1, page, kw), lambda bi, j, pt: (bi, 0, 0)))
    args += [new_k, new_v]
    kern = functools.partial(_sb_paged_kernel, pg=pg, t_new=t_new, kw=kw, scale=SB_HD ** -0.5, page=page)
    return pl.pallas_call(
        kern,
        grid_spec=pltpu.PrefetchScalarGridSpec(
            num_scalar_prefetch=1,
            grid=(b, n_steps),
            in_specs=in_specs,
            out_specs=pl.BlockSpec((1, rows, kw), lambda bi, j, pt: (bi, 0, 0)),
            scratch_shapes=[pltpu.VMEM((rows, LANE), F32), pltpu.VMEM((rows, kw), F32)],
        ),
        out_shape=jax.ShapeDtypeStruct((b, rows, kw), F32),
        compiler_params=pltpu.CompilerParams(
            dimension_semantics=("parallel", "arbitrary"), vmem_limit_bytes=VMEM_LIMIT),
        name="sb_paged",
    )(page_table.reshape(-1), *args)


def _chunk_proj_kernel(pt_ref, *refs, pg, rows_per_chunk, kinds):
    page_refs = refs[:pg]
    w_ref, o_ref, rows_ref = refs[pg:]
    page = page_refs[0].shape[3]
    n_c = page // rows_per_chunk
    for kind in range(kinds):
        for i, pr in enumerate(page_refs):
            rows_ref[kind, i * page:(i + 1) * page, :] = pr[0, kind].T
        acc = jnp.zeros((pg * n_c, w_ref.shape[3]), F32)
        for r in range(rows_per_chunk):
            a = rows_ref[kind, pl.ds(r, pg * n_c, stride=rows_per_chunk), :]
            acc = acc + jnp.dot(a.astype(BF16), w_ref[kind, r], preferred_element_type=F32)
        o_ref[0, kind] = acc


def nsa_chunk_proj(pool, page_table, w_rows, pg=PAGES_PER_STEP):
    b, n_pages = page_table.shape
    hd, page = pool.shape[2:]
    n_c = page // CMP_STRIDE
    m = w_rows.shape[3]
    pg = min(pg, n_pages)
    assert n_pages % pg == 0
    blk = (1, 2, hd, page)
    in_specs = [pl.BlockSpec(blk, lambda bi, j, pt, i=i: (pt[bi * n_pages + j * pg + i], 0, 0, 0)) for i in range(pg)]
    in_specs.append(pl.BlockSpec(w_rows.shape, lambda bi, j, pt: (0, 0, 0, 0)))
    return pl.pallas_call(
        functools.partial(_chunk_proj_kernel, pg=pg, rows_per_chunk=CMP_STRIDE, kinds=2),
        grid_spec=pltpu.PrefetchScalarGridSpec(
            num_scalar_prefetch=1,
            grid=(b, n_pages // pg),
            in_specs=in_specs,
            out_specs=pl.BlockSpec((1, 2, pg * n_c, m), lambda bi, j, pt: (bi, 0, j, 0)),
            scratch_shapes=[pltpu.VMEM((2, pg * page, hd), F32)],
        ),
        out_shape=jax.ShapeDtypeStruct((b, 2, n_pages * n_c, m), F32),
        compiler_params=pltpu.CompilerParams(
            dimension_semantics=("parallel", "arbitrary"), vmem_limit_bytes=VMEM_LIMIT),
        name="nsa_chunk_proj",
    )(page_table.reshape(-1), *([pool] * pg), w_rows)


def _old_paged_mqa_kernel(pt_ref, *refs, mode, pg, t_new, dk, dv, k_off, v_off, scale, kbase, q_base, page, use_slopes):
    refs = list(refs)
    q_ref = refs.pop(0)
    slope_ref = refs.pop(0)
    sel_ref = refs.pop(0) if mode == "select" else None
    page_refs = [refs.pop(0) for _ in range(pg)]
    new_ref, o_ref, m_ref, l_ref, acc_ref = refs
    j = pl.program_id(1)
    n_steps = pl.num_programs(1)
    rows = q_ref.shape[1]

    @pl.when(j == 0)
    def _():
        m_ref[...] = jnp.full_like(m_ref, NEG)
        l_ref[...] = jnp.zeros_like(l_ref)
        acc_ref[...] = jnp.zeros_like(acc_ref)

    qb = q_ref[0].astype(BF16)
    q_pos = q_base + lax.broadcasted_iota(jnp.int32, (rows, page), 0) % t_new
    lane = lax.broadcasted_iota(jnp.int32, (rows, page), 1)

    def tile(kv, kpos0, is_new, carry):
        m_old, l_old, acc = carry
        kt = kv[:, k_off:k_off + dk].astype(BF16)
        vt = kv[:, v_off:v_off + dv].astype(BF16)
        s = lax.dot_general(qb, kt, (((1,), (1,)), ((), ())), preferred_element_type=F32) * scale
        dist_i = q_pos - (kpos0 + lane)
        if use_slopes:
            s = s - slope_ref[:, :1] * dist_i.astype(F32)
        mask = None
        if is_new:
            mask = (lane < t_new) & (dist_i >= 0)
        if mode == "window":
            wmask = dist_i < WINDOW
            mask = wmask if mask is None else mask & wmask
        if mode == "select":
            selr = sel_ref[0]
            blk = lax.broadcasted_iota(jnp.int32, selr.shape, 1)
            blk0 = kpos0 // SEL_BLOCK
            c0 = jnp.sum(jnp.where(blk == blk0, selr, 0.0), axis=1, keepdims=True)
            c1 = jnp.sum(jnp.where(blk == blk0 + 1, selr, 0.0), axis=1, keepdims=True)
            cmask = jnp.where(lane < SEL_BLOCK, c0, c1) > 0.5
            mask = cmask if mask is None else mask & cmask
        if mask is not None:
            s = jnp.where(mask, s, NEG)
        m_new = jnp.maximum(m_old, jnp.max(s, axis=-1, keepdims=True))
        alpha = jnp.exp(m_old - m_new)
        p = jnp.exp(s - m_new)
        l_new = alpha * l_old + jnp.sum(p, axis=-1, keepdims=True)
        acc = alpha * acc + jnp.dot(p.astype(BF16), vt, preferred_element_type=F32)
        return m_new, l_new, acc

    carry = (m_ref[:, :1], l_ref[:, :1], acc_ref[...])
    for i in range(pg):
        carry = tile(page_refs[i][0], kbase + (j * pg + i) * page, False, carry)
    m_ref[...] = jnp.broadcast_to(carry[0], m_ref.shape)
    l_ref[...] = jnp.broadcast_to(carry[1], l_ref.shape)
    acc_ref[...] = carry[2]

    @pl.when(j == n_steps - 1)
    def _():
        m_f, l_f, acc_f = tile(new_ref[0], q_base, True, (m_ref[:, :1], l_ref[:, :1], acc_ref[...]))
        o_ref[0] = acc_f / l_f


def paged_mqa(q, slopes_rows, pool, lane_blk, kv_w, page_table, new_tile, *, mode, t_new, dk, dv, k_off, v_off,
              scale, kbase, q_base, sel_rows=None, use_slopes=True):
    b, rows, _ = q.shape
    n_pages = page_table.shape[1]
    page = pool.shape[1]
    pg = min(PAGES_PER_STEP, n_pages)
    assert n_pages % pg == 0
    n_steps = n_pages // pg
    in_specs = [
        pl.BlockSpec((1, rows, dk), lambda bi, j, pt: (bi, 0, 0)),
        pl.BlockSpec((rows, 1), lambda bi, j, pt: (0, 0)),
    ]
    args = [q, slopes_rows]
    if mode == "select":
        in_specs.append(pl.BlockSpec((1, rows, sel_rows.shape[2]), lambda bi, j, pt: (bi, 0, 0)))
        args.append(sel_rows)
    for i in range(pg):
        in_specs.append(pl.BlockSpec(
            (1, page, kv_w), lambda bi, j, pt, i=i: (pt[bi * n_pages + j * pg + i], 0, lane_blk)))
        args.append(pool)
    in_specs.append(pl.BlockSpec((1, page, kv_w), lambda bi, j, pt: (bi, 0, 0)))
    args.append(new_tile)
    kern = functools.partial(_paged_mqa_kernel, mode=mode, pg=pg, t_new=t_new, dk=dk, dv=dv, k_off=k_off, v_off=v_off,
                             scale=scale, kbase=kbase, q_base=q_base, page=page, use_slopes=use_slopes)
    return pl.pallas_call(
        kern,
        grid_spec=pltpu.PrefetchScalarGridSpec(
            num_scalar_prefetch=1,
            grid=(b, n_steps),
            in_specs=in_specs,
            out_specs=pl.BlockSpec((1, rows, dv), lambda bi, j, pt: (bi, 0, 0)),
            scratch_shapes=[
                pltpu.VMEM((rows, LANE), F32),
                pltpu.VMEM((rows, LANE), F32),
                pltpu.VMEM((rows, dv), F32),
            ],
        ),
        out_shape=jax.ShapeDtypeStruct((b, rows, dv), F32),
        compiler_params=pltpu.CompilerParams(
            dimension_semantics=("parallel", "arbitrary"), vmem_limit_bytes=VMEM_LIMIT),
        name="paged_mqa_" + mode,
    )(page_table.reshape(-1), *args)


def _sb_decode_kernel(pt_ref, *refs, pg, t_new, kw, scale, page):
    refs = list(refs)
    q_ref = refs.pop(0)
    page_refs = [refs.pop(0) for _ in range(pg)]
    new_ref, o_ref, run_ref, acc_ref = refs
    j = pl.program_id(1)
    n_steps = pl.num_programs(1)
    rows = q_ref.shape[1]
    qb = (q_ref[0] * scale).astype(BF16)
    later = (lax.broadcasted_iota(jnp.int32, (page, page), 0) >
             lax.broadcasted_iota(jnp.int32, (page, page), 1)).astype(BF16)

    def tile(kv, is_new, carry):
        run, acc = carry
        kt = kv[:, :kw].astype(BF16)
        vt = kv[:, kw:2 * kw].astype(BF16)
        z = lax.dot_general(qb, kt, (((1,), (1,)), ((), ())), preferred_element_type=F32)
        ls = jnp.minimum(z, 0.0) - jnp.log(1.0 + jnp.exp(-jnp.abs(z)))
        l1mb = ls - z
        if is_new:
            mask = (lax.broadcasted_iota(jnp.int32, (rows, page), 1) <
                    lax.broadcasted_iota(jnp.int32, (rows, page), 0) % t_new)
            l1mb = jnp.where(mask, l1mb, 0.0)
        hi = l1mb.astype(BF16)
        lo = (l1mb - hi.astype(F32)).astype(BF16)
        aft = jnp.dot(hi, later, preferred_element_type=F32) + jnp.dot(lo, later, preferred_element_type=F32)
        att = jnp.exp(ls + aft + run)
        if is_new:
            att = jnp.where(mask, att, 0.0)
        acc = acc + jnp.dot(att.astype(BF16), vt, preferred_element_type=F32)
        return run + aft[:, :1] + l1mb[:, :1], acc

    @pl.when(j == 0)
    def _():
        run0, acc0 = tile(new_ref[0], True, (jnp.zeros((rows, 1), F32), jnp.zeros((rows, kw), F32)))
        run_ref[...] = jnp.broadcast_to(run0, run_ref.shape)
        acc_ref[...] = acc0

    carry = (run_ref[:, :1], acc_ref[...])
    for i in range(pg):
        carry = tile(page_refs[i][0], False, carry)
    run_ref[...] = jnp.broadcast_to(carry[0], run_ref.shape)
    acc_ref[...] = carry[1]

    @pl.when(j == n_steps - 1)
    def _():
        o_ref[0] = acc_ref[...]


def sb_decode(q_rows, pool, page_table, new_tile, t_new):
    b, rows, kw = q_rows.shape
    n_pages = page_table.shape[1]
    page = pool.shape[1]
    pg = min(PAGES_PER_STEP, n_pages)
    assert n_pages % pg == 0
    n_steps = n_pages // pg
    in_specs = [pl.BlockSpec((1, rows, kw), lambda bi, j, pt: (bi, 0, 0))]
    args = [q_rows]
    for i in range(pg):
        in_specs.append(pl.BlockSpec(
            (1, page, 2 * kw), lambda bi, j, pt, i=i: (pt[bi * n_pages + n_pages - 1 - (j * pg + i)], 0, 0)))
        args.append(pool)
    in_specs.append(pl.BlockSpec((1, page, 2 * kw), lambda bi, j, pt: (bi, 0, 0)))
    args.append(new_tile)
    kern = functools.partial(_sb_decode_kernel, pg=pg, t_new=t_new, kw=kw, scale=SB_HD ** -0.5, page=page)
    return pl.pallas_call(
        kern,
        grid_spec=pltpu.PrefetchScalarGridSpec(
            num_scalar_prefetch=1,
            grid=(b, n_steps),
            in_specs=in_specs,
            out_specs=pl.BlockSpec((1, rows, kw), lambda bi, j, pt: (bi, 0, 0)),
            scratch_shapes=[pltpu.VMEM((rows, LANE), F32), pltpu.VMEM((rows, kw), F32)],
        ),
        out_shape=jax.ShapeDtypeStruct((b, rows, kw), F32),
        compiler_params=pltpu.CompilerParams(
            dimension_semantics=("parallel", "arbitrary"), vmem_limit_bytes=VMEM_LIMIT),
        name="sb_decode",
    )(page_table.reshape(-1), *args)


def _nsa_cmp_decode_kernel(q_ref, slope_ref, kc_ref, vc_ref, oc_ref, sel_ref, *, t_new, heads, scale, q_base, n_sel, k_top):
    rows = q_ref.shape[1]
    nc = kc_ref.shape[1]
    lsel = sel_ref.shape[2]
    pos = q_base + lax.broadcasted_iota(jnp.int32, (rows, nc), 0) % t_new
    cmp_end = CMP_STRIDE * lax.broadcasted_iota(jnp.int32, (rows, nc), 1) + (CMP_LEN - 1)
    valid = cmp_end <= pos
    dist = (pos - cmp_end).astype(F32)
    qb = (q_ref[0] * scale).astype(BF16)
    kc = kc_ref[0].astype(BF16)
    vc = vc_ref[0].astype(BF16)
    s = lax.dot_general(qb, kc, (((1,), (1,)), ((), ())), preferred_element_type=F32) - slope_ref[:, :1] * dist
    s = jnp.where(valid, s, NEG)
    e = jnp.exp(s - jnp.max(s, axis=-1, keepdims=True))
    p = jnp.where(valid, e / jnp.sum(e, axis=-1, keepdims=True), 0.0)
    oc_ref[0] = jnp.dot(p.astype(BF16), vc, preferred_element_type=F32)
    psum = p[0:t_new]
    for h in range(1, heads):
        psum = psum + p[h * t_new:(h + 1) * t_new]
    sel_ref[0] = _select_blocks(psum, q_base, 0, t_new, nc, lsel, n_sel, k_top)


def _select_blocks(psum, pos0, row0, tq, nc, lsel, n_sel, k_top):
    c_start = CMP_STRIDE * lax.broadcasted_iota(jnp.int32, (nc, lsel), 0)
    s_start = SEL_BLOCK * lax.broadcasted_iota(jnp.int32, (nc, lsel), 1)
    overlap = ((c_start < s_start + SEL_BLOCK) & (c_start + CMP_LEN > s_start)).astype(F32)
    imp = jnp.dot(psum, overlap, preferred_element_type=F32, precision=lax.Precision.HIGHEST)
    tpos = pos0 + row0 + lax.broadcasted_iota(jnp.int32, (tq, lsel), 0)
    j = lax.broadcasted_iota(jnp.int32, (tq, lsel), 1)
    cur = tpos // SEL_BLOCK
    forced = (j == 0) | (j == cur) | (j == cur - 1)
    cand = (SEL_BLOCK * j <= tpos) & (j < n_sel)
    score = jnp.where(cand, imp + jnp.where(forced, FORCE_BONUS, 0.0), NEG)
    work = score
    chosen = jnp.zeros((tq, lsel), F32)
    for _ in range(k_top):
        mx = jnp.max(work, axis=-1, keepdims=True)
        first = jnp.min(jnp.where(work == mx, j, lsel), axis=-1, keepdims=True)
        hit = j == first
        chosen = jnp.where(hit, 1.0, chosen)
        work = jnp.where(hit, -3.0e38, work)
    return jnp.where(score > 0.5 * NEG, chosen, 0.0)


def nsa_cmp_decode(q_rows, slopes_rows, k_cmp, v_cmp, t_new, q_base, n_sel):
    b, rows, hd = q_rows.shape
    nc = k_cmp.shape[1]
    lsel = -(-n_sel // LANE) * LANE
    kern = functools.partial(_nsa_cmp_decode_kernel, t_new=t_new, heads=rows // t_new, scale=hd ** -0.5,
                             q_base=q_base, n_sel=n_sel, k_top=min(N_SEL, n_sel))
    return pl.pallas_call(
        kern,
        grid=(b,),
        in_specs=[
            pl.BlockSpec((1, rows, hd), lambda bi: (bi, 0, 0)),
            pl.BlockSpec((rows, 1), lambda bi: (0, 0)),
            pl.BlockSpec((1, nc, hd), lambda bi: (bi, 0, 0)),
            pl.BlockSpec((1, nc, hd), lambda bi: (bi, 0, 0)),
        ],
        out_specs=[
            pl.BlockSpec((1, rows, hd), lambda bi: (bi, 0, 0)),
            pl.BlockSpec((1, t_new, lsel), lambda bi: (bi, 0, 0)),
        ],
        out_shape=[jax.ShapeDtypeStruct((b, rows, hd), F32), jax.ShapeDtypeStruct((b, t_new, lsel), F32)],
        compiler_params=pltpu.CompilerParams(dimension_semantics=("parallel",), vmem_limit_bytes=VMEM_LIMIT),
        name="nsa_cmp_decode",
    )(q_rows, slopes_rows, k_cmp, v_cmp)


def _router_kernel(x_ref, g_ref, w_ref, b_ref, lg_ref, h_ref):
    x = x_ref[...]
    ms = jnp.mean(x * x, axis=-1, keepdims=True)
    h = (x * lax.rsqrt(ms + EPS)) * g_ref[...]
    h_ref[...] = h.astype(BF16)
    lg_ref[...] = jnp.dot(h, w_ref[...], preferred_element_type=F32,
                          precision=lax.Precision.HIGHEST) + b_ref[...]


def _experts_kernel(te_ref, tv_ref, xs_ref, sw_ref, w1_ref, w3_ref, w2_ref, o_ref, b1_ref, b3_ref, b2_ref):
    i = pl.program_id(0)

    @pl.when((i == 0) | (te_ref[i] != te_ref[jnp.maximum(i - 1, 0)]))
    def _():
        b1_ref[...] = w1_ref[0].astype(BF16)
        b3_ref[...] = w3_ref[0].astype(BF16)
        b2_ref[...] = w2_ref[0].astype(BF16)

    @pl.when(tv_ref[i] != 0)
    def _():
        x = xs_ref[...]
        a = jnp.dot(x, b1_ref[...], preferred_element_type=F32)
        c = jnp.dot(x, b3_ref[...], preferred_element_type=F32)
        hid = (jax.nn.silu(a) * c) * sw_ref[...]
        o_ref[...] = jnp.dot(hid.astype(BF16), b2_ref[...], preferred_element_type=F32)

    @pl.when(tv_ref[i] == 0)
    def _():
        o_ref[...] = jnp.zeros_like(o_ref)


def moe(x, g, w_rg, b_rg, w_re, b_re, w_e1, w_e3, w_e2):
    n, d = x.shape
    ne = N_GROUPS * N_EXP
    tm = _pick_tile(n, (ROW_TILE, 256, 128))
    wr = jnp.zeros((d, LANE), F32).at[:, :N_GROUPS].set(w_rg).at[:, N_GROUPS:N_GROUPS + ne].set(w_re)
    br = jnp.zeros((1, LANE), F32).at[0, :N_GROUPS].set(b_rg).at[0, N_GROUPS:N_GROUPS + ne].set(b_re)
    logits, h = pl.pallas_call(
        _router_kernel,
        grid=(n // tm,),
        in_specs=[
            pl.BlockSpec((tm, d), lambda i: (i, 0)),
            pl.BlockSpec((1, d), lambda i: (0, 0)),
            pl.BlockSpec((d, LANE), lambda i: (0, 0)),
            pl.BlockSpec((1, LANE), lambda i: (0, 0)),
        ],
        out_specs=[pl.BlockSpec((tm, LANE), lambda i: (i, 0)), pl.BlockSpec((tm, d), lambda i: (i, 0))],
        out_shape=[jax.ShapeDtypeStruct((n, LANE), F32), jax.ShapeDtypeStruct((n, d), BF16)],
        compiler_params=pltpu.CompilerParams(dimension_semantics=("parallel",), vmem_limit_bytes=VMEM_LIMIT),
        name="moe_router",
    )(x, g.astype(F32).reshape(1, d), wr, br)

    lg = logits[:, :N_GROUPS]
    le = logits[:, N_GROUPS:N_GROUPS + ne].reshape(n, N_GROUPS, N_EXP)
    pg = jax.nn.softmax(lg, axis=-1)
    gi = jnp.argmax(lg, axis=-1)
    gw = jnp.take_along_axis(pg, gi[:, None], axis=1)[:, 0]
    le_g = jnp.take_along_axis(le, gi[:, None, None], axis=1)[:, 0]
    top_v, top_i = lax.top_k(le_g, TOP_K)
    we = jax.nn.softmax(top_v, axis=-1)
    eid = (gi[:, None] * N_EXP + top_i).astype(jnp.int32)
    wt = we * gw[:, None]

    ts = EXPERT_TILE
    n_tiles = -(-(TOP_K * n) // ts) + ne
    member = (eid[:, :, None] == jnp.arange(ne)[None, None, :]).any(axis=1).astype(jnp.int32)
    cnt = member.sum(axis=0)
    rank = jnp.cumsum(member, axis=0) - member
    tiles_e = (cnt + ts - 1) // ts
    tile_end = jnp.cumsum(tiles_e)
    pad_off = (tile_end - tiles_e) * ts
    pos = pad_off[eid] + jnp.take_along_axis(rank, eid, axis=1)
    tok = jnp.broadcast_to(jnp.arange(n, dtype=jnp.int32)[:, None], (n, TOP_K))
    slot_tok = jnp.zeros((n_tiles * ts,), jnp.int32).at[pos.reshape(-1)].set(tok.reshape(-1))
    slot_w = jnp.zeros((n_tiles * ts,), F32).at[pos.reshape(-1)].set(wt.reshape(-1))
    tile_id = jnp.arange(n_tiles, dtype=jnp.int32)
    tile_e = jnp.minimum(jnp.searchsorted(tile_end, tile_id, side="right"), ne - 1).astype(jnp.int32)
    tile_v = (tile_id < tile_end[-1]).astype(jnp.int32)

    xs = jnp.take(h, slot_tok, axis=0)
    w1 = w_e1.reshape(ne, d, D_EXP)
    w3 = w_e3.reshape(ne, d, D_EXP)
    w2 = w_e2.reshape(ne, D_EXP, d)
    ys = pl.pallas_call(
        _experts_kernel,
        grid_spec=pltpu.PrefetchScalarGridSpec(
            num_scalar_prefetch=2,
            grid=(n_tiles,),
            in_specs=[
                pl.BlockSpec((ts, d), lambda i, te, tv: (i, 0)),
                pl.BlockSpec((ts, 1), lambda i, te, tv: (i, 0)),
                pl.BlockSpec((1, d, D_EXP), lambda i, te, tv: (te[i], 0, 0)),
                pl.BlockSpec((1, d, D_EXP), lambda i, te, tv: (te[i], 0, 0)),
                pl.BlockSpec((1, D_EXP, d), lambda i, te, tv: (te[i], 0, 0)),
            ],
            out_specs=pl.BlockSpec((ts, d), lambda i, te, tv: (i, 0)),
            scratch_shapes=[pltpu.VMEM((d, D_EXP), BF16), pltpu.VMEM((d, D_EXP), BF16), pltpu.VMEM((D_EXP, d), BF16)],
        ),
        out_shape=jax.ShapeDtypeStruct((n_tiles * ts, d), F32),
        compiler_params=pltpu.CompilerParams(dimension_semantics=("arbitrary",), vmem_limit_bytes=VMEM_LIMIT),
        name="moe_experts",
    )(tile_e, tile_v, xs, slot_w.reshape(-1, 1), w1, w3, w2)
    return jnp.take(ys, pos[:, 0], axis=0) + jnp.take(ys, pos[:, 1], axis=0)


def rmsnorm(x, g):
    xf = x.astype(F32)
    y = xf * lax.rsqrt(jnp.mean(xf * xf, axis=-1, keepdims=True) + EPS)
    return (y * g.astype(F32)).astype(x.dtype)


def split_cols(a, widths):
    return jnp.split(a, np.cumsum(widths)[:-1].tolist(), axis=-1)


def qblock(t):
    return min(QBLOCK, t)


def map_query_blocks(fn, arrays, block):
    t = arrays[0].shape[1]
    nb = -(-t // block)
    pad = nb * block - t

    def to_blocks(a):
        a = jnp.pad(a, [(0, 0), (0, pad)] + [(0, 0)] * (a.ndim - 2))
        a = a.reshape((a.shape[0], nb, block) + a.shape[2:])
        return jnp.moveaxis(a, 1, 0)

    xs = (jnp.arange(nb),) + tuple(to_blocks(a) for a in arrays)
    out = lax.map(lambda args: fn(*args), xs)
    out = jnp.moveaxis(out, 0, 1)
    out = out.reshape((out.shape[0], nb * block) + out.shape[3:])
    return out[:, :t]


def rope_angles(pos, dim):
    inv = ROPE_THETA ** (-jnp.arange(0, dim, 2, dtype=F32) / dim)
    ang = pos.astype(F32)[:, None] * inv[None, :]
    return jnp.cos(ang), jnp.sin(ang)


def apply_rope(x, cos, sin):
    half = x.shape[-1] // 2
    x1 = x[..., :half].astype(F32)
    x2 = x[..., half:].astype(F32)
    return jnp.concatenate([x1 * cos - x2 * sin, x1 * sin + x2 * cos], axis=-1).astype(x.dtype)


def alibi_slopes(n):
    return 2.0 ** (-8.0 * jnp.arange(1, n + 1, dtype=F32) / n)


def gather_pages(pool, page_table):
    g = pool[page_table]
    return g.reshape((page_table.shape[0], page_table.shape[1] * pool.shape[1]) + pool.shape[2:])


def mla_mixer(c_q, c_kv_raw, k_r_raw, pos, past, g_cq, w_uq, g_ckv, w_uk, w_uv):
    b, t, _ = c_q.shape
    cos, sin = rope_angles(pos, ROPE_DIM)
    q = jnp.einsum('btr,rhd->bthd', rmsnorm(c_q, g_cq), w_uq)
    q_nope = q[..., :NOPE_DIM]
    q_rope = apply_rope(q[..., NOPE_DIM:], cos[None, :, None, :], sin[None, :, None, :])
    q_lat = jnp.einsum('bthd,rhd->bthr', q_nope, w_uk)
    c_kv = rmsnorm(c_kv_raw, g_ckv)
    k_r = apply_rope(k_r_raw, cos[None], sin[None])
    new_rows = jnp.concatenate([c_kv, k_r], axis=-1)
    keys = jnp.concatenate([past, new_rows], axis=1)
    k_pos = jnp.arange(keys.shape[1])
    lat, kr = keys[..., :KV_RANK], keys[..., KV_RANK:]
    scale = (NOPE_DIM + ROPE_DIM) ** -0.5

    def blk(bi, ql, qr, qp):
        s = (jnp.einsum('bqhr,bsr->bhqs', ql, lat) + jnp.einsum('bqhd,bsd->bhqs', qr, kr)).astype(F32) * scale
        mask = k_pos[None, :] <= qp[0][:, None]
        p = jax.nn.softmax(jnp.where(mask, s, NEG), axis=-1).astype(lat.dtype)
        return jnp.einsum('bhqs,bsr->bqhr', p, lat)

    o_lat = map_query_blocks(blk, (q_lat, q_rope, pos[None]), qblock(t))
    out = jnp.einsum('bthr,rhd->bthd', o_lat, w_uv).reshape(b, t, MLA_HEADS * MLA_VDIM)
    return out, new_rows


def mlstm_mixer(q, k, v, ig, fg, og, c0, n0, m0, g_mh):
    b, t = q.shape[:2]
    dt = q.dtype
    q = q.reshape(b, t, ML_HEADS, ML_QK)
    k = k.reshape(b, t, ML_HEADS, ML_QK) * (ML_QK ** -0.5)
    v = v.reshape(b, t, ML_HEADS, ML_V)
    li = ig.astype(F32)
    lf = jax.nn.log_sigmoid(fg.astype(F32))
    L = min(ML_CHUNK, t)
    nc = -(-t // L)
    pad = nc * L - t

    def chunks(a, val=0.0):
        a = jnp.pad(a, [(0, 0), (0, pad)] + [(0, 0)] * (a.ndim - 2), constant_values=val)
        return jnp.moveaxis(a.reshape((b, nc, L) + a.shape[2:]), 1, 0)

    tri = jnp.tril(jnp.ones((L, L), dtype=bool))

    def step(carry, xs):
        c, n, m = carry
        qc, kc, vc, lic, lfc = xs
        cb = jnp.cumsum(lfc, axis=1)
        a = cb + m[:, None, :]
        d = cb[:, :, None, :] - cb[:, None, :, :] + lic[:, None, :, :]
        d = jnp.where(tri[None, :, :, None], d, NEG)
        mt = jnp.maximum(a, d.max(axis=2))
        w = jnp.exp(d - mt[:, :, None, :])
        inter = jnp.exp(a - mt)
        sc = w * jnp.einsum('bthd,bshd->btsh', qc, kc).astype(F32)
        num = jnp.einsum('btsh,bshv->bthv', sc, vc) + inter[..., None] * jnp.einsum('bhvd,bthd->bthv', c, qc)
        den = sc.sum(axis=2) + inter * jnp.einsum('bhd,bthd->bth', n, qc)
        h = num / jnp.maximum(jnp.abs(den), jnp.exp(-mt))[..., None]
        wl, il = w[:, -1], inter[:, -1]
        c = il[..., None, None] * c + jnp.einsum('bsh,bshv,bshd->bhvd', wl, vc, kc)
        n = il[..., None] * n + jnp.einsum('bsh,bshd->bhd', wl, kc)
        return (c, n, mt[:, -1]), h

    carry0 = (c0.astype(F32), n0.astype(F32), m0.astype(F32))
    (c, n, m), h = lax.scan(step, carry0, (chunks(q), chunks(k), chunks(v), chunks(li, NEG), chunks(lf)))
    h = jnp.moveaxis(h, 0, 1).reshape(b, nc * L, ML_HEADS, ML_V)[:, :t]
    h = h - h.mean(axis=-1, keepdims=True)
    h = h * lax.rsqrt(jnp.mean(h * h, axis=-1, keepdims=True) + EPS) * g_mh.astype(F32)
    out = jax.nn.sigmoid(og.astype(F32)).reshape(b, t, ML_HEADS, ML_V) * h
    return out.reshape(b, t, ML_HEADS * ML_V).astype(dt), c, n, m


def sb_mixer(q, k_new, v_new, pos, past_kv):
    b, t = q.shape[:2]
    q = q.reshape(b, t, SB_KV_HEADS, SB_HEADS // SB_KV_HEADS, SB_HD)
    new_rows = jnp.stack([k_new.reshape(b, t, SB_KV_HEADS, SB_HD), v_new.reshape(b, t, SB_KV_HEADS, SB_HD)], axis=2)
    kv = jnp.concatenate([past_kv, new_rows], axis=1)
    k, v = kv[:, :, 0], kv[:, :, 1]
    k_pos = jnp.arange(kv.shape[1])

    def blk(bi, qh, qp):
        z = jnp.einsum('bqgrd,bsgd->bgrqs', qh, k).astype(F32) * (SB_HD ** -0.5)
        mask = k_pos[None, :] < qp[0][:, None]
        l1mb = jnp.where(mask, jax.nn.log_sigmoid(-z), 0.0)
        after = lax.cumsum(l1mb, axis=z.ndim - 1, reverse=True) - l1mb
        att = jnp.where(mask, jnp.exp(jax.nn.log_sigmoid(z) + after), 0.0).astype(v.dtype)
        return jnp.einsum('bgrqs,bsgd->bqgrd', att, v)

    o = map_query_blocks(blk, (q, pos[None]), qblock(t))
    return o.reshape(b, t, SB_HEADS * SB_HD), new_rows


def nsa_mixer(q, kv_new, win_new, gate, pos, past_kv, win_prior, pe, wc1, wc2):
    b, t = q.shape[:2]
    past = past_kv.shape[1]
    s_len = past + t
    scale = NSA_HD ** -0.5
    slopes = alibi_slopes(NSA_HEADS)
    kv_all = jnp.concatenate([past_kv, kv_new], axis=1)
    n_chunk = max(-(-s_len // CMP_STRIDE), 2)
    ck = jnp.pad(kv_all[:, :, :2], ((0, 0), (0, n_chunk * CMP_STRIDE - s_len), (0, 0), (0, 0)))
    ck = ck.reshape(b, n_chunk, CMP_STRIDE, 2, NSA_HD)
    w1 = wc1.reshape(2, 2, CMP_STRIDE, NSA_HD, CMP_HID)
    proj = jnp.einsum('bcrkd,kzrdh->bczkh', ck, w1)
    pe_term = jnp.einsum('kpd,kpdh->kh', pe, wc1.reshape(2, CMP_LEN, NSA_HD, CMP_HID))
    hid = jax.nn.gelu(proj[:, :-1, 0] + proj[:, 1:, 1] + pe_term)
    cmp = jnp.einsum('bnkh,khd->bnkd', hid, wc2)
    k_cmp, v_cmp = cmp[:, :, 0], cmp[:, :, 1]
    cmp_start = CMP_STRIDE * jnp.arange(n_chunk - 1)
    cmp_end = cmp_start + CMP_LEN - 1
    n_sel = -(-s_len // SEL_BLOCK)
    sel = jnp.pad(kv_all[:, :, 2:], ((0, 0), (0, n_sel * SEL_BLOCK - s_len), (0, 0), (0, 0)))
    sel = sel.reshape(b, n_sel, SEL_BLOCK, 2, NSA_HD)
    sel_start = SEL_BLOCK * jnp.arange(n_sel)
    overlap = ((cmp_start[:, None] < sel_start[None, :] + SEL_BLOCK) & (cmp_start[:, None] + CMP_LEN > sel_start[None, :])).astype(F32)
    k_top = min(N_SEL, n_sel)
    bidx = jnp.arange(b)[:, None, None]
    pw = win_prior.shape[1]
    qb = qblock(t)
    nb = -(-t // qb)
    band = jnp.concatenate([jnp.zeros((b, WINDOW - pw, 2, NSA_HD), win_new.dtype), win_prior, win_new, jnp.zeros((b, nb * qb - t, 2, NSA_HD), win_new.dtype)], axis=1)
    band_pos = past - WINDOW + jnp.arange(WINDOW + nb * qb)

    def blk(bi, qh, gh, qp):
        tq = qp[0]
        valid_c = cmp_end[None, :] <= tq[:, None]
        dist_c = (tq[:, None] - cmp_end[None, :]).astype(F32)
        s_c = jnp.einsum('bqhd,bnd->bhqn', qh, k_cmp).astype(F32) * scale - slopes[:, None, None] * dist_c
        p_c = jnp.where(valid_c, jax.nn.softmax(jnp.where(valid_c, s_c, NEG), axis=-1), 0.0)
        o_c = jnp.einsum('bhqn,bnd->bqhd', p_c.astype(v_cmp.dtype), v_cmp)
        imp = jnp.einsum('bhqn,nj->bqj', p_c, overlap)
        cur = tq // SEL_BLOCK
        j = jnp.arange(n_sel)
        forced = (j[None, :] == 0) | (j[None, :] == cur[:, None]) | (j[None, :] == cur[:, None] - 1)
        cand = sel_start[None, :] <= tq[:, None]
        score = jnp.where(cand[None], imp + jnp.where(forced, FORCE_BONUS, 0.0)[None], NEG)
        top_s, idx = lax.top_k(score, k_top)
        kv_sel = sel[bidx, idx]
        kpos = idx[..., None] * SEL_BLOCK + jnp.arange(SEL_BLOCK)
        ok = (top_s > 0.5 * NEG)[..., None] & (kpos <= tq[None, :, None, None])
        dist_s = (tq[None, :, None, None] - kpos).astype(F32)
        s_s = jnp.einsum('bqhd,bqkld->bqhkl', qh, kv_sel[..., 0, :]).astype(F32) * scale - slopes[None, None, :, None, None] * dist_s[:, :, None]
        s_s = jnp.where(ok[:, :, None], s_s, NEG)
        p_s = jax.nn.softmax(s_s.reshape(s_s.shape[:3] + (-1,)), axis=-1).reshape(s_s.shape)
        o_s = jnp.einsum('bqhkl,bqkld->bqhd', p_s.astype(kv_sel.dtype), kv_sel[..., 1, :])
        bw = lax.dynamic_slice_in_dim(band, bi * qb, qb + WINDOW, axis=1)
        bp = lax.dynamic_slice_in_dim(band_pos, bi * qb, qb + WINDOW, axis=0)
        dist_w = tq[:, None] - bp[None, :]
        ok_w = (bp[None, :] >= 0) & (dist_w >= 0) & (dist_w < WINDOW)
        s_w = jnp.einsum('bqhd,bsd->bhqs', qh, bw[:, :, 0]).astype(F32) * scale - slopes[:, None, None] * dist_w.astype(F32)
        p_w = jax.nn.softmax(jnp.where(ok_w, s_w, NEG), axis=-1)
        o_w = jnp.einsum('bhqs,bsd->bqhd', p_w.astype(bw.dtype), bw[:, :, 1])
        g = jax.nn.sigmoid(gh.astype(F32))
        return (g[..., 0:1] * o_c + g[..., 1:2] * o_s + g[..., 2:3] * o_w).astype(qh.dtype)

    o = map_query_blocks(blk, (q, gate, pos[None]), qb)
    new_win = jnp.concatenate([win_prior, win_new], axis=1)[:, -min(WINDOW, pw + t):]
    return o.reshape(b, t, NSA_HEADS * NSA_HD), kv_new, new_win


def mem_kv_rows(mem, g, wk, wv):
    b, nm, d = mem.shape
    kv = rms_matmul(mem.reshape(b * nm, d), jnp.concatenate([wk, wv], axis=1), g=g)
    hd = X_HEADS * X_HD
    k = kv[:, :hd].reshape(b, nm, X_HEADS, X_HD)
    v = kv[:, hd:].reshape(b, nm, X_HEADS, X_HD)
    return jnp.stack([k, v], axis=2)


def mem_attend(x, g, mkv, wq, wo):
    b, t, d = x.shape
    q = rms_matmul(x.reshape(b * t, d), wq, g=g).reshape(b, t, X_HEADS, X_HD)
    s = jnp.einsum('bthd,bmhd->bhtm', q, mkv[:, :, 0]).astype(F32) * (X_HD ** -0.5)
    p = jax.nn.softmax(s, axis=-1).astype(mkv.dtype)
    o = jnp.einsum('bhtm,bmhd->bthd', p, mkv[:, :, 1]).reshape(b * t, X_HEADS * X_HD)
    return rms_matmul(o, wo, res=x.reshape(b * t, d)).reshape(b, t, d)


def odd_mixers_prompt(proj, b, t, pe, wc1, wc2):
    n = b * t
    sb_w = SB_HEADS * SB_HD
    kv_w = SB_KV_HEADS * SB_HD
    q_col = sb_w + 2 * kv_w
    kv_col = q_col + NSA_HEADS * NSA_HD
    win_col = kv_col + 4 * NSA_HD
    gate_col = win_col + 2 * NSA_HD
    o_sb = sb_attn_prompt(proj, b, t, 0, sb_w, sb_w + kv_w)
    sb_rows = proj[:, sb_w:sb_w + 2 * kv_w].reshape(b, t, 2, SB_KV_HEADS, SB_HD)
    nsa_rows = proj[:, kv_col:win_col].reshape(b, t, 4, NSA_HD)
    win_rows = proj[:, win_col:gate_col].reshape(b, t, 2, NSA_HD)
    ck = proj[:, kv_col:kv_col + 2 * NSA_HD].reshape(b, t // CMP_STRIDE, CMP_STRIDE, 2, NSA_HD)
    k_cmp, v_cmp = nsa_compress(ck, pe, wc1, wc2)
    o_c, sel = nsa_cmp_select(proj, q_col, k_cmp, v_cmp, b, t, 0, t // SEL_BLOCK)
    slopes = _alibi_slopes_np(NSA_HEADS)
    common = dict(heads=NSA_HEADS, dk=NSA_HD, dv=NSA_HD, k_off=0, v_off=NSA_HD, scale=NSA_HD ** -0.5, slopes=slopes)
    o_s = mqa_flash(proj, q_col, proj, kv_col + 2 * NSA_HD, 2 * NSA_HD, b, t, mode="select", sel=sel, **common)
    o_w = mqa_flash(proj, q_col, proj, win_col, 2 * NSA_HD, b, t, mode="window", **common)
    g = jax.nn.sigmoid(proj[:, gate_col:gate_col + 3 * NSA_HEADS]).reshape(n, NSA_HEADS, 3)
    sh = (n, NSA_HEADS, NSA_HD)
    o_nsa = (g[..., 0:1] * o_c.reshape(sh) + g[..., 1:2] * o_s.reshape(sh) + g[..., 2:3] * o_w.reshape(sh)).reshape(n, -1)
    return o_sb, sb_rows, o_nsa, nsa_rows, win_rows[:, -min(WINDOW, t):]


def _pad_rows(a, rows):
    return jnp.pad(a, ((0, 0), (0, rows - a.shape[1]), (0, 0)))


def _rows_head_major(a, b, t, heads):
    w = a.shape[1] // heads
    return jnp.transpose(a.reshape(b, t, heads, w), (0, 2, 1, 3)).reshape(b, heads * t, w)


def _rows_token_major(a, b, t, heads):
    w = a.shape[2]
    return jnp.transpose(a.reshape(b, heads, t, w), (0, 2, 1, 3)).reshape(b * t, heads * w)


def odd_mixers_sample(proj, b, t, past, page_table, pool_sb, pool_nsa, win_prior, pe, wc1, wc2):
    n = b * t
    page = pool_sb.shape[1]
    sb_w = SB_HEADS * SB_HD
    kv_w = SB_KV_HEADS * SB_HD
    rep = SB_HEADS // SB_KV_HEADS
    q_col = sb_w + 2 * kv_w
    kv_col = q_col + NSA_HEADS * NSA_HD
    win_col = kv_col + 4 * NSA_HD
    gate_col = win_col + 2 * NSA_HD
    assert past % page == 0 and past % SEL_BLOCK == 0 and t <= CMP_STRIDE and t <= page

    q = jnp.transpose(proj[:, :sb_w].reshape(b, t, SB_KV_HEADS, rep, SB_HD), (0, 2, 3, 1, 4))
    eye = jnp.eye(SB_KV_HEADS, dtype=F32)
    q_rows = (q[:, :, :, :, None, :] * eye[None, :, None, None, :, None]).reshape(b, SB_HEADS * t, kv_w)
    new_sk = _pad_rows(proj[:, sb_w:sb_w + kv_w].reshape(b, t, kv_w), page)
    new_sv = _pad_rows(proj[:, sb_w + kv_w:sb_w + 2 * kv_w].reshape(b, t, kv_w), page)
    o = sb_paged(q_rows, jnp.transpose(pool_sb, (0, 2, 3, 4, 1)), page_table, new_sk, new_sv, t, pg=32)
    o = o.reshape(b, SB_KV_HEADS, rep, t, SB_KV_HEADS, SB_HD)
    o = jnp.stack([o[:, g, :, :, g] for g in range(SB_KV_HEADS)], axis=1)
    o_sb = jnp.transpose(o, (0, 3, 1, 2, 4)).reshape(n, sb_w)
    sb_rows = proj[:, sb_w:sb_w + 2 * kv_w].reshape(b, t, 2, SB_KV_HEADS, SB_HD)

    nsa_rows = proj[:, kv_col:win_col].reshape(b, t, 4, NSA_HD)
    win_new = proj[:, win_col:gate_col].reshape(b, t, 2, NSA_HD)
    n_pages = page_table.shape[1]
    w_rows = _cmp_weight_halves(wc1).reshape(2, CMP_STRIDE, NSA_HD, 2 * CMP_HID).astype(BF16)
    pool_t = jnp.transpose(pool_nsa, (0, 2, 3, 1))
    pr = nsa_chunk_proj(pool_t, page_table, w_rows, pg=32)
    k_cmp, v_cmp = nsa_compress_tail([pr[:, 0], pr[:, 1]], pe, wc1, wc2)
    slopes_rows = jnp.repeat(jnp.asarray(_alibi_slopes_np(NSA_HEADS), F32), t).reshape(NSA_HEADS * t, 1)
    q_nsa = _rows_head_major(proj[:, q_col:kv_col], b, t, NSA_HEADS)
    n_sel = -(-(past + t) // SEL_BLOCK)
    o_c, sel = nsa_cmp_decode(q_nsa, slopes_rows, k_cmp, v_cmp, t, past, n_sel)
    common = dict(t_new=t, scale=NSA_HD ** -0.5, q_base=past)

    def new_cols(c0):
        return _pad_rows(proj[:, c0:c0 + NSA_HD].reshape(b, t, NSA_HD), page)

    per_page = page // SEL_BLOCK
    picked = sel[:, :, :n_pages * per_page].reshape(b, t, n_pages, per_page).sum(axis=(1, 3)) > 0.5
    cnt = picked.sum(axis=1).astype(jnp.int32)
    order = jnp.argsort(jnp.logical_not(picked), axis=1, stable=True).astype(jnp.int32)
    last = jnp.take_along_axis(order, jnp.maximum(cnt - 1, 0)[:, None], axis=1)
    vis = jnp.where(jnp.arange(n_pages)[None, :] < cnt[:, None], order, last)
    kv_pair = lambda r: (r[0, 0], r[0, 1])
    o_s = paged_attn(q_nsa, slopes_rows, pool_t, kv_pair, NSA_HD, page_table,
                     new_cols(kv_col + 2 * NSA_HD), new_cols(kv_col + 3 * NSA_HD), mode="select", kbase=0,
                     sel_rows=jnp.tile(sel, (1, NSA_HEADS, 1)), visit=(vis, cnt), pg=16,
                     pool_block=((2, NSA_HD, page), (1, 0, 0)), **common)
    pw = win_prior.shape[1]
    assert pw % page == 0 and pw <= past
    win_pool = jnp.transpose(win_prior.reshape(b, pw // page, page, 2, NSA_HD), (0, 1, 3, 4, 2))
    win_pool = win_pool.reshape(b * (pw // page), 2, NSA_HD, page)
    win_pt = jnp.arange(b * (pw // page), dtype=jnp.int32).reshape(b, pw // page)
    o_w = paged_attn(q_nsa, slopes_rows, win_pool, kv_pair, NSA_HD, win_pt,
                     new_cols(win_col), new_cols(win_col + NSA_HD), mode="window", kbase=past - pw, **common)
    g = jax.nn.sigmoid(proj[:, gate_col:gate_col + 3 * NSA_HEADS]).reshape(n, NSA_HEADS, 3)
    sh = (n, NSA_HEADS, NSA_HD)
    o_nsa = (g[..., 0:1] * _rows_token_major(o_c, b, t, NSA_HEADS).reshape(sh)
             + g[..., 1:2] * _rows_token_major(o_s, b, t, NSA_HEADS).reshape(sh)
             + g[..., 2:3] * _rows_token_major(o_w, b, t, NSA_HEADS).reshape(sh)).reshape(n, -1)
    win = jnp.concatenate([win_prior, win_new], axis=1)[:, -min(WINDOW, pw + t):]
    return o_sb, sb_rows, o_nsa, nsa_rows, win


def mla_attend(c_q, c_kv_raw, k_r_raw, b, t, start, g_cq, w_uq, g_ckv, w_uk, w_uv, paged=None):
    n = b * t
    pos = start + jnp.arange(t, dtype=jnp.int32)
    cos, sin = rope_angles(pos, ROPE_DIM)
    cos = jnp.tile(cos, (b, 1))
    sin = jnp.tile(sin, (b, 1))
    q = rms_matmul(c_q, w_uq.reshape(Q_RANK, -1), g=g_cq).reshape(n, MLA_HEADS, NOPE_DIM + ROPE_DIM)
    q_rope = apply_rope(q[..., NOPE_DIM:], cos[:, None, :], sin[:, None, :])
    q_lat = jnp.einsum('nhd,rhd->nhr', q[..., :NOPE_DIM], w_uk)
    new_rows = jnp.concatenate([rmsnorm(c_kv_raw, g_ckv), apply_rope(k_r_raw, cos, sin)], axis=-1)
    qf = jnp.concatenate([q_lat, q_rope], axis=-1).reshape(n, -1)
    dk = KV_RANK + ROPE_DIM
    scale = (NOPE_DIM + ROPE_DIM) ** -0.5
    if paged is None:
        o_lat = mqa_flash(qf, 0, new_rows, 0, dk, b, t, mode="causal", heads=MLA_HEADS, dk=dk, dv=KV_RANK,
                          k_off=0, v_off=0, scale=scale)
    else:
        pool, page_table = paged
        page = pool.shape[1]
        new_k = _pad_rows(new_rows.reshape(b, t, dk), page)
        o = paged_attn(_rows_head_major(qf, b, t, MLA_HEADS), jnp.zeros((MLA_HEADS * t, 1), F32),
                       jnp.transpose(pool, (0, 2, 1)),
                       lambda r: (r[0], r[0, :KV_RANK, :]), KV_RANK, page_table, new_k, new_k[:, :, :KV_RANK],
                       mode="causal", t_new=t, scale=scale, kbase=0, q_base=start, use_slopes=False, pg=64)
        o_lat = _rows_token_major(o, b, t, MLA_HEADS)
    out = jnp.einsum('nhr,rhd->nhd', o_lat.reshape(n, MLA_HEADS, KV_RANK), w_uv).reshape(n, MLA_HEADS * MLA_VDIM)
    return out, new_rows


def trunk(x, start, ml_state, mem_kv, p, caches=None):
    b, t, d = x.shape
    depth = p['g_mix'].shape[0]
    mla_rows, ml_c, ml_n, ml_m, sb_rows, nsa_rows, wins = [], [], [], [], [], [], []
    for l in range(depth):
        e = l // 2
        x2 = x.reshape(b * t, d)
        if l % 2 == 0:
            proj = rms_matmul(x2, p['w_in_even'][e], g=p['g_mix'][l])
            c_q, c_kv, k_r, mq, mk, mv, mi, mf, mo = split_cols(proj, EVEN_COLS)
            paged = None if caches is None else (caches['mla'][e], caches['page_table'])
            o_a, rows = mla_attend(c_q, c_kv, k_r, b, t, start, p['g_cq'][e], p['w_uq'][e], p['g_ckv'][e], p['w_uk'][e], p['w_uv'][e], paged=paged)
            rows = rows.reshape(b, t, -1)
            mq, mk, mv, mi, mf, mo = (a.reshape(b, t, -1) for a in (mq, mk, mv, mi, mf, mo))
            c0, n0, m0 = ml_state[e]
            o_b, c, n, m = mlstm_mixer(mq, mk, mv, mi + p['b_ml_i'][e], mf + p['b_ml_f'][e], mo, c0, n0, m0, p['g_mh'][e])
            mla_rows.append(rows)
            ml_c.append(c)
            ml_n.append(n)
            ml_m.append(m)
            mix = jnp.concatenate([o_a, o_b.reshape(b * t, -1)], axis=-1)
            x = rms_matmul(mix, p['w_out_even'][e], res=x2).reshape(b, t, d)
        else:
            proj = rms_matmul(x2, p['w_in_odd'][e], g=p['g_mix'][l], keep_pad=True)
            if caches is None:
                o_c, srows, o_d, nrows, win = odd_mixers_prompt(proj, b, t, p['nsa_pe'][e], p['nsa_wc1'][e], p['nsa_wc2'][e])
            else:
                o_c, srows, o_d, nrows, win = odd_mixers_sample(
                    proj, b, t, start, caches['page_table'], caches['sb'][e], caches['nsa'][e], caches['win'][e],
                    p['nsa_pe'][e], p['nsa_wc1'][e], p['nsa_wc2'][e])
            sb_rows.append(srows)
            nsa_rows.append(nrows)
            wins.append(win)
            mix = jnp.concatenate([o_c, o_d], axis=-1)
            x = rms_matmul(mix, p['w_out_odd'][e], res=x2).reshape(b, t, d)
        x = mem_attend(x, p['g_xattn'][l], mem_kv[l], p['w_xq'][l], p['w_xo'][l])
        x2 = x.reshape(b * t, d)
        x = (x2 + moe(x2, p['g_ffn'][l], p['w_rg'][l], p['b_rg'][l], p['w_re'][l], p['b_re'][l], p['w_e1'][l], p['w_e3'][l], p['w_e2'][l])).reshape(b, t, d)
    y = rmsnorm(x, p['g_final'])
    return y, mla_rows, ml_c, ml_n, ml_m, sb_rows, nsa_rows, wins


def kernel(x_prompt, x_sample, mem_prompt, cache_mla, state_mlstm_c, state_mlstm_n, state_mlstm_m, cache_sb_kv, cache_nsa_kv, state_nsa_win, cache_mem_kv, page_table, g_mix, w_in_even, b_ml_i, b_ml_f, g_cq, w_uq, g_ckv, w_uk, w_uv, g_mh, w_out_even, w_in_odd, nsa_pe, nsa_wc1, nsa_wc2, w_out_odd, g_xattn, g_memnorm, w_xq, w_xk, w_xv, w_xo, g_ffn, w_rg, b_rg, w_re, b_re, w_e1, w_e3, w_e2, g_final):
    p = dict(g_mix=g_mix, w_in_even=w_in_even, b_ml_i=b_ml_i, b_ml_f=b_ml_f, g_cq=g_cq, w_uq=w_uq, g_ckv=g_ckv, w_uk=w_uk, w_uv=w_uv, g_mh=g_mh, w_out_even=w_out_even, w_in_odd=w_in_odd, nsa_pe=nsa_pe, nsa_wc1=nsa_wc1, nsa_wc2=nsa_wc2, w_out_odd=w_out_odd, g_xattn=g_xattn, w_xq=w_xq, w_xo=w_xo, g_ffn=g_ffn, w_rg=w_rg, b_rg=b_rg, w_re=w_re, b_re=b_re, w_e1=w_e1, w_e3=w_e3, w_e2=w_e2, g_final=g_final)
    dt = x_prompt.dtype
    bp = x_prompt.shape[0]
    depth = g_mix.shape[0]
    n_even = (depth + 1) // 2
    n_odd = depth // 2
    mem_kv_list_p = [mem_kv_rows(mem_prompt, g_memnorm[l], w_xk[l], w_xv[l]) for l in range(depth)]
    y_prompt, mla_p, c_p, n_p, m_p, sb_p, nsa_p, win_pl = trunk(
        x_prompt, 0,
        [(jnp.zeros((bp, ML_HEADS, ML_V, ML_QK), F32), jnp.zeros((bp, ML_HEADS, ML_QK), F32), jnp.zeros((bp, ML_HEADS), F32)) for _ in range(n_even)],
        mem_kv_list_p, p)
    past_len = page_table.shape[1] * cache_mla.shape[2]
    caches = dict(page_table=page_table, mla=[cache_mla[e] for e in range(n_even)],
                  sb=[cache_sb_kv[e] for e in range(n_odd)], nsa=[cache_nsa_kv[e] for e in range(n_odd)],
                  win=[state_nsa_win[e] for e in range(n_odd)])
    y_sample, mla_s, c_s, n_s, m_s, sb_s, nsa_s, win_sl = trunk(
        x_sample, past_len,
        [(state_mlstm_c[e], state_mlstm_n[e], state_mlstm_m[e]) for e in range(n_even)],
        [cache_mem_kv[l] for l in range(depth)], p, caches=caches)
    return (y_prompt, y_sample, jnp.stack(mla_p), jnp.stack(mla_s), jnp.stack(c_p), jnp.stack(c_s),
            jnp.stack(n_p), jnp.stack(n_s), jnp.stack(m_p), jnp.stack(m_s), jnp.stack(sb_p), jnp.stack(sb_s),
            jnp.stack(nsa_p), jnp.stack(nsa_s), jnp.stack(win_pl), jnp.stack(win_sl), jnp.stack(mem_kv_list_p))
```
